```python
import math
import jax
import jax.numpy as jnp
from jax import lax
import numpy as np

D_MODEL = 1024
BATCH = 32
SEQ = 256
DEPTH = 2
DEC_BATCH = 8
DEC_SEQ = 2048
PAST_LEN = 256

GRID_W = 64
Q_BLOCK = 128
HEAD_DIM = 64
GA_HEADS = 8
GA_KV_HEADS = 2
NA_HEADS = 8
NA_WIN_ROWS = 8
NA_WIN_COLS = 16
SSM_WIDTH = 512
SSM_GROUP_CH = 16
SSM_GROUPS = SSM_WIDTH // SSM_GROUP_CH
SSM_STATE = 64
D_FF = 2816
N_BRANCH = 3
ROPE_THETA = 10000.0
EPS = 1e-6
STEP_MIN = 1e-3
STEP_MAX = 1e-1
GA_Q_W = GA_HEADS * HEAD_DIM
GA_KV_W = GA_KV_HEADS * HEAD_DIM
NA_W = NA_HEADS * HEAD_DIM
IN_SPLITS = (GA_Q_W, GA_KV_W, GA_KV_W, SSM_WIDTH, NA_W, NA_W, NA_W, N_BRANCH * D_MODEL)
IN_WIDTH = GA_Q_W + 2 * GA_KV_W + SSM_WIDTH + 3 * NA_W + N_BRANCH * D_MODEL

kernel_name = 'hybrid_flow_trunk_step'


def rmsnorm(x, g):
    xf = x.astype(jnp.float32)
    y = xf * lax.rsqrt(jnp.mean(xf * xf, axis=-1, keepdims=True) + EPS)
    return (y * g.astype(jnp.float32)).astype(x.dtype)


def adaln(cvec, w, b):
    m = jax.nn.silu(cvec) @ w + b
    return m.reshape(cvec.shape[0], 6, D_MODEL)


def axial_rope(x):
    B, L, H, dh = x.shape
    nf = dh // 4
    t = jnp.arange(L)
    pos = jnp.stack([t // GRID_W, t % GRID_W]).astype(jnp.float32)
    inv = ROPE_THETA ** (-jnp.arange(nf, dtype=jnp.float32) / nf)
    ang = (pos[:, :, None] * inv).transpose(1, 0, 2)[:, None]
    cos, sin = jnp.cos(ang), jnp.sin(ang)
    xf = x.astype(jnp.float32).reshape(B, L, H, 2, 2, nf)
    x1, x2 = xf[..., 0, :], xf[..., 1, :]
    out = jnp.stack([x1 * cos - x2 * sin, x1 * sin + x2 * cos], axis=-2)
    return out.reshape(B, L, H, dh).astype(x.dtype)


def block_attention(q, k, v):
    B, Lq, H, dh = q.shape
    G = k.shape[2]
    R = H // G
    nb = Lq // Q_BLOCK
    scale = dh ** -0.5
    qb = q.reshape(B, nb, Q_BLOCK, G, R, dh).swapaxes(0, 1)

    def one_block(q_i):
        s = jnp.einsum('bqgrd,bkgd->bgrqk', q_i, k).astype(jnp.float32) * scale
        p = jax.nn.softmax(s, axis=-1).astype(v.dtype)
        return jnp.einsum('bgrqk,bkgd->bqgrd', p, v)

    out = lax.map(one_block, qb)
    return out.swapaxes(0, 1).reshape(B, Lq, H * dh)


def neighbourhood_attention(q, k, v, kc, vc, rpb):
    B, L, H, dh = q.shape
    rows = L // GRID_W
    kr_n = min(NA_WIN_ROWS, rows)
    kc_n = NA_WIN_COLS
    K = kr_n * kc_n
    t = jnp.arange(L)
    r = t // GRID_W
    col = t % GRID_W
    r0 = jnp.clip(r - kr_n // 2, 0, rows - kr_n)
    c0 = jnp.clip(col - kc_n // 2, 0, GRID_W - kc_n)
    key_r = r0[:, None] + jnp.arange(kr_n)
    key_c = c0[:, None] + jnp.arange(kc_n)
    idx = (key_r[:, :, None] * GRID_W + key_c[:, None, :]).reshape(L, K)
    dr = (key_r - r[:, None] + NA_WIN_ROWS - 1)[:, :, None]
    dc = (key_c - col[:, None] + NA_WIN_COLS - 1)[:, None, :]
    bias = rpb[:, dr, dc].reshape(H, L, K)
    nb = L // Q_BLOCK
    scale = dh ** -0.5
    qb = q.reshape(B, nb, Q_BLOCK, H, dh).swapaxes(0, 1)
    idxb = idx.reshape(nb, Q_BLOCK, K)
    biasb = bias.reshape(H, nb, Q_BLOCK, K).swapaxes(0, 1)

    def one_block(args):
        q_i, idx_i, bias_i = args
        k_i = k[:, idx_i]
        v_i = v[:, idx_i]
        s_loc = jnp.einsum('bqhd,bqkhd->bhqk', q_i, k_i).astype(jnp.float32) * scale + bias_i.astype(jnp.float32)
        s_ctx = jnp.einsum('bqhd,bchd->bhqc', q_i, kc).astype(jnp.float32) * scale
        p = jax.nn.softmax(jnp.concatenate([s_loc, s_ctx], axis=-1), axis=-1).astype(v.dtype)
        return (jnp.einsum('bhqk,bqkhd->bqhd', p[..., :K], v_i)
                + jnp.einsum('bhqc,bchd->bqhd', p[..., K:], vc))

    out = lax.map(one_block, (qb, idxb, biasb))
    return out.swapaxes(0, 1).reshape(B, L, H * dh)


def linear_recurrence(bu, lam_bar, h0, reverse):
    a = jnp.broadcast_to(lam_bar, bu.shape)

    def combine(e1, e2):
        a1, b1 = e1
        a2, b2 = e2
        return a2 * a1, a2 * b1 + b2

    a_cum, b_cum = lax.associative_scan(combine, (a, bu), axis=1, reverse=reverse)
    return b_cum + a_cum * h0[:, None]


def s5_mixer(u, lp, h0):
    B, L, _ = u.shape
    f32 = jnp.float32
    ug = u.astype(f32).reshape(B, L, SSM_GROUPS, SSM_GROUP_CH).astype(jnp.complex64)
    outs = []
    finals = []
    for d in range(2):
        lam = lax.complex(lp['lam_re'][d].astype(f32), lp['lam_im'][d].astype(f32))
        dt = jnp.exp(lp['log_step'][d].astype(f32))[:, None]
        lam_bar = jnp.exp(lam * dt)
        b = lax.complex(lp['b_re'][d].astype(f32), lp['b_im'][d].astype(f32))
        b_bar = ((lam_bar - 1.0) / lam)[..., None] * b
        bu = jnp.einsum('blgh,gph->blgp', ug, b_bar)
        states = linear_recurrence(bu, lam_bar, h0[:, d], reverse=(d == 1))
        cmat = lax.complex(lp['c_re'][d].astype(f32), lp['c_im'][d].astype(f32))
        outs.append(jnp.real(jnp.einsum('blgp,ghp->blgh', states, cmat)))
        finals.append(states[:, L - 1] if d == 0 else states[:, 0])
    y = (outs[0] + outs[1]).reshape(B, L, SSM_WIDTH) + lp['ssm_d'].astype(f32) * u.astype(f32)
    g = jax.nn.gelu(y).astype(u.dtype)
    out = g * jax.nn.sigmoid(g @ lp['w_glu'])
    return out, jnp.stack(finals, axis=1)


def conv_ffn(h, w_up, conv_w, conv_b, w_down):
    u = h @ w_up
    up = jnp.pad(u, ((0, 0), (1, 1), (0, 0)))
    u = up[:, :-2] * conv_w[0] + up[:, 1:-1] * conv_w[1] + up[:, 2:] * conv_w[2] + conv_b
    a, g = jnp.split(u, 2, axis=-1)
    return (jax.nn.silu(g) * a) @ w_down


def trunk_layer(x, mod, lp, cache):
    B, L, _ = x.shape
    sh_m, sc_m, gt_m, sh_f, sc_f, gt_f = [mod[:, i][:, None] for i in range(6)]
    h = rmsnorm(x, lp['norm_g'][0]) * (1 + sc_m) + sh_m
    z = h @ lp['w_in']
    offs = []
    acc = 0
    for s in IN_SPLITS[:-1]:
        acc += s
        offs.append(acc)
    zq, zk, zv, zu, nq, nk, nv, zg = jnp.split(z, offs, axis=-1)
    qa = rmsnorm(zq.reshape(B, L, GA_HEADS, HEAD_DIM), lp['qk_g'][0])
    ka = rmsnorm(zk.reshape(B, L, GA_KV_HEADS, HEAD_DIM), lp['qk_g'][1])
    va = zv.reshape(B, L, GA_KV_HEADS, HEAD_DIM)
    qn = nq.reshape(B, L, NA_HEADS, HEAD_DIM)
    kn = nk.reshape(B, L, NA_HEADS, HEAD_DIM)
    vn = nv.reshape(B, L, NA_HEADS, HEAD_DIM)
    if cache is None:
        ya = block_attention(qa, ka, va)
        yc = block_attention(qn, kn, vn)
        h0 = jnp.zeros((B, 2, SSM_GROUPS, SSM_STATE), jnp.complex64)
        yb, fin = s5_mixer(zu, lp, h0)
        ctx_tensors = (ka, va, kn, vn, jnp.real(fin), jnp.imag(fin))
    else:
        ck_a, cv_a, ck_n, cv_n, s_re, s_im = cache
        qa = axial_rope(qa)
        ka = axial_rope(ka)
        ya = block_attention(qa, jnp.concatenate([ka, ck_a], axis=1), jnp.concatenate([va, cv_a], axis=1))
        yc = neighbourhood_attention(qn, kn, vn, ck_n, cv_n, lp['na_rpb'])
        h0 = lax.complex(s_re.astype(jnp.float32), s_im.astype(jnp.float32))
        yb, _ = s5_mixer(zu, lp, h0)
        ctx_tensors = None
    gates = jax.nn.sigmoid(zg.astype(jnp.float32)).astype(x.dtype).reshape(B, L, N_BRANCH, D_MODEL)
    merged = (gates[:, :, 0] * (ya @ lp['w_br_a']) + gates[:, :, 1] * (yb @ lp['w_br_b'])
              + gates[:, :, 2] * (yc @ lp['w_br_c']))
    x = x + gt_m * rmsnorm(merged @ lp['w_out'], lp['norm_g'][1])
    h = rmsnorm(x, lp['norm_g'][2]) * (1 + sc_f) + sh_f
    f = conv_ffn(h, lp['w_up'], lp['conv_w'], lp['conv_b'], lp['w_down'])
    x = x + gt_f * rmsnorm(f, lp['norm_g'][3])
    return x, ctx_tensors


def setup_inputs(seed: int = 0) -> dict:
    key = jax.random.key(seed)
    ks = jax.random.split(key, 40)
    f32 = jnp.float32
    D = D_MODEL
    G = SSM_GROUPS
    P = SSM_STATE
    Hg = SSM_GROUP_CH

    def nrm(k, shape, s):
        return s * jax.random.normal(k, shape, f32)

    return {
        'x_prompt': nrm(ks[0], (BATCH, SEQ, D), 1.0),
        'x_sample': nrm(ks[1], (DEC_BATCH, DEC_SEQ, D), 1.0),
        'c': nrm(ks[2], (DEC_BATCH, D), 1.0),
        'cache_ga_k': nrm(ks[3], (DEC_BATCH, DEPTH, PAST_LEN, GA_KV_HEADS, HEAD_DIM), 1.0),
        'cache_ga_v': nrm(ks[4], (DEC_BATCH, DEPTH, PAST_LEN, GA_KV_HEADS, HEAD_DIM), 1.0),
        'cache_na_k': nrm(ks[5], (DEC_BATCH, DEPTH, PAST_LEN, NA_HEADS, HEAD_DIM), 1.0),
        'cache_na_v': nrm(ks[6], (DEC_BATCH, DEPTH, PAST_LEN, NA_HEADS, HEAD_DIM), 1.0),
        'state_ssm_re': nrm(ks[7], (DEC_BATCH, DEPTH, 2, G, P), 0.1),
        'state_ssm_im': nrm(ks[8], (DEC_BATCH, DEPTH, 2, G, P), 0.1),
        'c_ctx': nrm(ks[9], (D,), 1.0),
        'w_mod': nrm(ks[10], (DEPTH, D, 6 * D), 0.5 * D ** -0.5),
        'b_mod': nrm(ks[11], (DEPTH, 6 * D), 0.01),
        'norm_g': 1.0 + nrm(ks[12], (DEPTH, 4, D), 0.01),
        'w_in': nrm(ks[13], (DEPTH, D, IN_WIDTH), D ** -0.5),
        'qk_norm_g': 1.0 + nrm(ks[14], (DEPTH, 2, HEAD_DIM), 0.01),
        'na_rpb': nrm(ks[15], (DEPTH, NA_HEADS, 2 * NA_WIN_ROWS - 1, 2 * NA_WIN_COLS - 1), 0.1),
        'ssm_lam_re': -0.5 + nrm(ks[16], (DEPTH, 2, G, P), 0.01),
        'ssm_lam_im': math.pi * jnp.arange(P, dtype=f32) + nrm(ks[17], (DEPTH, 2, G, P), 0.01),
        'ssm_log_step': jax.random.uniform(ks[18], (DEPTH, 2, G), f32, math.log(STEP_MIN), math.log(STEP_MAX)),
        'ssm_b_re': nrm(ks[19], (DEPTH, 2, G, P, Hg), (2 * Hg) ** -0.5),
        'ssm_b_im': nrm(ks[20], (DEPTH, 2, G, P, Hg), (2 * Hg) ** -0.5),
        'ssm_c_re': nrm(ks[21], (DEPTH, 2, G, Hg, P), P ** -0.5),
        'ssm_c_im': nrm(ks[22], (DEPTH, 2, G, Hg, P), P ** -0.5),
        'ssm_d': nrm(ks[23], (DEPTH, SSM_WIDTH), 1.0),
        'w_glu': nrm(ks[24], (DEPTH, SSM_WIDTH, SSM_WIDTH), SSM_WIDTH ** -0.5),
        'w_br_a': nrm(ks[25], (DEPTH, GA_Q_W, D), GA_Q_W ** -0.5),
        'w_br_b': nrm(ks[26], (DEPTH, SSM_WIDTH, D), SSM_WIDTH ** -0.5),
        'w_br_c': nrm(ks[27], (DEPTH, NA_W, D), NA_W ** -0.5),
        'w_out': nrm(ks[28], (DEPTH, D, D), D ** -0.5),
        'w_up': nrm(ks[29], (DEPTH, D, 2 * D_FF), D ** -0.5),
        'conv_w': nrm(ks[30], (DEPTH, 3, 2 * D_FF), 3 ** -0.5),
        'conv_b': nrm(ks[31], (DEPTH, 2 * D_FF), 0.01),
        'w_down': nrm(ks[32], (DEPTH, D_FF, D), D_FF ** -0.5),
    }


def reference(x_prompt, x_sample, c, cache_ga_k, cache_ga_v, cache_na_k, cache_na_v, state_ssm_re, state_ssm_im,
              c_ctx, w_mod, b_mod, norm_g, w_in, qk_norm_g, na_rpb, ssm_lam_re, ssm_lam_im, ssm_log_step,
              ssm_b_re, ssm_b_im, ssm_c_re, ssm_c_im, ssm_d, w_glu, w_br_a, w_br_b, w_br_c, w_out,
              w_up, conv_w, conv_b, w_down):
    y_p = x_prompt
    y_s = x_sample
    ga_k, ga_v, na_k, na_v, s_re, s_im = [], [], [], [], [], []
    for l in range(DEPTH):
        lp = {
            'norm_g': norm_g[l], 'w_in': w_in[l], 'qk_g': qk_norm_g[l], 'na_rpb': na_rpb[l],
            'lam_re': ssm_lam_re[l], 'lam_im': ssm_lam_im[l], 'log_step': ssm_log_step[l],
            'b_re': ssm_b_re[l], 'b_im': ssm_b_im[l], 'c_re': ssm_c_re[l], 'c_im': ssm_c_im[l],
            'ssm_d': ssm_d[l], 'w_glu': w_glu[l], 'w_br_a': w_br_a[l], 'w_br_b': w_br_b[l],
            'w_br_c': w_br_c[l], 'w_out': w_out[l], 'w_up': w_up[l], 'conv_w': conv_w[l],
            'conv_b': conv_b[l], 'w_down': w_down[l],
        }
        mod_ctx = adaln(c_ctx[None], w_mod[l], b_mod[l])
        mod_lat = adaln(c, w_mod[l], b_mod[l])
        y_p, ctx_t = trunk_layer(y_p, mod_ctx, lp, None)
        cache_l = (cache_ga_k[:, l], cache_ga_v[:, l], cache_na_k[:, l], cache_na_v[:, l],
                   state_ssm_re[:, l], state_ssm_im[:, l])
        y_s, _ = trunk_layer(y_s, mod_lat, lp, cache_l)
        ga_k.append(ctx_t[0])
        ga_v.append(ctx_t[1])
        na_k.append(ctx_t[2])
        na_v.append(ctx_t[3])
        s_re.append(ctx_t[4])
        s_im.append(ctx_t[5])
    new_ga_k = jnp.stack(ga_k, axis=1)
    new_ga_v = jnp.stack(ga_v, axis=1)
    new_na_k = jnp.stack(na_k, axis=1)
    new_na_v = jnp.stack(na_v, axis=1)
    new_ssm_re = jnp.stack(s_re, axis=1)
    new_ssm_im = jnp.stack(s_im, axis=1)
    return (y_p, y_s, new_ga_k, new_ga_v, new_na_k, new_na_v, new_ssm_re, new_ssm_im)
```

```python
import functools
import math

import numpy as np
import jax
import jax.numpy as jnp
from jax import lax
from jax.experimental import pallas as pl
from jax.experimental.pallas import tpu as pltpu

F32 = jnp.float32
MXU_DTYPE = jnp.bfloat16

D_MODEL = 1024
HEAD_DIM = 64
N_HEADS = 8
GA_KV_HEADS = 2
GRID_W = 64
NA_WIN_ROWS = 8
NA_WIN_COLS = 16
NA_KEY_ROWS = 10
SSM_WIDTH = 512
SSM_GROUPS = 32
SSM_GROUP_CH = 16
SSM_STATE = 64
SSM_CHUNK = 16
SSM_PAIRS = SSM_GROUPS // 2
D_FF = 2816
FF_TILE = 1408
ROPE_THETA = 10000.0
EPS = 1e-6
IN_WIDTH = 5888
NEG_BIG = -1e30

LANE = 128
TOKEN_TILE = 512
VMEM_LIMIT = 56 * 1024 * 1024

_Q0, _K0, _V0, _U0, _NQ0, _NK0, _NV0, _G0 = 0, 512, 640, 768, 1280, 1792, 2304, 2816


def _sigmoid(x):
    return 1.0 / (1.0 + jnp.exp(-x))


def _gelu_tanh(x):
    return 0.5 * x * (1.0 + jnp.tanh(math.sqrt(2.0 / math.pi) * (x + 0.044715 * (x * x * x))))


def _rms(x, g):
    ms = jnp.mean(x * x, axis=-1, keepdims=True)
    return (x * lax.rsqrt(ms + EPS)) * g


def _mm(a, b):
    return jnp.dot(a.astype(MXU_DTYPE), b.astype(MXU_DTYPE), preferred_element_type=F32)


def _mm_nt(a, b):
    return lax.dot_general(a.astype(MXU_DTYPE), b.astype(MXU_DTYPE), (((1,), (1,)), ((), ())),
                           preferred_element_type=F32)


def _params(sem):
    return pltpu.CompilerParams(dimension_semantics=sem, vmem_limit_bytes=VMEM_LIMIT)


def _mod_kernel(c_ref, w_ref, b_ref, o_ref):
    c = c_ref[...]
    o_ref[...] = _mm(c * _sigmoid(c), w_ref[...]) + b_ref[...]


def _modulation(cvec, w_mod, b_mod):
    depth = w_mod.shape[0]
    rows = cvec.shape[0]
    tn = 1536
    return pl.pallas_call(
        _mod_kernel,
        grid=(depth, 6 * D_MODEL // tn),
        in_specs=[
            pl.BlockSpec((rows, D_MODEL), lambda l, j: (0, 0)),
            pl.BlockSpec((None, D_MODEL, tn), lambda l, j: (l, 0, j)),
            pl.BlockSpec((None, 1, tn), lambda l, j: (l, 0, j)),
        ],
        out_specs=pl.BlockSpec((None, rows, tn), lambda l, j: (l, 0, j)),
        out_shape=jax.ShapeDtypeStruct((depth, rows, 6 * D_MODEL), F32),
        compiler_params=_params(("parallel", "parallel")),
        name="adaln_mod",
    )(cvec, w_mod, b_mod.reshape(depth, 1, 6 * D_MODEL))


def _head_rms(z, seg, gain):
    ms = jnp.dot((z * z).astype(MXU_DTYPE), seg, preferred_element_type=F32)
    return (z * lax.rsqrt(ms + EPS)) * gain


def _rope_tile(t, c, s_up, s_dn):
    return t * c + pltpu.roll(t, LANE - 16, 1) * s_up + pltpu.roll(t, 16, 1) * s_dn


def _inproj_kernel(*refs, rope):
    if rope:
        (x_ref, mod_ref, ng_ref, w_ref, qg_ref, kg_ref, seg_ref, cos_ref, sup_ref, sdn_ref,
         q_o, k_o, v_o, u_o, nq_o, nk_o, nv_o, g_o) = refs
    else:
        (x_ref, mod_ref, ng_ref, w_ref, qg_ref, kg_ref, seg_ref,
         q_o, k_o, v_o, u_o, nq_o, nk_o, nv_o, g_o) = refs
    x = x_ref[...]
    h = _rms(x, ng_ref[0:1, :]) * (1.0 + mod_ref[1:2, :]) + mod_ref[0:1, :]
    hb = h.astype(MXU_DTYPE)
    scale = HEAD_DIM ** -0.5

    def proj(lo, width):
        return jnp.dot(hb, w_ref[:, lo:lo + width], preferred_element_type=F32)

    def maybe_rope(z):
        if not rope:
            return z
        c, su, sd = cos_ref[...], sup_ref[...], sdn_ref[...]
        tiles = [_rope_tile(z[:, i * LANE:(i + 1) * LANE], c, su, sd) for i in range(z.shape[1] // LANE)]
        return tiles[0] if len(tiles) == 1 else jnp.concatenate(tiles, axis=1)

    q = maybe_rope(_head_rms(proj(_Q0, 512), seg_ref[...], qg_ref[...]))
    q_o[...] = (q * scale).astype(q_o.dtype)
    k = maybe_rope(_head_rms(proj(_K0, 128), seg_ref[0:LANE, 0:LANE], kg_ref[...]))
    k_o[...] = k.astype(k_o.dtype)
    v_o[...] = proj(_V0, 128).astype(v_o.dtype)
    u_o[...] = proj(_U0, 512).astype(u_o.dtype)
    nq_o[...] = (proj(_NQ0, 512) * scale).astype(nq_o.dtype)
    nk_o[...] = proj(_NK0, 512).astype(nk_o.dtype)
    nv_o[...] = proj(_NV0, 512).astype(nv_o.dtype)
    for i in range(3):
        g_o[:, i * D_MODEL:(i + 1) * D_MODEL] = _sigmoid(proj(_G0 + i * D_MODEL, D_MODEL)).astype(g_o.dtype)


def _in_projection(x2d, mod_l, mod_row, ng, w_in, qg, kg, seg, rope_tabs, seq_len, kv_dtype):
    t = x2d.shape[0]
    tm = TOKEN_TILE
    tiles_per_seq = max(seq_len // tm, 1)
    rope = rope_tabs is not None
    row = lambda i: (i, 0)
    const = lambda i: (0, 0)
    in_specs = [
        pl.BlockSpec((tm, D_MODEL), row),
        pl.BlockSpec((None, 6, D_MODEL), lambda i: (mod_row(i), 0, 0)),
        pl.BlockSpec((4, D_MODEL), const),
        pl.BlockSpec((D_MODEL, IN_WIDTH), const),
        pl.BlockSpec((1, 512), const),
        pl.BlockSpec((1, LANE), const),
        pl.BlockSpec((512, 512), const),
    ]
    args = [x2d, mod_l, ng, w_in, qg, kg, seg]
    if rope:
        in_specs += [pl.BlockSpec((tm, LANE), lambda i: (i % tiles_per_seq, 0))] * 3
        args += list(rope_tabs)
    widths = (512, 128, 128, 512, 512, 512, 512, 3 * D_MODEL)
    dtypes = (MXU_DTYPE, kv_dtype, kv_dtype, F32, MXU_DTYPE, kv_dtype, kv_dtype, MXU_DTYPE)
    return pl.pallas_call(
        functools.partial(_inproj_kernel, rope=rope),
        grid=(t // tm,),
        in_specs=in_specs,
        out_specs=[pl.BlockSpec((tm, w), row) for w in widths],
        out_shape=[jax.ShapeDtypeStruct((t, w), dt) for w, dt in zip(widths, dtypes)],
        compiler_params=_params(("parallel",)),
        name="in_proj_rope" if rope else "in_proj",
    )(*args)


def _lane_masks(dtype):
    lane = lax.broadcasted_iota(jnp.int32, (1, LANE), 1)
    lo = lane < HEAD_DIM
    return lo, lo.astype(dtype), (~lo).astype(dtype)


def _attn_kernel(*refs, kv_tiles, cached):
    if cached:
        q_ref, k_ref, v_ref, kc_ref, vc_ref, o_ref = refs
    else:
        q_ref, k_ref, v_ref, o_ref = refs
    tq = q_ref.shape[0]
    lo, m_lo, m_hi = _lane_masks(MXU_DTYPE)
    n_pairs = N_HEADS // 2
    pairs_per_kv = n_pairs // kv_tiles
    for kt in range(kv_tiles):
        ksl = slice(kt * LANE, (kt + 1) * LANE)
        pairs = [kt * pairs_per_kv + j for j in range(pairs_per_kv)]
        rows = []
        for hp in pairs:
            q2 = q_ref[:, hp * LANE:(hp + 1) * LANE]
            rows += [q2 * m_lo, q2 * m_hi]
        qs = jnp.concatenate(rows, axis=0)
        s = _mm_nt(qs, k_ref[:, ksl])
        m = jnp.max(s, axis=-1, keepdims=True)
        if cached:
            s2 = _mm_nt(qs, kc_ref[:, ksl])
            m = jnp.maximum(m, jnp.max(s2, axis=-1, keepdims=True))
        p = jnp.exp(s - m)
        l = jnp.sum(p, axis=-1, keepdims=True)
        o = _mm(p, v_ref[:, ksl])
        if cached:
            p2 = jnp.exp(s2 - m)
            l = l + jnp.sum(p2, axis=-1, keepdims=True)
            o = o + _mm(p2, vc_ref[:, ksl])
        o = o * (1.0 / l)
        for j, hp in enumerate(pairs):
            o_e = o[(2 * j) * tq:(2 * j + 1) * tq]
            o_o = o[(2 * j + 1) * tq:(2 * j + 2) * tq]
            o_ref[:, hp * LANE:(hp + 1) * LANE] = jnp.where(lo, o_e, o_o).astype(o_ref.dtype)


def _attention(q, k, v, kc, vc, tq, name):
    b, lq, _ = q.shape
    lk, kw = k.shape[1], k.shape[2]
    cached = kc is not None
    qmap = lambda bi, ti: (bi, ti, 0)
    kmap = lambda bi, ti: (bi, 0, 0)
    in_specs = [
        pl.BlockSpec((None, tq, 512), qmap),
        pl.BlockSpec((None, lk, kw), kmap),
        pl.BlockSpec((None, lk, kw), kmap),
    ]
    args = [q, k, v]
    if cached:
        lc = kc.shape[1]
        in_specs += [pl.BlockSpec((None, lc, kw), kmap)] * 2
        args += [kc, vc]
    return pl.pallas_call(
        functools.partial(_attn_kernel, kv_tiles=kw // LANE, cached=cached),
        grid=(b, lq // tq),
        in_specs=in_specs,
        out_specs=pl.BlockSpec((None, tq, 512), qmap),
        out_shape=jax.ShapeDtypeStruct((b, lq, 512), MXU_DTYPE),
        compiler_params=_params(("parallel", "parallel")),
        name=name,
    )(*args)


def _na_geometry(seq_len):
    rows = seq_len // GRID_W
    n_tiles = rows // 2
    assert rows >= NA_KEY_ROWS and NA_WIN_ROWS <= rows
    ws = np.clip(2 * np.arange(n_tiles) - NA_WIN_ROWS // 2, 0, rows - NA_KEY_ROWS)
    ql = np.arange(2 * GRID_W)
    kl = np.arange(NA_KEY_ROWS * GRID_W)
    r = 2 * np.arange(n_tiles)[:, None] + (ql // GRID_W)[None, :]
    c = (ql % GRID_W)[None, :]
    key_r = ws[:, None] + (kl // GRID_W)[None, :]
    key_c = (kl % GRID_W)[None, :]
    r0 = np.clip(r - NA_WIN_ROWS // 2, 0, rows - NA_WIN_ROWS)
    c0 = np.clip(c - NA_WIN_COLS // 2, 0, GRID_W - NA_WIN_COLS)
    valid = ((key_r[:, None, :] >= r0[:, :, None]) & (key_r[:, None, :] < r0[:, :, None] + NA_WIN_ROWS)
             & (key_c[:, None, :] >= c0[:, :, None]) & (key_c[:, None, :] < c0[:, :, None] + NA_WIN_COLS))
    dr = np.clip(key_r[:, None, :] - r[:, :, None] + NA_WIN_ROWS - 1, 0, 2 * NA_WIN_ROWS - 2)
    dc = np.clip(key_c[:, None, :] - c[:, :, None] + NA_WIN_COLS - 1, 0, 2 * NA_WIN_COLS - 2)
    dc = np.broadcast_to(dc, dr.shape)
    table_of_tile, reps = [], []
    for i in range(n_tiles):
        for ti, j in enumerate(reps):
            if (np.array_equal(valid[i], valid[j]) and np.array_equal(dr[i][valid[i]], dr[j][valid[j]])
                    and np.array_equal(dc[i][valid[i]], dc[j][valid[j]])):
                table_of_tile.append(ti)
                break
        else:
            table_of_tile.append(len(reps))
            reps.append(i)
    reps = np.array(reps)
    return ws, np.array(table_of_tile), valid[reps], dr[reps], dc[reps]


def _na_kernel(ws_ref, tab_ref, q_ref, k_ref, v_ref, kc_ref, vc_ref, bias_ref, o_ref):
    i = pl.program_id(1)
    start = pl.multiple_of(ws_ref[i] * GRID_W, GRID_W)
    nk = NA_KEY_ROWS * GRID_W
    tq = q_ref.shape[0]
    lo, m_lo, m_hi = _lane_masks(MXU_DTYPE)
    for hp in range(N_HEADS // 2):
        sl = slice(hp * LANE, (hp + 1) * LANE)
        q2 = q_ref[:, sl]
        qs = jnp.concatenate([q2 * m_lo, q2 * m_hi], axis=0)
        kw = k_ref[pl.ds(start, nk), sl]
        vw = v_ref[pl.ds(start, nk), sl]
        bias = jnp.concatenate([bias_ref[2 * hp], bias_ref[2 * hp + 1]], axis=0)
        s = _mm_nt(qs, kw) + bias
        s2 = _mm_nt(qs, kc_ref[:, sl])
        m = jnp.maximum(jnp.max(s, axis=-1, keepdims=True), jnp.max(s2, axis=-1, keepdims=True))
        p = jnp.exp(s - m)
        p2 = jnp.exp(s2 - m)
        l = jnp.sum(p, axis=-1, keepdims=True) + jnp.sum(p2, axis=-1, keepdims=True)
        o = (_mm(p, vw) + _mm(p2, vc_ref[:, sl])) * (1.0 / l)
        o_ref[:, sl] = jnp.where(lo, o[:tq], o[tq:]).astype(o_ref.dtype)


def _neighbourhood_attention(q, k, v, kc, vc, rpb):
    b, seq_len, _ = q.shape
    lc = kc.shape[1]
    ws, table_of_tile, valid, dr, dc = _na_geometry(seq_len)
    bias = jnp.where(jnp.asarray(valid)[None], rpb[:, dr, dc], NEG_BIG)
    bias = jnp.transpose(bias, (1, 0, 2, 3)).astype(F32)
    n_tiles = len(ws)
    tq = 2 * GRID_W
    nk = NA_KEY_ROWS * GRID_W
    qmap = lambda bi, ti, ws_r, tab_r: (bi, ti, 0)
    kmap = lambda bi, ti, ws_r, tab_r: (bi, 0, 0)
    grid_spec = pltpu.PrefetchScalarGridSpec(
        num_scalar_prefetch=2,
        grid=(b, n_tiles),
        in_specs=[
            pl.BlockSpec((None, tq, 512), qmap),
            pl.BlockSpec((None, seq_len, 512), kmap),
            pl.BlockSpec((None, seq_len, 512), kmap),
            pl.BlockSpec((None, lc, 512), kmap),
            pl.BlockSpec((None, lc, 512), kmap),
            pl.BlockSpec((None, N_HEADS, tq, nk), lambda bi, ti, ws_r, tab_r: (tab_r[ti], 0, 0, 0)),
        ],
        out_specs=pl.BlockSpec((None, tq, 512), qmap),
    )
    return pl.pallas_call(
        _na_kernel,
        grid_spec=grid_spec,
        out_shape=jax.ShapeDtypeStruct((b, seq_len, 512), MXU_DTYPE),
        compiler_params=_params(("parallel", "arbitrary")),
        name="na_attn",
    )(jnp.asarray(ws, jnp.int32), jnp.asarray(table_of_tile, jnp.int32), q, k, v, kc, vc, bias)


def _ssm_prep_kernel(lrc_ref, lic_ref, ls_ref, lrr_ref, lir_ref, btr_ref, bti_ref,
                     bre_ref, bim_ref, cre_ref, cim_ref,
                     kt_o, pbr_o, pbi_o, cpr_o, cpi_o, l16r_o, l16i_o):
    nb = lrc_ref.shape[0]
    dt = jnp.exp(ls_ref[...])
    lr, li = lrc_ref[...], lic_ref[...]
    ar, ai = lr * dt, li * dt
    tau = (lax.broadcasted_iota(jnp.int32, (1, 1, SSM_CHUNK * SSM_GROUP_CH), 2) // SSM_GROUP_CH).astype(F32)
    mag = jnp.exp(tau * ar)
    ang = tau * ai
    pwr, pwi = mag * jnp.cos(ang), mag * jnp.sin(ang)
    e1 = jnp.exp(ar)
    lbr, lbi = e1 * jnp.cos(ai), e1 * jnp.sin(ai)

    def zoh_coef(lr_, li_, lbr_, lbi_):
        nr, ni = lbr_ - 1.0, lbi_
        den = 1.0 / (lr_ * lr_ + li_ * li_)
        return (nr * lr_ + ni * li_) * den, (ni * lr_ - nr * li_) * den

    cfr, cfi = zoh_coef(lr, li, lbr, lbi)
    bre, bim = bre_ref[...], bim_ref[...]
    bbr, bbi = cfr * bre - cfi * bim, cfr * bim + cfi * bre
    pbr_o[...] = pwr * bbr - pwi * bbi
    pbi_o[...] = pwr * bbi + pwi * bbr
    cre, cim = cre_ref[...], cim_ref[...]
    c0r, c0i = cre * pwr - cim * pwi, cre * pwi + cim * pwr
    cpr_o[...] = c0r * lbr - c0i * lbi
    cpi_o[...] = c0r * lbi + c0i * lbr
    m16 = jnp.exp(float(SSM_CHUNK) * ar)
    a16 = float(SSM_CHUNK) * ai
    l16r_o[...] = jnp.broadcast_to(m16 * jnp.cos(a16), l16r_o.shape)
    l16i_o[...] = jnp.broadcast_to(m16 * jnp.sin(a16), l16i_o.shape)
    lrr, lir = lrr_ref[...], lir_ref[...]
    arr, air = lrr * dt, lir * dt
    e1r = jnp.exp(arr)
    rfr, rfi = zoh_coef(lrr, lir, e1r * jnp.cos(air), e1r * jnp.sin(air))
    btr, bti = btr_ref[...], bti_ref[...]
    tbr, tbi = rfr * btr - rfi * bti, rfr * bti + rfi * btr
    for n in range(nb):
        kt_o[n] = (jnp.dot(tbr[n], c0r[n], preferred_element_type=F32, precision=lax.Precision.HIGHEST)
                   - jnp.dot(tbi[n], c0i[n], preferred_element_type=F32, precision=lax.Precision.HIGHEST))


def _ssm_operators(lam_re, lam_im, log_step, b_re, b_im, c_re, c_im):
    depth = lam_re.shape[0]
    n = depth * 2 * SSM_GROUPS
    p, hg, tc = SSM_STATE, SSM_GROUP_CH, SSM_CHUNK
    w = tc * hg
    nb = 8
    lam_re, lam_im = lam_re.reshape(n, p).astype(F32), lam_im.reshape(n, p).astype(F32)
    b_re, b_im = b_re.reshape(n, p, hg).astype(F32), b_im.reshape(n, p, hg).astype(F32)
    ct_re = jnp.swapaxes(c_re.reshape(n, hg, p), 1, 2).astype(F32)
    ct_im = jnp.swapaxes(c_im.reshape(n, hg, p), 1, 2).astype(F32)
    args = [
        lam_re.reshape(n, p, 1), lam_im.reshape(n, p, 1), log_step.reshape(n, 1, 1).astype(F32),
        lam_re.reshape(n, 1, p), lam_im.reshape(n, 1, p),
        jnp.swapaxes(b_re, 1, 2), jnp.swapaxes(b_im, 1, 2),
        jnp.tile(b_re, (1, 1, tc)), jnp.tile(b_im, (1, 1, tc)),
        jnp.tile(ct_re, (1, 1, tc)), jnp.tile(ct_im, (1, 1, tc)),
    ]
    blk = lambda a: pl.BlockSpec((nb,) + a.shape[1:], lambda i: (i, 0, 0))
    out_shapes = [(n, hg, w), (n, p, w), (n, p, w), (n, p, w), (n, p, w), (n, p, LANE), (n, p, LANE)]
    kt, pbr, pbi, cpr, cpi, l16r, l16i = pl.pallas_call(
        _ssm_prep_kernel,
        grid=(n // nb,),
        in_specs=[blk(a) for a in args],
        out_specs=[pl.BlockSpec((nb,) + s[1:], lambda i: (i, 0, 0)) for s in out_shapes],
        out_shape=[jax.ShapeDtypeStruct(s, F32) for s in out_shapes],
        compiler_params=_params(("parallel",)),
        name="ssm_prep",
    )(*args)

    g, q = SSM_GROUPS, SSM_PAIRS
    eye2 = jnp.eye(2, dtype=F32)
    s_idx, j_idx = np.arange(tc)[:, None], np.arange(tc)[None, :]
    kt = kt.reshape(depth, 2, g, hg, tc, hg)

    def toeplitz(k, lag):
        t = jnp.take(k, jnp.asarray(np.clip(lag, 0, tc - 1).reshape(-1)), axis=3)
        t = t.reshape(depth, g, hg, tc, tc, hg) * jnp.asarray((lag >= 0).astype(np.float32))[None, None, None, :, :, None]
        return t

    t = toeplitz(kt[:, 0], j_idx - s_idx) + toeplitz(kt[:, 1], s_idx - j_idx)
    t = jnp.transpose(t, (0, 1, 3, 2, 4, 5)).reshape(depth, q, 2, tc, hg, tc, hg)
    w_t = jnp.einsum('lqashjo,ab->lqsahjbo', t, eye2).reshape(depth, q, 2 * w, 2 * w)

    def inject(pb, flip):
        pb = pb.reshape(depth, g, p, tc, hg)
        if flip:
            pb = pb[:, :, :, ::-1, :]
        pb = jnp.transpose(pb, (0, 1, 3, 4, 2)).reshape(depth, q, 2, tc, hg, p)
        return jnp.einsum('lqashp,ab->lqsahbp', pb, eye2).reshape(depth, q, 2 * w, 2 * p)

    pbr, pbi = pbr.reshape(depth, 2, g, p, w), pbi.reshape(depth, 2, g, p, w)
    w_b = jnp.concatenate([inject(pbr[:, 0], True), inject(pbi[:, 0], True),
                           inject(pbr[:, 1], False), inject(pbi[:, 1], False)], axis=-1)

    def readout(cp, flip, sign):
        cp = cp.reshape(depth, q, 2, p, tc, hg)
        if flip:
            cp = cp[:, :, :, :, ::-1, :]
        return sign * jnp.einsum('lqapjo,ab->lqapjbo', cp, eye2).reshape(depth, q, 2 * p, 2 * w)

    cpr, cpi = cpr.reshape(depth, 2, g, p, w), cpi.reshape(depth, 2, g, p, w)
    w_c = jnp.concatenate([readout(cpr[:, 0], False, 1.0), readout(cpi[:, 0], False, -1.0),
                           readout(cpr[:, 1], True, 1.0), readout(cpi[:, 1], True, -1.0)], axis=-2)

    l16 = jnp.stack([l16r[:, :, 0], l16i[:, :, 0]], axis=0).reshape(2, depth, 2, q, 2 * p)
    l16 = jnp.transpose(l16, (1, 3, 2, 0, 4)).reshape(depth, q, 4, 2 * p)
    l16 = jnp.pad(l16, ((0, 0), (0, 0), (0, 4), (0, 0)))
    return w_t.astype(MXU_DTYPE), w_b.astype(MXU_DTYPE), w_c.astype(MXU_DTYPE), l16


def _ssm_kernel(u_ref, wt_ref, wb_ref, wc_ref, l16_ref, h0_ref, y_o, fin_o, dx_ref, xs_ref, *, batch):
    nrows = u_ref.shape[0]
    n_chunks = nrows // batch
    u = u_ref[...]
    y_intra = jnp.dot(u, wt_ref[...], preferred_element_type=F32)
    dx_ref[...] = jnp.dot(u, wb_ref[...], preferred_element_type=F32)
    lfr, lfi, lbr, lbi = (l16_ref[r:r + 1, :] for r in range(4))
    h0 = h0_ref[...]
    sp = LANE

    def body(c, carry):
        fr, fi, br, bi = carry
        rf = pl.multiple_of(c * batch, batch)
        rb = pl.multiple_of((n_chunks - 1 - c) * batch, batch)
        xs_ref[pl.ds(rf, batch), 0:sp] = fr
        xs_ref[pl.ds(rf, batch), sp:2 * sp] = fi
        xs_ref[pl.ds(rb, batch), 2 * sp:3 * sp] = br
        xs_ref[pl.ds(rb, batch), 3 * sp:4 * sp] = bi
        nfr = lfr * fr - lfi * fi + dx_ref[pl.ds(rf, batch), 0:sp]
        nfi = lfr * fi + lfi * fr + dx_ref[pl.ds(rf, batch), sp:2 * sp]
        nbr = lbr * br - lbi * bi + dx_ref[pl.ds(rb, batch), 2 * sp:3 * sp]
        nbi = lbr * bi + lbi * br + dx_ref[pl.ds(rb, batch), 3 * sp:4 * sp]
        return nfr, nfi, nbr, nbi

    fin = lax.fori_loop(0, n_chunks, body, tuple(h0[:, r * sp:(r + 1) * sp] for r in range(4)))
    for r in range(4):
        fin_o[:, r * sp:(r + 1) * sp] = fin[r]
    y_o[...] = (y_intra + jnp.dot(xs_ref[...].astype(MXU_DTYPE), wc_ref[...],
                                  preferred_element_type=F32)).astype(y_o.dtype)


def _ssm_scan(zu, w_t, w_b, w_c, l16, h0):
    b, seq_len, _ = zu.shape
    q, tc = SSM_PAIRS, SSM_CHUNK
    nc = seq_len // tc
    pw = SSM_WIDTH // q
    u = zu.astype(MXU_DTYPE).reshape(b, nc, tc, q, pw)
    u = jnp.transpose(u, (3, 1, 0, 2, 4)).reshape(q, nc * b, tc * pw)
    wspec = pl.BlockSpec((None, 512, 512), lambda i: (i, 0, 0))
    y, fin = pl.pallas_call(
        functools.partial(_ssm_kernel, batch=b),
        grid=(q,),
        in_specs=[
            pl.BlockSpec((None, nc * b, 512), lambda i: (i, 0, 0)),
            wspec, wspec, wspec,
            pl.BlockSpec((None, 8, LANE), lambda i: (i, 0, 0)),
            pl.BlockSpec((None, b, 512), lambda i: (i, 0, 0)),
        ],
        out_specs=[
            pl.BlockSpec((None, nc * b, 512), lambda i: (i, 0, 0)),
            pl.BlockSpec((None, b, 512), lambda i: (i, 0, 0)),
        ],
        out_shape=[jax.ShapeDtypeStruct((q, nc * b, 512), MXU_DTYPE),
                   jax.ShapeDtypeStruct((q, b, 512), F32)],
        scratch_shapes=[pltpu.VMEM((nc * b, 512), F32), pltpu.VMEM((nc * b, 512), F32)],
        compiler_params=_params(("parallel",)),
        name="ssm_scan",
    )(u, w_t, w_b, w_c, l16, h0)
    y = jnp.transpose(y.reshape(q, nc, b, tc, pw), (2, 1, 3, 0, 4)).reshape(b, seq_len, SSM_WIDTH)
    return y, fin


def _pack_state(s_re, s_im):
    b = s_re.shape[0]
    a = jnp.stack([s_re, s_im], axis=2).reshape(b, 2, 2, SSM_PAIRS, 2 * SSM_STATE)
    return jnp.transpose(a, (3, 0, 1, 2, 4)).reshape(SSM_PAIRS, b, 4 * 2 * SSM_STATE).astype(F32)


def _unpack_state(fin):
    b = fin.shape[1]
    a = jnp.transpose(fin.reshape(SSM_PAIRS, b, 2, 2, 2 * SSM_STATE), (1, 2, 3, 0, 4))
    a = a.reshape(b, 2, 2, SSM_GROUPS, SSM_STATE)
    return a[:, :, 0], a[:, :, 1]


def _merge_kernel(ya_ref, ys_ref, u_ref, yc_ref, g_ref, x_ref, mod_ref, ng_ref, d_ref,
                  wglu_ref, wa_ref, wb_ref, wc_ref, wo_ref, x_o, h_o):
    y = ys_ref[...].astype(F32) + d_ref[...] * u_ref[...]
    gl = _gelu_tanh(y)
    yb = gl * _sigmoid(_mm(gl, wglu_ref[...]))
    merged = (g_ref[:, 0:D_MODEL].astype(F32) * _mm(ya_ref[...], wa_ref[...])
              + g_ref[:, D_MODEL:2 * D_MODEL].astype(F32) * _mm(yb, wb_ref[...])
              + g_ref[:, 2 * D_MODEL:3 * D_MODEL].astype(F32) * _mm(yc_ref[...], wc_ref[...]))
    x1 = x_ref[...] + mod_ref[2:3, :] * _rms(_mm(merged, wo_ref[...]), ng_ref[1:2, :])
    x_o[...] = x1
    h_o[...] = (_rms(x1, ng_ref[2:3, :]) * (1.0 + mod_ref[4:5, :]) + mod_ref[3:4, :]).astype(h_o.dtype)


def _merge(ya, ys, zu, yc, gates, x2d, mod_l, mod_row, ng, ssm_d, w_glu, w_a, w_b, w_c, w_o):
    t = x2d.shape[0]
    tm = TOKEN_TILE
    row = lambda i: (i, 0)
    const = lambda i: (0, 0)
    r512 = pl.BlockSpec((tm, 512), row)
    wbr = pl.BlockSpec((512, D_MODEL), const)
    return pl.pallas_call(
        _merge_kernel,
        grid=(t // tm,),
        in_specs=[
            r512, r512, r512, r512,
            pl.BlockSpec((tm, 3 * D_MODEL), row),
            pl.BlockSpec((tm, D_MODEL), row),
            pl.BlockSpec((None, 6, D_MODEL), lambda i: (mod_row(i), 0, 0)),
            pl.BlockSpec((4, D_MODEL), const),
            pl.BlockSpec((1, 512), const),
            pl.BlockSpec((512, 512), const),
            wbr, wbr, wbr,
            pl.BlockSpec((D_MODEL, D_MODEL), const),
        ],
        out_specs=[pl.BlockSpec((tm, D_MODEL), row), pl.BlockSpec((tm, D_MODEL), row)],
        out_shape=[jax.ShapeDtypeStruct((t, D_MODEL), F32), jax.ShapeDtypeStruct((t, D_MODEL), MXU_DTYPE)],
        compiler_params=_params(("parallel",)),
        name="merge",
    )(ya, ys, zu, yc, gates, x2d, mod_l, ng, ssm_d, w_glu, w_a, w_b, w_c, w_o)


def _ffn_kernel(h_ref, hp_ref, hn_ref, x_ref, mod_ref, ng_ref, wu_ref, cw_ref, wd_ref, x_o,
                acc_ref, u_ref, act_ref, *, seq_len):
    i, j = pl.program_id(0), pl.program_id(1)
    tm = h_ref.shape[0]
    ft = wd_ref.shape[0]
    n = tm + 16
    hh = jnp.concatenate([hp_ref[...], h_ref[...], hn_ref[...]], axis=0)
    u_ref[...] = jnp.dot(hh, wu_ref[...], preferred_element_type=F32)
    pos = jnp.bitwise_and(i * tm + lax.broadcasted_iota(jnp.int32, (tm, 1), 0), seq_len - 1)
    has_prev = (pos != 0).astype(F32)
    has_next = (pos != seq_len - 1).astype(F32)

    def conv(lo):
        uc = u_ref[:, lo:lo + LANE]
        up = pltpu.roll(uc, 1, 0)[8:8 + tm] * has_prev
        un = pltpu.roll(uc, n - 1, 0)[8:8 + tm] * has_next
        return (cw_ref[0:1, lo:lo + LANE] * up + cw_ref[1:2, lo:lo + LANE] * uc[8:8 + tm]
                + cw_ref[2:3, lo:lo + LANE] * un + cw_ref[3:4, lo:lo + LANE])

    for kc in range(ft // LANE):
        a, g = conv(kc * LANE), conv(ft + kc * LANE)
        act_ref[:, kc * LANE:(kc + 1) * LANE] = (g * _sigmoid(g) * a).astype(act_ref.dtype)
    part = jnp.dot(act_ref[...], wd_ref[...], preferred_element_type=F32)

    @pl.when(j == 0)
    def _():
        acc_ref[...] = part

    @pl.when(j != 0)
    def _():
        acc_ref[...] += part

    @pl.when(j == pl.num_programs(1) - 1)
    def _():
        x_o[...] = x_ref[...] + mod_ref[5:6, :] * _rms(acc_ref[...], ng_ref[3:4, :])


def _conv_ffn(h2, x1, mod_l, mod_row, ng, w_up_t, conv_t, w_down_t, seq_len):
    t = x1.shape[0]
    tm = TOKEN_TILE
    nf = w_up_t.shape[0]
    ft = w_down_t.shape[1]
    nblk8 = t // 8
    assert seq_len & (seq_len - 1) == 0 and (seq_len % tm == 0 or tm % seq_len == 0)
    row = lambda i, j: (i, 0)
    return pl.pallas_call(
        functools.partial(_ffn_kernel, seq_len=seq_len),
        grid=(t // tm, nf),
        in_specs=[
            pl.BlockSpec((tm, D_MODEL), row),
            pl.BlockSpec((8, D_MODEL), lambda i, j: (jnp.maximum(i * (tm // 8) - 1, 0), 0)),
            pl.BlockSpec((8, D_MODEL), lambda i, j: (jnp.minimum((i + 1) * (tm // 8), nblk8 - 1), 0)),
            pl.BlockSpec((tm, D_MODEL), row),
            pl.BlockSpec((None, 6, D_MODEL), lambda i, j: (mod_row(i), 0, 0)),
            pl.BlockSpec((4, D_MODEL), lambda i, j: (0, 0)),
            pl.BlockSpec((None, D_MODEL, 2 * ft), lambda i, j: (j, 0, 0)),
            pl.BlockSpec((None, 8, 2 * ft), lambda i, j: (j, 0, 0)),
            pl.BlockSpec((None, ft, D_MODEL), lambda i, j: (j, 0, 0)),
        ],
        out_specs=pl.BlockSpec((tm, D_MODEL), row),
        out_shape=jax.ShapeDtypeStruct((t, D_MODEL), F32),
        scratch_shapes=[pltpu.VMEM((tm, D_MODEL), F32), pltpu.VMEM((tm + 16, 2 * ft), F32),
                        pltpu.VMEM((tm, ft), MXU_DTYPE)],
        compiler_params=_params(("parallel", "arbitrary")),
        name="conv_ffn",
    )(h2, h2, h2, x1, mod_l, ng, w_up_t, conv_t, w_down_t)


_Q_HEAD_ORDER = (0, 4, 1, 5, 2, 6, 3, 7)


def _rope_tables(seq_len):
    nf = HEAD_DIM // 4
    t = np.arange(seq_len)
    pos = np.stack([t // GRID_W, t % GRID_W]).astype(np.float32)
    inv = jnp.asarray(ROPE_THETA, F32) ** (-jnp.arange(nf, dtype=F32) / nf)
    ang = jnp.asarray(pos)[:, :, None] * inv
    d = np.arange(HEAD_DIM)
    ang = ang[d // (2 * nf), :, d % nf].T
    second = jnp.asarray(((d % (2 * nf)) // nf) == 1)[None, :]
    cos, sin = jnp.cos(ang), jnp.sin(ang)
    tabs = (cos, jnp.where(second, 0.0, -sin), jnp.where(second, sin, 0.0))
    return tuple(jnp.tile(x, (1, LANE // HEAD_DIM)).astype(F32) for x in tabs)


def _layer_weights(w_in, qk_g, w_br_a, w_up, conv_w, conv_b, w_down):
    q_cols = np.concatenate([np.arange(h * HEAD_DIM, (h + 1) * HEAD_DIM) for h in _Q_HEAD_ORDER])
    cols = np.concatenate([q_cols, np.arange(512, IN_WIDTH)])
    w_in_p = jnp.take(w_in, jnp.asarray(cols), axis=1).astype(MXU_DTYPE)
    w_a_p = jnp.take(w_br_a, jnp.asarray(q_cols), axis=0).astype(MXU_DTYPE)
    qg = jnp.tile(qk_g[0], N_HEADS).reshape(1, 512).astype(F32)
    kg = jnp.tile(qk_g[1], GA_KV_HEADS).reshape(1, LANE).astype(F32)
    nf = D_FF // FF_TILE
    wu = w_up.reshape(D_MODEL, 2, nf, FF_TILE)
    w_up_t = jnp.transpose(wu, (2, 0, 1, 3)).reshape(nf, D_MODEL, 2 * FF_TILE).astype(MXU_DTYPE)
    cw = jnp.concatenate([conv_w, conv_b[None]], axis=0).reshape(4, 2, nf, FF_TILE)
    conv_t = jnp.transpose(cw, (2, 0, 1, 3)).reshape(nf, 4, 2 * FF_TILE)
    conv_t = jnp.pad(conv_t, ((0, 0), (0, 4), (0, 0))).astype(F32)
    w_down_t = w_down.reshape(nf, FF_TILE, D_MODEL).astype(MXU_DTYPE)
    return w_in_p, w_a_p, qg, kg, w_up_t, conv_t, w_down_t


def kernel(x_prompt, x_sample, c, cache_ga_k, cache_ga_v, cache_na_k, cache_na_v, state_ssm_re, state_ssm_im,
           c_ctx, w_mod, b_mod, norm_g, w_in, qk_norm_g, na_rpb, ssm_lam_re, ssm_lam_im, ssm_log_step,
           ssm_b_re, ssm_b_im, ssm_c_re, ssm_c_im, ssm_d, w_glu, w_br_a, w_br_b, w_br_c, w_out,
           w_up, conv_w, conv_b, w_down):
    depth = w_in.shape[0]
    bp, lp, _ = x_prompt.shape
    bs, ls, _ = x_sample.shape
    lc = cache_ga_k.shape[2]
    assert lp % 256 == 0 and ls % TOKEN_TILE == 0 and (bp * lp) % TOKEN_TILE == 0
    assert bs % 8 == 0 and bp % 8 == 0, "the scan keeps one batch row per sublane"

    rows = 1 + bs
    rows_p = -(-rows // 8) * 8
    cvec = jnp.concatenate([c_ctx[None], c, jnp.zeros((rows_p - rows, D_MODEL), F32)], axis=0)
    mod = _modulation(cvec, w_mod, b_mod).reshape(depth, rows_p, 6, D_MODEL)

    w_t, w_b, w_c, l16 = _ssm_operators(ssm_lam_re, ssm_lam_im, ssm_log_step, ssm_b_re, ssm_b_im,
                                        ssm_c_re, ssm_c_im)
    seg = jnp.asarray(np.kron(np.eye(N_HEADS), np.full((HEAD_DIM, HEAD_DIM), 1.0 / HEAD_DIM)), MXU_DTYPE)
    rope_tabs = _rope_tables(ls)
    tiles_per_sample = ls // TOKEN_TILE
    ctx_row = lambda i: 0
    lat_row = lambda i: 1 + i // tiles_per_sample

    y_p = x_prompt.reshape(bp * lp, D_MODEL)
    y_s = x_sample.reshape(bs * ls, D_MODEL)
    zero_state = jnp.zeros((SSM_PAIRS, bp, 512), F32)
    outs = [[] for _ in range(6)]
    for l in range(depth):
        w_in_p, w_a_p, qg, kg, w_up_t, conv_t, w_down_t = _layer_weights(
            w_in[l], qk_norm_g[l], w_br_a[l], w_up[l], conv_w[l], conv_b[l], w_down[l])
        w_glu_l, w_b_l, w_c_l, w_o_l = (a[l].astype(MXU_DTYPE) for a in (w_glu, w_br_b, w_br_c, w_out))
        d_l = ssm_d[l].reshape(1, SSM_WIDTH).astype(F32)
        ng = norm_g[l].astype(F32)
        ssm_ops = (w_t[l], w_b[l], w_c[l], l16[l])

        q, k, v, zu, nq, nk, nv, gates = _in_projection(
            y_p, mod[l], ctx_row, ng, w_in_p, qg, kg, seg, None, lp, F32)
        r3 = lambda a: a.reshape(bp, lp, a.shape[-1])
        ya = _attention(r3(q), r3(k), r3(v), None, None, lp, "ga_ctx")
        yc = _attention(r3(nq), r3(nk), r3(nv), None, None, lp, "na_ctx")
        ys, fin = _ssm_scan(r3(zu), *ssm_ops, zero_state)
        x1, h2 = _merge(ya.reshape(-1, 512), ys.reshape(-1, 512), zu, yc.reshape(-1, 512), gates, y_p,
                        mod[l], ctx_row, ng, d_l, w_glu_l, w_a_p, w_b_l, w_c_l, w_o_l)
        y_p = _conv_ffn(h2, x1, mod[l], ctx_row, ng, w_up_t, conv_t, w_down_t, lp)
        f_re, f_im = _unpack_state(fin)
        for lst, a in zip(outs, (k.reshape(bp, lp, GA_KV_HEADS, HEAD_DIM), v.reshape(bp, lp, GA_KV_HEADS, HEAD_DIM),
                                 nk.reshape(bp, lp, N_HEADS, HEAD_DIM), nv.reshape(bp, lp, N_HEADS, HEAD_DIM),
                                 f_re, f_im)):
            lst.append(a)

        q, k, v, zu, nq, nk, nv, gates = _in_projection(
            y_s, mod[l], lat_row, ng, w_in_p, qg, kg, seg, rope_tabs, ls, MXU_DTYPE)
        r3 = lambda a: a.reshape(bs, ls, a.shape[-1])
        ck = cache_ga_k[:, l].reshape(bs, lc, LANE)
        cv = cache_ga_v[:, l].reshape(bs, lc, LANE)
        ya = _attention(r3(q), r3(k), r3(v), ck, cv, 2 * GRID_W, "ga_lat")
        nck = cache_na_k[:, l].reshape(bs, lc, 512)
        ncv = cache_na_v[:, l].reshape(bs, lc, 512)
        yc = _neighbourhood_attention(r3(nq), r3(nk), r3(nv), nck, ncv, na_rpb[l])
        h0 = _pack_state(state_ssm_re[:, l], state_ssm_im[:, l])
        ys, _ = _ssm_scan(r3(zu), *ssm_ops, h0)
        x1, h2 = _merge(ya.reshape(-1, 512), ys.reshape(-1, 512), zu, yc.reshape(-1, 512), gates, y_s,
                        mod[l], lat_row, ng, d_l, w_glu_l, w_a_p, w_b_l, w_c_l, w_o_l)
        y_s = _conv_ffn(h2, x1, mod[l], lat_row, ng, w_up_t, conv_t, w_down_t, ls)

    new = [jnp.stack(lst, axis=1) for lst in outs]
    return (y_p.reshape(bp, lp, D_MODEL), y_s.reshape(bs, ls, D_MODEL), *new)
```

```python
import functools
import math

import numpy as np
import jax
import jax.numpy as jnp
from jax import lax
from jax.experimental import pallas as pl
from jax.experimental.pallas import tpu as pltpu

F32 = jnp.float32
MXU_DTYPE = jnp.bfloat16

D_MODEL = 1024
HEAD_DIM = 64
N_HEADS = 8
GA_KV_HEADS = 2
GRID_W = 64
NA_WIN_ROWS = 8
NA_WIN_COLS = 16
NA_KEY_ROWS = 10
SSM_WIDTH = 512
SSM_GROUPS = 32
SSM_GROUP_CH = 16
SSM_STATE = 64
SSM_CHUNK = 16
SSM_PAIRS = SSM_GROUPS // 2
D_FF = 2816
FF_TILE = 1408
ROPE_THETA = 10000.0
EPS = 1e-6
IN_WIDTH = 5888
NEG_BIG = -1e30

LANE = 128
TOKEN_TILE = 512
VMEM_LIMIT = 56 * 1024 * 1024

_Q0, _K0, _V0, _U0, _NQ0, _NK0, _NV0, _G0 = 0, 512, 640, 768, 1280, 1792, 2304, 2816


def _sigmoid(x):
    return 1.0 / (1.0 + jnp.exp(-x))


def _gelu_tanh(x):
    return 0.5 * x * (1.0 + jnp.tanh(math.sqrt(2.0 / math.pi) * (x + 0.044715 * (x * x * x))))


def _rms(x, g):
    ms = jnp.mean(x * x, axis=-1, keepdims=True)
    return (x * lax.rsqrt(ms + EPS)) * g


def _mm(a, b):
    return jnp.dot(a.astype(MXU_DTYPE), b.astype(MXU_DTYPE), preferred_element_type=F32)


def _mm_nt(a, b):
    return lax.dot_general(a.astype(MXU_DTYPE), b.astype(MXU_DTYPE), (((1,), (1,)), ((), ())),
                           preferred_element_type=F32)


def _params(sem):
    return pltpu.CompilerParams(dimension_semantics=sem, vmem_limit_bytes=VMEM_LIMIT)


def _mod_kernel(c_ref, w_ref, b_ref, o_ref):
    c = c_ref[...]
    o_ref[...] = _mm(c * _sigmoid(c), w_ref[...]) + b_ref[...]


def _modulation(cvec, w_mod, b_mod):
    depth = w_mod.shape[0]
    rows = cvec.shape[0]
    tn = 1536
    return pl.pallas_call(
        _mod_kernel,
        grid=(depth, 6 * D_MODEL // tn),
        in_specs=[
            pl.BlockSpec((rows, D_MODEL), lambda l, j: (0, 0)),
            pl.BlockSpec((None, D_MODEL, tn), lambda l, j: (l, 0, j)),
            pl.BlockSpec((None, 1, tn), lambda l, j: (l, 0, j)),
        ],
        out_specs=pl.BlockSpec((None, rows, tn), lambda l, j: (l, 0, j)),
        out_shape=jax.ShapeDtypeStruct((depth, rows, 6 * D_MODEL), F32),
        compiler_params=_params(("parallel", "parallel")),
        name="adaln_mod",
    )(cvec, w_mod, b_mod.reshape(depth, 1, 6 * D_MODEL))


def _head_rms(z, seg, gain):
    ms = jnp.dot((z * z).astype(MXU_DTYPE), seg, preferred_element_type=F32)
    return (z * lax.rsqrt(ms + EPS)) * gain


def _rope_tile(t, c, s_up, s_dn):
    return t * c + pltpu.roll(t, LANE - 16, 1) * s_up + pltpu.roll(t, 16, 1) * s_dn


def _inproj_kernel(*refs, rope):
    if rope:
        (x_ref, mod_ref, ng_ref, w_ref, qg_ref, kg_ref, seg_ref, cos_ref, sup_ref, sdn_ref,
         q_o, k_o, v_o, u_o, nq_o, nk_o, nv_o, g_o) = refs
    else:
        (x_ref, mod_ref, ng_ref, w_ref, qg_ref, kg_ref, seg_ref,
         q_o, k_o, v_o, u_o, nq_o, nk_o, nv_o, g_o) = refs
    x = x_ref[...]
    h = _rms(x, ng_ref[0:1, :]) * (1.0 + mod_ref[1:2, :]) + mod_ref[0:1, :]
    hb = h.astype(MXU_DTYPE)
    scale = HEAD_DIM ** -0.5

    def proj(lo, width):
        return jnp.dot(hb, w_ref[:, lo:lo + width], preferred_element_type=F32)

    def maybe_rope(z):
        if not rope:
            return z
        c, su, sd = cos_ref[...], sup_ref[...], sdn_ref[...]
        tiles = [_rope_tile(z[:, i * LANE:(i + 1) * LANE], c, su, sd) for i in range(z.shape[1] // LANE)]
        return tiles[0] if len(tiles) == 1 else jnp.concatenate(tiles, axis=1)

    q = maybe_rope(_head_rms(proj(_Q0, 512), seg_ref[...], qg_ref[...]))
    q_o[...] = (q * scale).astype(q_o.dtype)
    k = maybe_rope(_head_rms(proj(_K0, 128), seg_ref[0:LANE, 0:LANE], kg_ref[...]))
    k_o[...] = k.astype(k_o.dtype)
    v_o[...] = proj(_V0, 128).astype(v_o.dtype)
    u_o[...] = proj(_U0, 512).astype(u_o.dtype)
    nq_o[...] = (proj(_NQ0, 512) * scale).astype(nq_o.dtype)
    nk_o[...] = proj(_NK0, 512).astype(nk_o.dtype)
    nv_o[...] = proj(_NV0, 512).astype(nv_o.dtype)
    for i in range(3):
        g_o[:, i * D_MODEL:(i + 1) * D_MODEL] = _sigmoid(proj(_G0 + i * D_MODEL, D_MODEL)).astype(g_o.dtype)


def _in_projection(x2d, mod_l, mod_row, ng, w_in, qg, kg, seg, rope_tabs, seq_len, kv_dtype):
    t = x2d.shape[0]
    tm = TOKEN_TILE
    tiles_per_seq = max(seq_len // tm, 1)
    rope = rope_tabs is not None
    row = lambda i: (i, 0)
    const = lambda i: (0, 0)
    in_specs = [
        pl.BlockSpec((tm, D_MODEL), row),
        pl.BlockSpec((None, 6, D_MODEL), lambda i: (mod_row(i), 0, 0)),
        pl.BlockSpec((4, D_MODEL), const),
        pl.BlockSpec((D_MODEL, IN_WIDTH), const),
        pl.BlockSpec((1, 512), const),
        pl.BlockSpec((1, LANE), const),
        pl.BlockSpec((512, 512), const),
    ]
    args = [x2d, mod_l, ng, w_in, qg, kg, seg]
    if rope:
        in_specs += [pl.BlockSpec((tm, LANE), lambda i: (i % tiles_per_seq, 0))] * 3
        args += list(rope_tabs)
    widths = (512, 128, 128, 512, 512, 512, 512, 3 * D_MODEL)
    dtypes = (MXU_DTYPE, kv_dtype, kv_dtype, F32, MXU_DTYPE, kv_dtype, kv_dtype, MXU_DTYPE)
    return pl.pallas_call(
        functools.partial(_inproj_kernel, rope=rope),
        grid=(t // tm,),
        in_specs=in_specs,
        out_specs=[pl.BlockSpec((tm, w), row) for w in widths],
        out_shape=[jax.ShapeDtypeStruct((t, w), dt) for w, dt in zip(widths, dtypes)],
        compiler_params=_params(("parallel",)),
        name="in_proj_rope" if rope else "in_proj",
    )(*args)


def _lane_masks(dtype):
    lane = lax.broadcasted_iota(jnp.int32, (1, LANE), 1)
    lo = lane < HEAD_DIM
    return lo, lo.astype(dtype), (~lo).astype(dtype)


def _attn_kernel(*refs, kv_tiles, cached):
    if cached:
        q_ref, k_ref, v_ref, kc_ref, vc_ref, o_ref = refs
    else:
        q_ref, k_ref, v_ref, o_ref = refs
    tq = q_ref.shape[0]
    lo, m_lo, m_hi = _lane_masks(MXU_DTYPE)
    n_pairs = N_HEADS // 2
    pairs_per_kv = n_pairs // kv_tiles
    for kt in range(kv_tiles):
        ksl = slice(kt * LANE, (kt + 1) * LANE)
        pairs = [kt * pairs_per_kv + j for j in range(pairs_per_kv)]
        rows = []
        for hp in pairs:
            q2 = q_ref[:, hp * LANE:(hp + 1) * LANE]
            rows += [q2 * m_lo, q2 * m_hi]
        qs = jnp.concatenate(rows, axis=0)
        s = _mm_nt(qs, k_ref[:, ksl])
        m = jnp.max(s, axis=-1, keepdims=True)
        if cached:
            s2 = _mm_nt(qs, kc_ref[:, ksl])
            m = jnp.maximum(m, jnp.max(s2, axis=-1, keepdims=True))
        p = jnp.exp(s - m)
        l = jnp.sum(p, axis=-1, keepdims=True)
        o = _mm(p, v_ref[:, ksl])
        if cached:
            p2 = jnp.exp(s2 - m)
            l = l + jnp.sum(p2, axis=-1, keepdims=True)
            o = o + _mm(p2, vc_ref[:, ksl])
        o = o * (1.0 / l)
        for j, hp in enumerate(pairs):
            o_e = o[(2 * j) * tq:(2 * j + 1) * tq]
            o_o = o[(2 * j + 1) * tq:(2 * j + 2) * tq]
            o_ref[:, hp * LANE:(hp + 1) * LANE] = jnp.where(lo, o_e, o_o).astype(o_ref.dtype)


def _attention(q, k, v, kc, vc, tq, name):
    b, lq, _ = q.shape
    lk, kw = k.shape[1], k.shape[2]
    cached = kc is not None
    qmap = lambda bi, ti: (bi, ti, 0)
    kmap = lambda bi, ti: (bi, 0, 0)
    in_specs = [
        pl.BlockSpec((None, tq, 512), qmap),
        pl.BlockSpec((None, lk, kw), kmap),
        pl.BlockSpec((None, lk, kw), kmap),
    ]
    args = [q, k, v]
    if cached:
        lc = kc.shape[1]
        in_specs += [pl.BlockSpec((None, lc, kw), kmap)] * 2
        args += [kc, vc]
    return pl.pallas_call(
        functools.partial(_attn_kernel, kv_tiles=kw // LANE, cached=cached),
        grid=(b, lq // tq),
        in_specs=in_specs,
        out_specs=pl.BlockSpec((None, tq, 512), qmap),
        out_shape=jax.ShapeDtypeStruct((b, lq, 512), MXU_DTYPE),
        compiler_params=_params(("parallel", "parallel")),
        name=name,
    )(*args)


def _na_geometry(seq_len):
    rows = seq_len // GRID_W
    n_tiles = rows // 2
    assert rows >= NA_KEY_ROWS and NA_WIN_ROWS <= rows and NA_KEY_ROWS % 2 == 0
    ws = np.clip(2 * np.arange(n_tiles) - NA_WIN_ROWS // 2, 0, rows - NA_KEY_ROWS)
    r = 2 * np.arange(n_tiles)[:, None, None] + np.arange(2)[None, :, None]
    key_r = ws[:, None, None] + np.arange(NA_KEY_ROWS)[None, None, :]
    r0 = np.clip(r - NA_WIN_ROWS // 2, 0, rows - NA_WIN_ROWS)
    valid = (key_r >= r0) & (key_r < r0 + NA_WIN_ROWS)
    dr = np.where(valid, key_r - r + NA_WIN_ROWS - 1, 2 * NA_WIN_ROWS - 1)
    assert (valid.sum(-1) == NA_WIN_ROWS).all()
    return ws.astype(np.int32), dr.reshape(-1).astype(np.int32)


def _na_bias_blocks(rpb):
    h = rpb.shape[0]
    nrel = 2 * NA_WIN_ROWS - 1
    zeros = jnp.zeros((h, nrel, LANE - (2 * NA_WIN_COLS - 1)), F32)
    v = jnp.concatenate([rpb[..., NA_WIN_COLS - 1:], zeros, rpb[..., :NA_WIN_COLS - 1]], axis=-1).astype(F32)
    t = jnp.tile(v, (1, 1, GRID_W))[..., :GRID_W * (LANE - 1)].reshape(h, nrel, GRID_W, LANE - 1)[..., :GRID_W]
    c = np.arange(GRID_W)
    c0 = np.clip(c - NA_WIN_COLS // 2, 0, GRID_W - NA_WIN_COLS)
    colmask = (c[None, :] >= c0[:, None]) & (c[None, :] < c0[:, None] + NA_WIN_COLS)
    t = jnp.where(jnp.asarray(colmask)[None, None], t, NEG_BIG)
    t = jnp.concatenate([t, jnp.full((h, 1, GRID_W, GRID_W), NEG_BIG, F32)], axis=1)
    pad = jnp.zeros_like(t)
    return jnp.concatenate([t, pad], axis=-1), jnp.concatenate([pad, t], axis=-1)


def _na_kernel(ws_ref, dr_ref, q_ref, k_ref, v_ref, kc_ref, vc_ref, bl_ref, br_ref, o_ref):
    i = pl.program_id(1)
    start = pl.multiple_of(ws_ref[i] * GRID_W, GRID_W)
    nk = NA_KEY_ROWS * GRID_W
    tq = q_ref.shape[0]
    lo, m_lo, m_hi = _lane_masks(MXU_DTYPE)

    def head_bias(h):
        rows = []
        for qr in range(2):
            base = (i * 2 + qr) * NA_KEY_ROWS
            tiles = [bl_ref[h, dr_ref[base + 2 * kp]] + br_ref[h, dr_ref[base + 2 * kp + 1]]
                     for kp in range(NA_KEY_ROWS // 2)]
            rows.append(jnp.concatenate(tiles, axis=1))
        return jnp.concatenate(rows, axis=0)

    for hp in range(N_HEADS // 2):
        sl = slice(hp * LANE, (hp + 1) * LANE)
        q2 = q_ref[:, sl]
        qs = jnp.concatenate([q2 * m_lo, q2 * m_hi], axis=0)
        kw = k_ref[pl.ds(start, nk), sl]
        vw = v_ref[pl.ds(start, nk), sl]
        bias = jnp.concatenate([head_bias(2 * hp), head_bias(2 * hp + 1)], axis=0)
        s = _mm_nt(qs, kw) + bias
        s2 = _mm_nt(qs, kc_ref[:, sl])
        m = jnp.maximum(jnp.max(s, axis=-1, keepdims=True), jnp.max(s2, axis=-1, keepdims=True))
        p = jnp.exp(s - m)
        p2 = jnp.exp(s2 - m)
        l = jnp.sum(p, axis=-1, keepdims=True) + jnp.sum(p2, axis=-1, keepdims=True)
        o = (_mm(p, vw) + _mm(p2, vc_ref[:, sl])) * (1.0 / l)
        o_ref[:, sl] = jnp.where(lo, o[:tq], o[tq:]).astype(o_ref.dtype)


def _neighbourhood_attention(q, k, v, kc, vc, rpb):
    b, seq_len, _ = q.shape
    lc = kc.shape[1]
    ws, dr = _na_geometry(seq_len)
    b_left, b_right = _na_bias_blocks(rpb)
    n_tiles = len(ws)
    tq = 2 * GRID_W
    qmap = lambda bi, ti, ws_r, dr_r: (bi, ti, 0)
    kmap = lambda bi, ti, ws_r, dr_r: (bi, 0, 0)
    bmap = lambda bi, ti, ws_r, dr_r: (0, 0, 0, 0)
    grid_spec = pltpu.PrefetchScalarGridSpec(
        num_scalar_prefetch=2,
        grid=(b, n_tiles),
        in_specs=[
            pl.BlockSpec((None, tq, 512), qmap),
            pl.BlockSpec((None, seq_len, 512), kmap),
            pl.BlockSpec((None, seq_len, 512), kmap),
            pl.BlockSpec((None, lc, 512), kmap),
            pl.BlockSpec((None, lc, 512), kmap),
            pl.BlockSpec(b_left.shape, bmap),
            pl.BlockSpec(b_right.shape, bmap),
        ],
        out_specs=pl.BlockSpec((None, tq, 512), qmap),
    )
    return pl.pallas_call(
        _na_kernel,
        grid_spec=grid_spec,
        out_shape=jax.ShapeDtypeStruct((b, seq_len, 512), MXU_DTYPE),
        compiler_params=_params(("parallel", "arbitrary")),
        name="na_attn",
    )(jnp.asarray(ws), jnp.asarray(dr), q, k, v, kc, vc, b_left, b_right)


def _ssm_prep_kernel(lrc_ref, lic_ref, ls_ref, lrr_ref, lir_ref, btr_ref, bti_ref,
                     bre_ref, bim_ref, cre_ref, cim_ref,
                     kt_o, pbr_o, pbi_o, cpr_o, cpi_o, l16r_o, l16i_o):
    nb = lrc_ref.shape[0]
    dt = jnp.exp(ls_ref[...])
    lr, li = lrc_ref[...], lic_ref[...]
    ar, ai = lr * dt, li * dt
    tau = (lax.broadcasted_iota(jnp.int32, (1, 1, SSM_CHUNK * SSM_GROUP_CH), 2) // SSM_GROUP_CH).astype(F32)
    mag = jnp.exp(tau * ar)
    ang = tau * ai
    pwr, pwi = mag * jnp.cos(ang), mag * jnp.sin(ang)
    e1 = jnp.exp(ar)
    lbr, lbi = e1 * jnp.cos(ai), e1 * jnp.sin(ai)

    def zoh_coef(lr_, li_, lbr_, lbi_):
        nr, ni = lbr_ - 1.0, lbi_
        den = 1.0 / (lr_ * lr_ + li_ * li_)
        return (nr * lr_ + ni * li_) * den, (ni * lr_ - nr * li_) * den

    cfr, cfi = zoh_coef(lr, li, lbr, lbi)
    bre, bim = bre_ref[...], bim_ref[...]
    bbr, bbi = cfr * bre - cfi * bim, cfr * bim + cfi * bre
    pbr_o[...] = pwr * bbr - pwi * bbi
    pbi_o[...] = pwr * bbi + pwi * bbr
    cre, cim = cre_ref[...], cim_ref[...]
    c0r, c0i = cre * pwr - cim * pwi, cre * pwi + cim * pwr
    cpr_o[...] = c0r * lbr - c0i * lbi
    cpi_o[...] = c0r * lbi + c0i * lbr
    m16 = jnp.exp(float(SSM_CHUNK) * ar)
    a16 = float(SSM_CHUNK) * ai
    l16r_o[...] = jnp.broadcast_to(m16 * jnp.cos(a16), l16r_o.shape)
    l16i_o[...] = jnp.broadcast_to(m16 * jnp.sin(a16), l16i_o.shape)
    lrr, lir = lrr_ref[...], lir_ref[...]
    arr, air = lrr * dt, lir * dt
    e1r = jnp.exp(arr)
    rfr, rfi = zoh_coef(lrr, lir, e1r * jnp.cos(air), e1r * jnp.sin(air))
    btr, bti = btr_ref[...], bti_ref[...]
    tbr, tbi = rfr * btr - rfi * bti, rfr * bti + rfi * btr
    for n in range(nb):
        kt_o[n] = (jnp.dot(tbr[n], c0r[n], preferred_element_type=F32, precision=lax.Precision.HIGHEST)
                   - jnp.dot(tbi[n], c0i[n], preferred_element_type=F32, precision=lax.Precision.HIGHEST))


def _ssm_operators(lam_re, lam_im, log_step, b_re, b_im, c_re, c_im):
    depth = lam_re.shape[0]
    n = depth * 2 * SSM_GROUPS
    p, hg, tc = SSM_STATE, SSM_GROUP_CH, SSM_CHUNK
    w = tc * hg
    nb = 8
    lam_re, lam_im = lam_re.reshape(n, p).astype(F32), lam_im.reshape(n, p).astype(F32)
    b_re, b_im = b_re.reshape(n, p, hg).astype(F32), b_im.reshape(n, p, hg).astype(F32)
    ct_re = jnp.swapaxes(c_re.reshape(n, hg, p), 1, 2).astype(F32)
    ct_im = jnp.swapaxes(c_im.reshape(n, hg, p), 1, 2).astype(F32)
    args = [
        lam_re.reshape(n, p, 1), lam_im.reshape(n, p, 1), log_step.reshape(n, 1, 1).astype(F32),
        lam_re.reshape(n, 1, p), lam_im.reshape(n, 1, p),
        jnp.swapaxes(b_re, 1, 2), jnp.swapaxes(b_im, 1, 2),
        jnp.tile(b_re, (1, 1, tc)), jnp.tile(b_im, (1, 1, tc)),
        jnp.tile(ct_re, (1, 1, tc)), jnp.tile(ct_im, (1, 1, tc)),
    ]
    blk = lambda a: pl.BlockSpec((nb,) + a.shape[1:], lambda i: (i, 0, 0))
    out_shapes = [(n, hg, w), (n, p, w), (n, p, w), (n, p, w), (n, p, w), (n, p, LANE), (n, p, LANE)]
    kt, pbr, pbi, cpr, cpi, l16r, l16i = pl.pallas_call(
        _ssm_prep_kernel,
        grid=(n // nb,),
        in_specs=[blk(a) for a in args],
        out_specs=[pl.BlockSpec((nb,) + s[1:], lambda i: (i, 0, 0)) for s in out_shapes],
        out_shape=[jax.ShapeDtypeStruct(s, F32) for s in out_shapes],
        compiler_params=_params(("parallel",)),
        name="ssm_prep",
    )(*args)

    g, q = SSM_GROUPS, SSM_PAIRS
    eye2 = jnp.eye(2, dtype=F32)
    s_idx, j_idx = np.arange(tc)[:, None], np.arange(tc)[None, :]
    kt = kt.reshape(depth, 2, g, hg, tc, hg)

    def toeplitz(k, lag):
        t = jnp.take(k, jnp.asarray(np.clip(lag, 0, tc - 1).reshape(-1)), axis=3)
        t = t.reshape(depth, g, hg, tc, tc, hg) * jnp.asarray((lag >= 0).astype(np.float32))[None, None, None, :, :, None]
        return t

    t = toeplitz(kt[:, 0], j_idx - s_idx) + toeplitz(kt[:, 1], s_idx - j_idx)
    t = jnp.transpose(t, (0, 1, 3, 2, 4, 5)).reshape(depth, q, 2, tc, hg, tc, hg)
    w_t = jnp.einsum('lqashjo,ab->lqsahjbo', t, eye2).reshape(depth, q, 2 * w, 2 * w)

    def inject(pb, flip):
        pb = pb.reshape(depth, g, p, tc, hg)
        if flip:
            pb = pb[:, :, :, ::-1, :]
        pb = jnp.transpose(pb, (0, 1, 3, 4, 2)).reshape(depth, q, 2, tc, hg, p)
        return jnp.einsum('lqashp,ab->lqsahbp', pb, eye2).reshape(depth, q, 2 * w, 2 * p)

    pbr, pbi = pbr.reshape(depth, 2, g, p, w), pbi.reshape(depth, 2, g, p, w)
    w_b = jnp.concatenate([inject(pbr[:, 0], True), inject(pbi[:, 0], True),
                           inject(pbr[:, 1], False), inject(pbi[:, 1], False)], axis=-1)

    def readout(cp, flip, sign):
        cp = cp.reshape(depth, q, 2, p, tc, hg)
        if flip:
            cp = cp[:, :, :, :, ::-1, :]
        return sign * jnp.einsum('lqapjo,ab->lqapjbo', cp, eye2).reshape(depth, q, 2 * p, 2 * w)

    cpr, cpi = cpr.reshape(depth, 2, g, p, w), cpi.reshape(depth, 2, g, p, w)
    w_c = jnp.concatenate([readout(cpr[:, 0], False, 1.0), readout(cpi[:, 0], False, -1.0),
                           readout(cpr[:, 1], True, 1.0), readout(cpi[:, 1], True, -1.0)], axis=-2)

    l16 = jnp.stack([l16r[:, :, 0], l16i[:, :, 0]], axis=0).reshape(2, depth, 2, q, 2 * p)
    l16 = jnp.transpose(l16, (1, 3, 2, 0, 4)).reshape(depth, q, 4, 2 * p)
    l16 = jnp.pad(l16, ((0, 0), (0, 0), (0, 4), (0, 0)))
    return w_t.astype(MXU_DTYPE), w_b.astype(MXU_DTYPE), w_c.astype(MXU_DTYPE), l16


def _ssm_kernel(u_ref, wt_ref, wb_ref, wc_ref, l16_ref, h0_ref, y_o, fin_o, dx_ref, xs_ref, *, batch):
    nrows = u_ref.shape[0]
    n_chunks = nrows // batch
    u = u_ref[...]
    y_intra = jnp.dot(u, wt_ref[...], preferred_element_type=F32)
    dx_ref[...] = jnp.dot(u, wb_ref[...], preferred_element_type=F32)
    lfr, lfi, lbr, lbi = (l16_ref[r:r + 1, :] for r in range(4))
    h0 = h0_ref[...]
    sp = LANE

    def body(c, carry):
        fr, fi, br, bi = carry
        rf = pl.multiple_of(c * batch, batch)
        rb = pl.multiple_of((n_chunks - 1 - c) * batch, batch)
        xs_ref[pl.ds(rf, batch), 0:sp] = fr
        xs_ref[pl.ds(rf, batch), sp:2 * sp] = fi
        xs_ref[pl.ds(rb, batch), 2 * sp:3 * sp] = br
        xs_ref[pl.ds(rb, batch), 3 * sp:4 * sp] = bi
        nfr = lfr * fr - lfi * fi + dx_ref[pl.ds(rf, batch), 0:sp]
        nfi = lfr * fi + lfi * fr + dx_ref[pl.ds(rf, batch), sp:2 * sp]
        nbr = lbr * br - lbi * bi + dx_ref[pl.ds(rb, batch), 2 * sp:3 * sp]
        nbi = lbr * bi + lbi * br + dx_ref[pl.ds(rb, batch), 3 * sp:4 * sp]
        return nfr, nfi, nbr, nbi

    fin = lax.fori_loop(0, n_chunks, body, tuple(h0[:, r * sp:(r + 1) * sp] for r in range(4)))
    for r in range(4):
        fin_o[:, r * sp:(r + 1) * sp] = fin[r]
    y_o[...] = (y_intra + jnp.dot(xs_ref[...].astype(MXU_DTYPE), wc_ref[...],
                                  preferred_element_type=F32)).astype(y_o.dtype)


def _ssm_scan(zu, w_t, w_b, w_c, l16, h0):
    b, seq_len, _ = zu.shape
    q, tc = SSM_PAIRS, SSM_CHUNK
    nc = seq_len // tc
    pw = SSM_WIDTH // q
    u = zu.astype(MXU_DTYPE).reshape(b, nc, tc, q, pw)
    u = jnp.transpose(u, (3, 1, 0, 2, 4)).reshape(q, nc * b, tc * pw)
    wspec = pl.BlockSpec((None, 512, 512), lambda i: (i, 0, 0))
    y, fin = pl.pallas_call(
        functools.partial(_ssm_kernel, batch=b),
        grid=(q,),
        in_specs=[
            pl.BlockSpec((None, nc * b, 512), lambda i: (i, 0, 0)),
            wspec, wspec, wspec,
            pl.BlockSpec((None, 8, LANE), lambda i: (i, 0, 0)),
            pl.BlockSpec((None, b, 512), lambda i: (i, 0, 0)),
        ],
        out_specs=[
            pl.BlockSpec((None, nc * b, 512), lambda i: (i, 0, 0)),
            pl.BlockSpec((None, b, 512), lambda i: (i, 0, 0)),
        ],
        out_shape=[jax.ShapeDtypeStruct((q, nc * b, 512), MXU_DTYPE),
                   jax.ShapeDtypeStruct((q, b, 512), F32)],
        scratch_shapes=[pltpu.VMEM((nc * b, 512), F32), pltpu.VMEM((nc * b, 512), F32)],
        compiler_params=_params(("parallel",)),
        name="ssm_scan",
    )(u, w_t, w_b, w_c, l16, h0)
    y = jnp.transpose(y.reshape(q, nc, b, tc, pw), (2, 1, 3, 0, 4)).reshape(b, seq_len, SSM_WIDTH)
    return y, fin


def _pack_state(s_re, s_im):
    b = s_re.shape[0]
    a = jnp.stack([s_re, s_im], axis=2).reshape(b, 2, 2, SSM_PAIRS, 2 * SSM_STATE)
    return jnp.transpose(a, (3, 0, 1, 2, 4)).reshape(SSM_PAIRS, b, 4 * 2 * SSM_STATE).astype(F32)


def _unpack_state(fin):
    b = fin.shape[1]
    a = jnp.transpose(fin.reshape(SSM_PAIRS, b, 2, 2, 2 * SSM_STATE), (1, 2, 3, 0, 4))
    a = a.reshape(b, 2, 2, SSM_GROUPS, SSM_STATE)
    return a[:, :, 0], a[:, :, 1]


def _merge_kernel(ya_ref, ys_ref, u_ref, yc_ref, g_ref, x_ref, mod_ref, ng_ref, d_ref,
                  wglu_ref, wa_ref, wb_ref, wc_ref, wo_ref, x_o, h_o):
    y = ys_ref[...].astype(F32) + d_ref[...] * u_ref[...]
    gl = _gelu_tanh(y)
    yb = gl * _sigmoid(_mm(gl, wglu_ref[...]))
    merged = (g_ref[:, 0:D_MODEL].astype(F32) * _mm(ya_ref[...], wa_ref[...])
              + g_ref[:, D_MODEL:2 * D_MODEL].astype(F32) * _mm(yb, wb_ref[...])
              + g_ref[:, 2 * D_MODEL:3 * D_MODEL].astype(F32) * _mm(yc_ref[...], wc_ref[...]))
    x1 = x_ref[...] + mod_ref[2:3, :] * _rms(_mm(merged, wo_ref[...]), ng_ref[1:2, :])
    x_o[...] = x1
    h_o[...] = (_rms(x1, ng_ref[2:3, :]) * (1.0 + mod_ref[4:5, :]) + mod_ref[3:4, :]).astype(h_o.dtype)


def _merge(ya, ys, zu, yc, gates, x2d, mod_l, mod_row, ng, ssm_d, w_glu, w_a, w_b, w_c, w_o):
    t = x2d.shape[0]
    tm = TOKEN_TILE
    row = lambda i: (i, 0)
    const = lambda i: (0, 0)
    r512 = pl.BlockSpec((tm, 512), row)
    wbr = pl.BlockSpec((512, D_MODEL), const)
    return pl.pallas_call(
        _merge_kernel,
        grid=(t // tm,),
        in_specs=[
            r512, r512, r512, r512,
            pl.BlockSpec((tm, 3 * D_MODEL), row),
            pl.BlockSpec((tm, D_MODEL), row),
            pl.BlockSpec((None, 6, D_MODEL), lambda i: (mod_row(i), 0, 0)),
            pl.BlockSpec((4, D_MODEL), const),
            pl.BlockSpec((1, 512), const),
            pl.BlockSpec((512, 512), const),
            wbr, wbr, wbr,
            pl.BlockSpec((D_MODEL, D_MODEL), const),
        ],
        out_specs=[pl.BlockSpec((tm, D_MODEL), row), pl.BlockSpec((tm, D_MODEL), row)],
        out_shape=[jax.ShapeDtypeStruct((t, D_MODEL), F32), jax.ShapeDtypeStruct((t, D_MODEL), MXU_DTYPE)],
        compiler_params=_params(("parallel",)),
        name="merge",
    )(ya, ys, zu, yc, gates, x2d, mod_l, ng, ssm_d, w_glu, w_a, w_b, w_c, w_o)


def _ffn_kernel(h_ref, hp_ref, hn_ref, x_ref, mod_ref, ng_ref, wa_ref, wg_ref, cwa_ref, cwg_ref,
                cba_ref, cbg_ref, wd_ref, x_o, acc_ref, ua_ref, ug_ref, act_ref, *, seq_len):
    i, j = pl.program_id(0), pl.program_id(1)
    tm = h_ref.shape[0]
    ft = wd_ref.shape[0]
    n = tm + 16
    hh = jnp.concatenate([hp_ref[...], h_ref[...], hn_ref[...]], axis=0)
    ua_ref[...] = jnp.dot(hh, wa_ref[...], preferred_element_type=F32)
    ug_ref[...] = jnp.dot(hh, wg_ref[...], preferred_element_type=F32)
    pos = jnp.bitwise_and(i * tm + lax.broadcasted_iota(jnp.int32, (tm, 1), 0), seq_len - 1)
    has_prev = (pos != 0).astype(F32)
    has_next = (pos != seq_len - 1).astype(F32)

    def conv(u_ref, cw_ref, cb_ref, lo):
        uc = u_ref[:, lo:lo + LANE]
        up = pltpu.roll(uc, 1, 0)[8:8 + tm] * has_prev
        un = pltpu.roll(uc, n - 1, 0)[8:8 + tm] * has_next
        return (cw_ref[0:1, lo:lo + LANE] * up + cw_ref[1:2, lo:lo + LANE] * uc[8:8 + tm]
                + cw_ref[2:3, lo:lo + LANE] * un + cb_ref[0:1, lo:lo + LANE])

    for kc in range(ft // LANE):
        a = conv(ua_ref, cwa_ref, cba_ref, kc * LANE)
        g = conv(ug_ref, cwg_ref, cbg_ref, kc * LANE)
        act_ref[:, kc * LANE:(kc + 1) * LANE] = (g * _sigmoid(g) * a).astype(act_ref.dtype)
    part = jnp.dot(act_ref[...], wd_ref[...], preferred_element_type=F32)

    @pl.when(j == 0)
    def _():
        acc_ref[...] = part

    @pl.when(j != 0)
    def _():
        acc_ref[...] += part

    @pl.when(j == pl.num_programs(1) - 1)
    def _():
        x_o[...] = x_ref[...] + mod_ref[5:6, :] * _rms(acc_ref[...], ng_ref[3:4, :])


def _conv_ffn(h2, x1, mod_l, mod_row, ng, w_up, conv_w, conv_b, w_down, seq_len):
    t = x1.shape[0]
    tm = TOKEN_TILE
    ft = FF_TILE
    nf = D_FF // ft
    nblk8 = t // 8
    assert seq_len & (seq_len - 1) == 0 and (seq_len % tm == 0 or tm % seq_len == 0)
    row = lambda i, j: (i, 0)
    return pl.pallas_call(
        functools.partial(_ffn_kernel, seq_len=seq_len),
        grid=(t // tm, nf),
        in_specs=[
            pl.BlockSpec((tm, D_MODEL), row),
            pl.BlockSpec((8, D_MODEL), lambda i, j: (jnp.maximum(i * (tm // 8) - 1, 0), 0)),
            pl.BlockSpec((8, D_MODEL), lambda i, j: (jnp.minimum((i + 1) * (tm // 8), nblk8 - 1), 0)),
            pl.BlockSpec((tm, D_MODEL), row),
            pl.BlockSpec((None, 6, D_MODEL), lambda i, j: (mod_row(i), 0, 0)),
            pl.BlockSpec((4, D_MODEL), lambda i, j: (0, 0)),
            pl.BlockSpec((D_MODEL, ft), lambda i, j: (0, j)),
            pl.BlockSpec((D_MODEL, ft), lambda i, j: (0, nf + j)),
            pl.BlockSpec((3, ft), lambda i, j: (0, j)),
            pl.BlockSpec((3, ft), lambda i, j: (0, nf + j)),
            pl.BlockSpec((1, ft), lambda i, j: (0, j)),
            pl.BlockSpec((1, ft), lambda i, j: (0, nf + j)),
            pl.BlockSpec((ft, D_MODEL), lambda i, j: (j, 0)),
        ],
        out_specs=pl.BlockSpec((tm, D_MODEL), row),
        out_shape=jax.ShapeDtypeStruct((t, D_MODEL), F32),
        scratch_shapes=[pltpu.VMEM((tm, D_MODEL), F32), pltpu.VMEM((tm + 16, ft), F32),
                        pltpu.VMEM((tm + 16, ft), F32), pltpu.VMEM((tm, ft), MXU_DTYPE)],
        compiler_params=_params(("parallel", "arbitrary")),
        name="conv_ffn",
    )(h2, h2, h2, x1, mod_l, ng, w_up, w_up, conv_w, conv_w, conv_b, conv_b, w_down)


_Q_HEAD_ORDER = (0, 4, 1, 5, 2, 6, 3, 7)


def _rope_tables(seq_len):
    nf = HEAD_DIM // 4
    t = np.arange(seq_len)
    pos = np.stack([t // GRID_W, t % GRID_W]).astype(np.float32)
    inv = jnp.asarray(ROPE_THETA, F32) ** (-jnp.arange(nf, dtype=F32) / nf)
    ang = jnp.asarray(pos)[:, :, None] * inv
    d = np.arange(HEAD_DIM)
    ang = ang[d // (2 * nf), :, d % nf].T
    second = jnp.asarray(((d % (2 * nf)) // nf) == 1)[None, :]
    cos, sin = jnp.cos(ang), jnp.sin(ang)
    tabs = (cos, jnp.where(second, 0.0, -sin), jnp.where(second, sin, 0.0))
    return tuple(jnp.tile(x, (1, LANE // HEAD_DIM)).astype(F32) for x in tabs)


def _layer_weights(w_in, qk_g, w_br_a):
    hd = HEAD_DIM
    w_in_p = jnp.concatenate([w_in[:, h * hd:(h + 1) * hd] for h in _Q_HEAD_ORDER] + [w_in[:, 512:]],
                             axis=1).astype(MXU_DTYPE)
    w_a_p = jnp.concatenate([w_br_a[h * hd:(h + 1) * hd] for h in _Q_HEAD_ORDER], axis=0).astype(MXU_DTYPE)
    qg = jnp.tile(qk_g[0], N_HEADS).reshape(1, 512).astype(F32)
    kg = jnp.tile(qk_g[1], GA_KV_HEADS).reshape(1, LANE).astype(F32)
    return w_in_p, w_a_p, qg, kg


def kernel(x_prompt, x_sample, c, cache_ga_k, cache_ga_v, cache_na_k, cache_na_v, state_ssm_re, state_ssm_im,
           c_ctx, w_mod, b_mod, norm_g, w_in, qk_norm_g, na_rpb, ssm_lam_re, ssm_lam_im, ssm_log_step,
           ssm_b_re, ssm_b_im, ssm_c_re, ssm_c_im, ssm_d, w_glu, w_br_a, w_br_b, w_br_c, w_out,
           w_up, conv_w, conv_b, w_down):
    depth = w_in.shape[0]
    bp, lp, _ = x_prompt.shape
    bs, ls, _ = x_sample.shape
    lc = cache_ga_k.shape[2]
    assert lp % 256 == 0 and ls % TOKEN_TILE == 0 and (bp * lp) % TOKEN_TILE == 0
    assert bs % 8 == 0 and bp % 8 == 0, "the scan keeps one batch row per sublane"

    rows = 1 + bs
    rows_p = -(-rows // 8) * 8
    cvec = jnp.concatenate([c_ctx[None], c, jnp.zeros((rows_p - rows, D_MODEL), F32)], axis=0)
    mod = _modulation(cvec, w_mod, b_mod).reshape(depth, rows_p, 6, D_MODEL)

    w_t, w_b, w_c, l16 = _ssm_operators(ssm_lam_re, ssm_lam_im, ssm_log_step, ssm_b_re, ssm_b_im,
                                        ssm_c_re, ssm_c_im)
    seg = jnp.asarray(np.kron(np.eye(N_HEADS), np.full((HEAD_DIM, HEAD_DIM), 1.0 / HEAD_DIM)), MXU_DTYPE)
    rope_tabs = _rope_tables(ls)
    tiles_per_sample = ls // TOKEN_TILE
    ctx_row = lambda i: 0
    lat_row = lambda i: 1 + i // tiles_per_sample

    y_p = x_prompt.reshape(bp * lp, D_MODEL)
    y_s = x_sample.reshape(bs * ls, D_MODEL)
    zero_state = jnp.zeros((SSM_PAIRS, bp, 512), F32)
    outs = [[] for _ in range(6)]
    for l in range(depth):
        w_in_p, w_a_p, qg, kg = _layer_weights(w_in[l], qk_norm_g[l], w_br_a[l])
        w_glu_l, w_b_l, w_c_l, w_o_l = (a[l].astype(MXU_DTYPE) for a in (w_glu, w_br_b, w_br_c, w_out))
        ffn_w = (w_up[l].astype(MXU_DTYPE), conv_w[l].astype(F32), conv_b[l].reshape(1, 2 * D_FF).astype(F32),
                 w_down[l].astype(MXU_DTYPE))
        d_l = ssm_d[l].reshape(1, SSM_WIDTH).astype(F32)
        ng = norm_g[l].astype(F32)
        ssm_ops = (w_t[l], w_b[l], w_c[l], l16[l])

        q, k, v, zu, nq, nk, nv, gates = _in_projection(
            y_p, mod[l], ctx_row, ng, w_in_p, qg, kg, seg, None, lp, F32)
        r3 = lambda a: a.reshape(bp, lp, a.shape[-1])
        ya = _attention(r3(q), r3(k), r3(v), None, None, lp, "ga_ctx")
        yc = _attention(r3(nq), r3(nk), r3(nv), None, None, lp, "na_ctx")
        ys, fin = _ssm_scan(r3(zu), *ssm_ops, zero_state)
        x1, h2 = _merge(ya.reshape(-1, 512), ys.reshape(-1, 512), zu, yc.reshape(-1, 512), gates, y_p,
                        mod[l], ctx_row, ng, d_l, w_glu_l, w_a_p, w_b_l, w_c_l, w_o_l)
        y_p = _conv_ffn(h2, x1, mod[l], ctx_row, ng, *ffn_w, lp)
        f_re, f_im = _unpack_state(fin)
        for lst, a in zip(outs, (k.reshape(bp, lp, GA_KV_HEADS, HEAD_DIM), v.reshape(bp, lp, GA_KV_HEADS, HEAD_DIM),
                                 nk.reshape(bp, lp, N_HEADS, HEAD_DIM), nv.reshape(bp, lp, N_HEADS, HEAD_DIM),
                                 f_re, f_im)):
            lst.append(a)

        q, k, v, zu, nq, nk, nv, gates = _in_projection(
            y_s, mod[l], lat_row, ng, w_in_p, qg, kg, seg, rope_tabs, ls, MXU_DTYPE)
        r3 = lambda a: a.reshape(bs, ls, a.shape[-1])
        ck = cache_ga_k[:, l].reshape(bs, lc, LANE)
        cv = cache_ga_v[:, l].reshape(bs, lc, LANE)
        ya = _attention(r3(q), r3(k), r3(v), ck, cv, 2 * GRID_W, "ga_lat")
        nck = cache_na_k[:, l].reshape(bs, lc, 512)
        ncv = cache_na_v[:, l].reshape(bs, lc, 512)
        yc = _neighbourhood_attention(r3(nq), r3(nk), r3(nv), nck, ncv, na_rpb[l])
        h0 = _pack_state(state_ssm_re[:, l], state_ssm_im[:, l])
        ys, _ = _ssm_scan(r3(zu), *ssm_ops, h0)
        x1, h2 = _merge(ya.reshape(-1, 512), ys.reshape(-1, 512), zu, yc.reshape(-1, 512), gates, y_s,
                        mod[l], lat_row, ng, d_l, w_glu_l, w_a_p, w_b_l, w_c_l, w_o_l)
        y_s = _conv_ffn(h2, x1, mod[l], lat_row, ng, *ffn_w, ls)

    new = [jnp.stack(lst, axis=1) for lst in outs]
    return (y_p.reshape(bp, lp, D_MODEL), y_s.reshape(bs, ls, D_MODEL), *new)
```

```python
import functools
import math

import numpy as np
import jax
import jax.numpy as jnp
from jax import lax
from jax.experimental import pallas as pl
from jax.experimental.pallas import tpu as pltpu

F32 = jnp.float32
MXU_DTYPE = jnp.bfloat16

D_MODEL = 1024
HEAD_DIM = 64
N_HEADS = 8
GA_KV_HEADS = 2
GRID_W = 64
NA_WIN_ROWS = 8
NA_WIN_COLS = 16
NA_KEY_ROWS = 10
SSM_WIDTH = 512
SSM_GROUPS = 32
SSM_GROUP_CH = 16
SSM_STATE = 64
SSM_CHUNK = 16
SSM_PAIRS = SSM_GROUPS // 2
D_FF = 2816
FF_TILE = 1408
ROPE_THETA = 10000.0
EPS = 1e-6
IN_WIDTH = 5888
NEG_BIG = -1e30

LANE = 128
TOKEN_TILE = 512
VMEM_LIMIT = 56 * 1024 * 1024

_Q0, _K0, _V0, _U0, _NQ0, _NK0, _NV0, _G0 = 0, 512, 640, 768, 1280, 1792, 2304, 2816


def _sigmoid(x):
    return 1.0 / (1.0 + jnp.exp(-x))


def _gelu_tanh(x):
    return 0.5 * x * (1.0 + jnp.tanh(math.sqrt(2.0 / math.pi) * (x + 0.044715 * (x * x * x))))


def _rms(x, g):
    ms = jnp.mean(x * x, axis=-1, keepdims=True)
    return (x * lax.rsqrt(ms + EPS)) * g


def _mm(a, b):
    return jnp.dot(a.astype(MXU_DTYPE), b.astype(MXU_DTYPE), preferred_element_type=F32)


def _mm_nt(a, b):
    return lax.dot_general(a.astype(MXU_DTYPE), b.astype(MXU_DTYPE), (((1,), (1,)), ((), ())),
                           preferred_element_type=F32)


def _params(sem):
    return pltpu.CompilerParams(dimension_semantics=sem, vmem_limit_bytes=VMEM_LIMIT)


def _mod_kernel(c_ref, w_ref, b_ref, o_ref):
    c = c_ref[...]
    o_ref[...] = _mm(c * _sigmoid(c), w_ref[...]) + b_ref[...]


def _modulation(cvec, w_mod, b_mod):
    depth = w_mod.shape[0]
    rows = cvec.shape[0]
    tn = 1536
    return pl.pallas_call(
        _mod_kernel,
        grid=(depth, 6 * D_MODEL // tn),
        in_specs=[
            pl.BlockSpec((rows, D_MODEL), lambda l, j: (0, 0)),
            pl.BlockSpec((None, D_MODEL, tn), lambda l, j: (l, 0, j)),
            pl.BlockSpec((None, 1, tn), lambda l, j: (l, 0, j)),
        ],
        out_specs=pl.BlockSpec((None, rows, tn), lambda l, j: (l, 0, j)),
        out_shape=jax.ShapeDtypeStruct((depth, rows, 6 * D_MODEL), F32),
        compiler_params=_params(("parallel", "parallel")),
        name="adaln_mod",
    )(cvec, w_mod, b_mod.reshape(depth, 1, 6 * D_MODEL))


def _head_rms(z, seg, gain):
    ms = jnp.dot((z * z).astype(MXU_DTYPE), seg, preferred_element_type=F32)
    return (z * lax.rsqrt(ms + EPS)) * gain


def _rope_tile(t, c, s_up, s_dn):
    return t * c + pltpu.roll(t, LANE - 16, 1) * s_up + pltpu.roll(t, 16, 1) * s_dn


def _inproj_kernel(*refs, rope):
    if rope:
        (x_ref, mod_ref, ng_ref, w_ref, qg_ref, kg_ref, seg_ref, cos_ref, sup_ref, sdn_ref,
         q_o, k_o, v_o, u_o, nq_o, nk_o, nv_o, g_o) = refs
    else:
        (x_ref, mod_ref, ng_ref, w_ref, qg_ref, kg_ref, seg_ref,
         q_o, k_o, v_o, u_o, nq_o, nk_o, nv_o, g_o) = refs
    x = x_ref[...]
    h = _rms(x, ng_ref[0:1, :]) * (1.0 + mod_ref[1:2, :]) + mod_ref[0:1, :]
    hb = h.astype(MXU_DTYPE)
    scale = HEAD_DIM ** -0.5

    def proj(lo, width):
        return jnp.dot(hb, w_ref[:, lo:lo + width], preferred_element_type=F32)

    def maybe_rope(z):
        if not rope:
            return z
        c, su, sd = cos_ref[...], sup_ref[...], sdn_ref[...]
        tiles = [_rope_tile(z[:, i * LANE:(i + 1) * LANE], c, su, sd) for i in range(z.shape[1] // LANE)]
        return tiles[0] if len(tiles) == 1 else jnp.concatenate(tiles, axis=1)

    q = maybe_rope(_head_rms(proj(_Q0, 512), seg_ref[...], qg_ref[...]))
    q_o[...] = (q * scale).astype(q_o.dtype)
    k = maybe_rope(_head_rms(proj(_K0, 128), seg_ref[0:LANE, 0:LANE], kg_ref[...]))
    k_o[...] = k.astype(k_o.dtype)
    v_o[...] = proj(_V0, 128).astype(v_o.dtype)
    u_o[...] = proj(_U0, 512).astype(u_o.dtype)
    nq_o[...] = (proj(_NQ0, 512) * scale).astype(nq_o.dtype)
    nk_o[...] = proj(_NK0, 512).astype(nk_o.dtype)
    nv_o[...] = proj(_NV0, 512).astype(nv_o.dtype)
    for i in range(3):
        g_o[:, i * D_MODEL:(i + 1) * D_MODEL] = _sigmoid(proj(_G0 + i * D_MODEL, D_MODEL)).astype(g_o.dtype)


def _in_projection(x2d, mod_l, mod_row, ng, w_in, qg, kg, seg, rope_tabs, seq_len, kv_dtype):
    t = x2d.shape[0]
    tm = TOKEN_TILE
    tiles_per_seq = max(seq_len // tm, 1)
    rope = rope_tabs is not None
    row = lambda i: (i, 0)
    const = lambda i: (0, 0)
    in_specs = [
        pl.BlockSpec((tm, D_MODEL), row),
        pl.BlockSpec((None, 6, D_MODEL), lambda i: (mod_row(i), 0, 0)),
        pl.BlockSpec((4, D_MODEL), const),
        pl.BlockSpec((D_MODEL, IN_WIDTH), const),
        pl.BlockSpec((1, 512), const),
        pl.BlockSpec((1, LANE), const),
        pl.BlockSpec((512, 512), const),
    ]
    args = [x2d, mod_l, ng, w_in, qg, kg, seg]
    if rope:
        in_specs += [pl.BlockSpec((tm, LANE), lambda i: (i % tiles_per_seq, 0))] * 3
        args += list(rope_tabs)
    widths = (512, 128, 128, 512, 512, 512, 512, 3 * D_MODEL)
    dtypes = (MXU_DTYPE, kv_dtype, kv_dtype, F32, MXU_DTYPE, kv_dtype, kv_dtype, MXU_DTYPE)
    return pl.pallas_call(
        functools.partial(_inproj_kernel, rope=rope),
        grid=(t // tm,),
        in_specs=in_specs,
        out_specs=[pl.BlockSpec((tm, w), row) for w in widths],
        out_shape=[jax.ShapeDtypeStruct((t, w), dt) for w, dt in zip(widths, dtypes)],
        compiler_params=_params(("parallel",)),
        name="in_proj_rope" if rope else "in_proj",
    )(*args)


def _lane_masks(dtype):
    lane = lax.broadcasted_iota(jnp.int32, (1, LANE), 1)
    lo = lane < HEAD_DIM
    return lo, lo.astype(dtype), (~lo).astype(dtype)


def _attn_kernel(*refs, kv_tiles, cached):
    if cached:
        q_ref, k_ref, v_ref, kc_ref, vc_ref, o_ref = refs
    else:
        q_ref, k_ref, v_ref, o_ref = refs
    tq = q_ref.shape[0]
    lo, m_lo, m_hi = _lane_masks(MXU_DTYPE)
    n_pairs = N_HEADS // 2
    pairs_per_kv = n_pairs // kv_tiles
    for kt in range(kv_tiles):
        ksl = slice(kt * LANE, (kt + 1) * LANE)
        pairs = [kt * pairs_per_kv + j for j in range(pairs_per_kv)]
        rows = []
        for hp in pairs:
            q2 = q_ref[:, hp * LANE:(hp + 1) * LANE]
            rows += [q2 * m_lo, q2 * m_hi]
        qs = jnp.concatenate(rows, axis=0)
        s = _mm_nt(qs, k_ref[:, ksl])
        m = jnp.max(s, axis=-1, keepdims=True)
        if cached:
            s2 = _mm_nt(qs, kc_ref[:, ksl])
            m = jnp.maximum(m, jnp.max(s2, axis=-1, keepdims=True))
        p = jnp.exp(s - m)
        l = jnp.sum(p, axis=-1, keepdims=True)
        o = _mm(p, v_ref[:, ksl])
        if cached:
            p2 = jnp.exp(s2 - m)
            l = l + jnp.sum(p2, axis=-1, keepdims=True)
            o = o + _mm(p2, vc_ref[:, ksl])
        o = o * (1.0 / l)
        for j, hp in enumerate(pairs):
            o_e = o[(2 * j) * tq:(2 * j + 1) * tq]
            o_o = o[(2 * j + 1) * tq:(2 * j + 2) * tq]
            o_ref[:, hp * LANE:(hp + 1) * LANE] = jnp.where(lo, o_e, o_o).astype(o_ref.dtype)


def _attention(q, k, v, kc, vc, tq, name):
    b, lq, _ = q.shape
    lk, kw = k.shape[1], k.shape[2]
    cached = kc is not None
    qmap = lambda bi, ti: (bi, ti, 0)
    kmap = lambda bi, ti: (bi, 0, 0)
    in_specs = [
        pl.BlockSpec((None, tq, 512), qmap),
        pl.BlockSpec((None, lk, kw), kmap),
        pl.BlockSpec((None, lk, kw), kmap),
    ]
    args = [q, k, v]
    if cached:
        lc = kc.shape[1]
        in_specs += [pl.BlockSpec((None, lc, kw), kmap)] * 2
        args += [kc, vc]
    return pl.pallas_call(
        functools.partial(_attn_kernel, kv_tiles=kw // LANE, cached=cached),
        grid=(b, lq // tq),
        in_specs=in_specs,
        out_specs=pl.BlockSpec((None, tq, 512), qmap),
        out_shape=jax.ShapeDtypeStruct((b, lq, 512), MXU_DTYPE),
        compiler_params=_params(("parallel", "parallel")),
        name=name,
    )(*args)


def _na_geometry(seq_len):
    rows = seq_len // GRID_W
    n_tiles = rows // 2
    assert rows >= NA_KEY_ROWS and NA_WIN_ROWS <= rows and NA_KEY_ROWS % 2 == 0
    ws = np.clip(2 * np.arange(n_tiles) - NA_WIN_ROWS // 2, 0, rows - NA_KEY_ROWS)
    r = 2 * np.arange(n_tiles)[:, None, None] + np.arange(2)[None, :, None]
    key_r = ws[:, None, None] + np.arange(NA_KEY_ROWS)[None, None, :]
    r0 = np.clip(r - NA_WIN_ROWS // 2, 0, rows - NA_WIN_ROWS)
    valid = (key_r >= r0) & (key_r < r0 + NA_WIN_ROWS)
    dr = np.where(valid, key_r - r + NA_WIN_ROWS - 1, 2 * NA_WIN_ROWS - 1)
    assert (valid.sum(-1) == NA_WIN_ROWS).all()
    return ws.astype(np.int32), dr.reshape(-1).astype(np.int32)


def _na_bias_blocks(rpb):
    h = rpb.shape[0]
    nrel = 2 * NA_WIN_ROWS - 1
    zeros = jnp.zeros((h, nrel, LANE - (2 * NA_WIN_COLS - 1)), F32)
    v = jnp.concatenate([rpb[..., NA_WIN_COLS - 1:], zeros, rpb[..., :NA_WIN_COLS - 1]], axis=-1).astype(F32)
    t = jnp.tile(v, (1, 1, GRID_W))[..., :GRID_W * (LANE - 1)].reshape(h, nrel, GRID_W, LANE - 1)[..., :GRID_W]
    c = np.arange(GRID_W)
    c0 = np.clip(c - NA_WIN_COLS // 2, 0, GRID_W - NA_WIN_COLS)
    colmask = (c[None, :] >= c0[:, None]) & (c[None, :] < c0[:, None] + NA_WIN_COLS)
    t = jnp.where(jnp.asarray(colmask)[None, None], t, NEG_BIG)
    t = jnp.concatenate([t, jnp.full((h, 1, GRID_W, GRID_W), NEG_BIG, F32)], axis=1)
    pad = jnp.zeros_like(t)
    return jnp.concatenate([t, pad], axis=-1), jnp.concatenate([pad, t], axis=-1)


def _na_kernel(ws_ref, dr_ref, q_ref, k_ref, v_ref, kc_ref, vc_ref, bl_ref, br_ref, o_ref):
    i = pl.program_id(1)
    start = pl.multiple_of(ws_ref[i] * GRID_W, GRID_W)
    nk = NA_KEY_ROWS * GRID_W
    tq = q_ref.shape[0]
    lo, m_lo, m_hi = _lane_masks(MXU_DTYPE)

    def head_bias(h):
        rows = []
        for qr in range(2):
            base = (i * 2 + qr) * NA_KEY_ROWS
            tiles = [bl_ref[h, dr_ref[base + 2 * kp]] + br_ref[h, dr_ref[base + 2 * kp + 1]]
                     for kp in range(NA_KEY_ROWS // 2)]
            rows.append(jnp.concatenate(tiles, axis=1))
        return jnp.concatenate(rows, axis=0)

    for hp in range(N_HEADS // 2):
        sl = slice(hp * LANE, (hp + 1) * LANE)
        q2 = q_ref[:, sl]
        qs = jnp.concatenate([q2 * m_lo, q2 * m_hi], axis=0)
        kw = k_ref[pl.ds(start, nk), sl]
        vw = v_ref[pl.ds(start, nk), sl]
        bias = jnp.concatenate([head_bias(2 * hp), head_bias(2 * hp + 1)], axis=0)
        s = _mm_nt(qs, kw) + bias
        s2 = _mm_nt(qs, kc_ref[:, sl])
        m = jnp.maximum(jnp.max(s, axis=-1, keepdims=True), jnp.max(s2, axis=-1, keepdims=True))
        p = jnp.exp(s - m)
        p2 = jnp.exp(s2 - m)
        l = jnp.sum(p, axis=-1, keepdims=True) + jnp.sum(p2, axis=-1, keepdims=True)
        o = (_mm(p, vw) + _mm(p2, vc_ref[:, sl])) * (1.0 / l)
        o_ref[:, sl] = jnp.where(lo, o[:tq], o[tq:]).astype(o_ref.dtype)


def _neighbourhood_attention(q, k, v, kc, vc, rpb):
    b, seq_len, _ = q.shape
    lc = kc.shape[1]
    ws, dr = _na_geometry(seq_len)
    b_left, b_right = _na_bias_blocks(rpb)
    n_tiles = len(ws)
    tq = 2 * GRID_W
    qmap = lambda bi, ti, ws_r, dr_r: (bi, ti, 0)
    kmap = lambda bi, ti, ws_r, dr_r: (bi, 0, 0)
    bmap = lambda bi, ti, ws_r, dr_r: (0, 0, 0, 0)
    grid_spec = pltpu.PrefetchScalarGridSpec(
        num_scalar_prefetch=2,
        grid=(b, n_tiles),
        in_specs=[
            pl.BlockSpec((None, tq, 512), qmap),
            pl.BlockSpec((None, seq_len, 512), kmap),
            pl.BlockSpec((None, seq_len, 512), kmap),
            pl.BlockSpec((None, lc, 512), kmap),
            pl.BlockSpec((None, lc, 512), kmap),
            pl.BlockSpec(b_left.shape, bmap),
            pl.BlockSpec(b_right.shape, bmap),
        ],
        out_specs=pl.BlockSpec((None, tq, 512), qmap),
    )
    return pl.pallas_call(
        _na_kernel,
        grid_spec=grid_spec,
        out_shape=jax.ShapeDtypeStruct((b, seq_len, 512), MXU_DTYPE),
        compiler_params=_params(("parallel", "arbitrary")),
        name="na_attn",
    )(jnp.asarray(ws), jnp.asarray(dr), q, k, v, kc, vc, b_left, b_right)


def _cmul(ar, ai, br, bi):
    return ar * br - ai * bi, ar * bi + ai * br


def _lam_bar(lr, li, ls):
    dt = jnp.exp(ls)
    mag = jnp.exp(lr * dt)
    return mag * jnp.cos(li * dt), mag * jnp.sin(li * dt)


def _zoh_coef(lr, li, zr, zi):
    nr, ni = zr - 1.0, zi
    den = 1.0 / (lr * lr + li * li)
    return (nr * lr + ni * li) * den, (ni * lr - nr * li) * den


def _squarings(zr, zi, n):
    out = [(zr, zi)]
    for _ in range(n - 1):
        zr, zi = _cmul(zr, zi, zr, zi)
        out.append((zr, zi))
    return out


def _cpow(squares, e):
    pr, pi = jnp.ones(e.shape, F32), jnp.zeros(e.shape, F32)
    for k, (zr, zi) in enumerate(squares):
        bit = jnp.bitwise_and(jnp.right_shift(e, k), 1) == 1
        nr, ni = _cmul(pr, pi, zr, zi)
        pr, pi = jnp.where(bit, nr, pr), jnp.where(bit, ni, pi)
    return pr, pi


def _ssm_ops_kernel(lrr_ref, lir_ref, lsr_ref, lrc_ref, lic_ref, lsc_ref, bt_ref, ct_ref, s0_ref, s1_ref,
                    wt_o, wb_o, wc_o, l16_o, tg_ref):
    tc, hg, p = SSM_CHUNK, SSM_GROUP_CH, SSM_STATE
    w = tc * hg
    lane_w = lax.broadcasted_iota(jnp.int32, (1, w), 1)
    lane_p = lax.broadcasted_iota(jnp.int32, (1, 2 * p), 1)
    row_w = lax.broadcasted_iota(jnp.int32, (2 * w, 1), 0)
    row_p = lax.broadcasted_iota(jnp.int32, (2 * p, 1), 0)
    tau_of_lane = jnp.right_shift(lane_w, 4)
    gl_of_lane = jnp.right_shift(lane_p, 6)
    step_of_row = jnp.right_shift(row_w, 5)
    same_group = jnp.bitwise_and(jnp.right_shift(row_w, 4), 1) == gl_of_lane
    first_rows = row_p < p
    tg_ref[...] = jnp.zeros_like(tg_ref)
    l16_rows = []
    for d in range(2):
        lr, li = lrr_ref[d], lir_ref[d]
        zr, zi = _lam_bar(lr, li, lsr_ref[d])
        cfr, cfi = _zoh_coef(lr, li, zr, zi)
        btr, bti = _cmul(cfr, cfi, bt_ref[d, 0], bt_ref[d, 1])
        sq = _squarings(zr, zi, 5)
        e_inj = (tc - 1 - step_of_row) if d == 0 else step_of_row
        pr, pi = _cpow(sq[:4], jnp.broadcast_to(e_inj, (2 * w, 2 * p)))
        ir, ii = _cmul(pr, pi, jnp.tile(btr, (2 * tc, 1)), jnp.tile(bti, (2 * tc, 1)))
        wb_o[:, (2 * d) * LANE:(2 * d + 1) * LANE] = jnp.where(same_group, ir, 0.0).astype(wb_o.dtype)
        wb_o[:, (2 * d + 1) * LANE:(2 * d + 2) * LANE] = jnp.where(same_group, ii, 0.0).astype(wb_o.dtype)
        l16_rows += [sq[4][0], sq[4][1]]
        lrc, lic = lrc_ref[d], lic_ref[d]
        zcr, zci = _lam_bar(lrc, lic, lsc_ref[d])
        tau = tau_of_lane if d == 0 else (tc - 1) - tau_of_lane
        pr, pi = _cpow(_squarings(zcr, zci, 4), jnp.broadcast_to(tau, (2 * p, w)))
        c0r, c0i = _cmul(ct_ref[d, 0], ct_ref[d, 1], pr, pi)
        c1r, c1i = _cmul(c0r, c0i, zcr, zci)
        for r, val in ((2 * d, c1r), (2 * d + 1, -c1i)):
            vb = val.astype(MXU_DTYPE)
            spread = jnp.where(first_rows, jnp.dot(vb, s0_ref[...], preferred_element_type=F32),
                               jnp.dot(vb, s1_ref[...], preferred_element_type=F32))
            wc_o[r * LANE:(r + 1) * LANE, :] = spread.astype(wc_o.dtype)
        for gl in range(2):
            in_group = gl_of_lane == gl
            kt = (jnp.dot(jnp.where(in_group, btr, 0.0), c0r, preferred_element_type=F32,
                          precision=lax.Precision.HIGHEST)
                  - jnp.dot(jnp.where(in_group, bti, 0.0), c0i, preferred_element_type=F32,
                            precision=lax.Precision.HIGHEST))
            for s in range(tc):
                if d == 0:
                    shift, keep = hg * s, lane_w >= hg * s
                else:
                    shift, keep = (w - hg * (tc - 1 - s)) % w, lane_w < hg * (s + 1)
                rolled = kt if shift == 0 else pltpu.roll(kt, shift, 1)
                tg_ref[gl, s * hg:(s + 1) * hg, :] += jnp.where(keep, rolled, 0.0)
    for gl, s_ref in enumerate((s0_ref, s1_ref)):
        spread = jnp.dot(tg_ref[gl].astype(MXU_DTYPE), s_ref[...], preferred_element_type=F32)
        for s in range(tc):
            wt_o[(2 * s + gl) * hg:(2 * s + gl + 1) * hg, :] = spread[s * hg:(s + 1) * hg].astype(wt_o.dtype)
    l16_o[...] = jnp.concatenate(l16_rows + [jnp.zeros((4, 2 * p), F32)], axis=0)


def _ssm_operators(lam_re, lam_im, log_step, b_re, b_im, c_re, c_im):
    depth = lam_re.shape[0]
    p, hg, tc, q = SSM_STATE, SSM_GROUP_CH, SSM_CHUNK, SSM_PAIRS
    assert (hg, p, tc) == (16, 64, 16), "lane/row index arithmetic in the kernel uses these as shifts"
    w = tc * hg

    def per_pair(a, tail):
        a = a.astype(F32).reshape((depth, 2, q, 2) + tail)
        return jnp.transpose(a, (0, 2, 1, 3) + tuple(range(4, 4 + len(tail))))

    lam_r, lam_i = per_pair(lam_re, (p,)), per_pair(lam_im, (p,))
    ls = jnp.broadcast_to(per_pair(log_step, ())[..., None], lam_r.shape)
    rows = [a.reshape(depth, q, 2, 1, 2 * p) for a in (lam_r, lam_i, ls)]
    cols = [a.reshape(depth, q, 2, 2 * p, 1) for a in (lam_r, lam_i, ls)]
    bt = jnp.stack([per_pair(b_re, (p, hg)), per_pair(b_im, (p, hg))], axis=3)
    bt = jnp.transpose(bt, (0, 1, 2, 3, 6, 4, 5)).reshape(depth, q, 2, 2, hg, 2 * p)
    ct = jnp.stack([per_pair(c_re, (hg, p)), per_pair(c_im, (hg, p))], axis=3)
    ct = jnp.transpose(ct, (0, 1, 2, 3, 4, 6, 5)).reshape(depth, q, 2, 2, 2 * p, hg)
    ct = jnp.tile(ct, (1, 1, 1, 1, 1, tc))
    r, c = np.arange(w)[:, None], np.arange(2 * w)[None, :]
    hit = (r // hg == c // (2 * hg)) & (r % hg == c % hg)
    spread = [jnp.asarray(hit & ((c // hg) % 2 == gl), MXU_DTYPE) for gl in range(2)]

    blk = lambda a: pl.BlockSpec((None, None) + a.shape[2:], lambda l, i: (l, i) + (0,) * (a.ndim - 2))
    const = pl.BlockSpec((w, 2 * w), lambda l, i: (0, 0))
    mat = pl.BlockSpec((None, None, 2 * w, 2 * w), lambda l, i: (l, i, 0, 0))
    args = rows + cols + [bt, ct]
    return pl.pallas_call(
        _ssm_ops_kernel,
        grid=(depth, q),
        in_specs=[blk(a) for a in args] + [const, const],
        out_specs=[mat, mat, mat, pl.BlockSpec((None, None, 8, 2 * p), lambda l, i: (l, i, 0, 0))],
        out_shape=[jax.ShapeDtypeStruct((depth, q, 2 * w, 2 * w), MXU_DTYPE)] * 3
        + [jax.ShapeDtypeStruct((depth, q, 8, 2 * p), F32)],
        scratch_shapes=[pltpu.VMEM((2, w, w), F32)],
        compiler_params=_params(("parallel", "parallel")),
        name="ssm_ops",
    )(*args, *spread)


_PAIRS_PER_TILE = 4
_SSM_ROW_BLOCK = 64


def _ssm_kernel(zu_ref, wt_ref, wb_ref, wc_ref, l16_ref, h0_ref, y_o, fin_o, u_ref, yp_ref, *state_refs,
                batch, seq_len):
    tc, npair = SSM_CHUNK, _PAIRS_PER_TILE
    n_chunks = seq_len // tc
    nrows = batch * n_chunks
    rb = _SSM_ROW_BLOCK
    slot_w = LANE // npair
    slot = jnp.right_shift(lax.broadcasted_iota(jnp.int32, (1, LANE), 1), 5)
    dx_refs, xs_refs = state_refs[:4], state_refs[4:]

    def place(pieces, src_slot):
        offset = src_slot
        out = None
        for j, piece in enumerate(pieces):
            shift = (slot_w * (j - offset[j])) % LANE
            r = piece if shift == 0 else pltpu.roll(piece, shift, 1)
            out = r if out is None else jnp.where(slot == j, r, out)
        return out

    def gather_block(i, carry):
        r0 = pl.multiple_of(i * rb, rb)
        steps = [zu_ref[pl.ds(r0 * tc + s, rb, stride=tc), :] for s in range(tc)]
        for p in range(npair):
            tiles = [place(steps[4 * k:4 * k + 4], [p] * 4) for k in range(tc // 4)]
            u_ref[p, pl.ds(r0, rb), :] = jnp.concatenate(tiles, axis=1).astype(u_ref.dtype)
        return carry

    lax.fori_loop(0, nrows // rb, gather_block, 0)

    for p in range(npair):
        u = u_ref[p]
        y_intra = jnp.dot(u, wt_ref[p], preferred_element_type=F32)
        dx = jnp.dot(u, wb_ref[p], preferred_element_type=F32)
        for r in range(4):
            dx_refs[r][...] = dx[:, r * LANE:(r + 1) * LANE]
        lfr, lfi, lbr, lbi = (l16_ref[p, r:r + 1, :] for r in range(4))

        def body(c, carry):
            fr, fi, br, bi = carry
            fwd = pl.ds(c, batch, stride=n_chunks)
            bwd = pl.ds(n_chunks - 1 - c, batch, stride=n_chunks)
            xs_refs[0][fwd, :] = fr
            xs_refs[1][fwd, :] = fi
            xs_refs[2][bwd, :] = br
            xs_refs[3][bwd, :] = bi
            nfr = lfr * fr - lfi * fi + dx_refs[0][fwd, :]
            nfi = lfr * fi + lfi * fr + dx_refs[1][fwd, :]
            nbr = lbr * br - lbi * bi + dx_refs[2][bwd, :]
            nbi = lbr * bi + lbi * br + dx_refs[3][bwd, :]
            return nfr, nfi, nbr, nbi

        fin = lax.fori_loop(0, n_chunks, body, tuple(h0_ref[p, :, r * LANE:(r + 1) * LANE] for r in range(4)))
        for r in range(4):
            fin_o[p, :, r * LANE:(r + 1) * LANE] = fin[r]
        xs = jnp.concatenate([x[...] for x in xs_refs], axis=1).astype(MXU_DTYPE)
        yp_ref[p] = y_intra + jnp.dot(xs, wc_ref[p], preferred_element_type=F32)

    def scatter_block(i, carry):
        r0 = pl.multiple_of(i * rb, rb)
        for k in range(tc // 4):
            pieces = [yp_ref[p, pl.ds(r0, rb), k * LANE:(k + 1) * LANE] for p in range(npair)]
            for j in range(4):
                y_o[pl.ds(r0 * tc + 4 * k + j, rb, stride=tc), :] = place(pieces, [j] * npair)
        return carry

    lax.fori_loop(0, nrows // rb, scatter_block, 0)


def _ssm_scan(zu2d, w_t, w_b, w_c, l16, h0, batch, seq_len):
    t = zu2d.shape[0]
    nrows = t // SSM_CHUNK
    npair = _PAIRS_PER_TILE
    assert nrows % _SSM_ROW_BLOCK == 0 and seq_len % SSM_CHUNK == 0
    once = pl.Buffered(1)
    wspec = pl.BlockSpec((npair, 512, 512), lambda i: (i, 0, 0))
    return pl.pallas_call(
        functools.partial(_ssm_kernel, batch=batch, seq_len=seq_len),
        grid=(SSM_PAIRS // npair,),
        in_specs=[
            pl.BlockSpec((t, LANE), lambda i: (0, i), pipeline_mode=once),
            wspec, wspec, wspec,
            pl.BlockSpec((npair, 8, LANE), lambda i: (i, 0, 0)),
            pl.BlockSpec((npair, batch, 512), lambda i: (i, 0, 0)),
        ],
        out_specs=[
            pl.BlockSpec((t, LANE), lambda i: (0, i), pipeline_mode=once),
            pl.BlockSpec((npair, batch, 512), lambda i: (i, 0, 0)),
        ],
        out_shape=[jax.ShapeDtypeStruct((t, SSM_WIDTH), F32),
                   jax.ShapeDtypeStruct((SSM_PAIRS, batch, 512), F32)],
        scratch_shapes=[pltpu.VMEM((npair, nrows, 512), MXU_DTYPE), pltpu.VMEM((npair, nrows, 512), F32)]
        + [pltpu.VMEM((nrows, LANE), F32)] * 8,
        compiler_params=_params(("arbitrary",)),
        name="ssm_scan",
    )(zu2d, w_t, w_b, w_c, l16, h0)


def _pack_state(s_re, s_im):
    b = s_re.shape[0]
    a = jnp.stack([s_re, s_im], axis=2).reshape(b, 2, 2, SSM_PAIRS, 2 * SSM_STATE)
    return jnp.transpose(a, (3, 0, 1, 2, 4)).reshape(SSM_PAIRS, b, 4 * 2 * SSM_STATE).astype(F32)


def _unpack_state(fin):
    b = fin.shape[1]
    a = jnp.transpose(fin.reshape(SSM_PAIRS, b, 2, 2, 2 * SSM_STATE), (1, 2, 3, 0, 4))
    a = a.reshape(b, 2, 2, SSM_GROUPS, SSM_STATE)
    return a[:, :, 0], a[:, :, 1]


def _merge_kernel(ya_ref, ys_ref, u_ref, yc_ref, g_ref, x_ref, mod_ref, ng_ref, d_ref,
                  wglu_ref, wa_ref, wb_ref, wc_ref, wo_ref, x_o, h_o):
    y = ys_ref[...].astype(F32) + d_ref[...] * u_ref[...]
    gl = _gelu_tanh(y)
    yb = gl * _sigmoid(_mm(gl, wglu_ref[...]))
    merged = (g_ref[:, 0:D_MODEL].astype(F32) * _mm(ya_ref[...], wa_ref[...])
              + g_ref[:, D_MODEL:2 * D_MODEL].astype(F32) * _mm(yb, wb_ref[...])
              + g_ref[:, 2 * D_MODEL:3 * D_MODEL].astype(F32) * _mm(yc_ref[...], wc_ref[...]))
    x1 = x_ref[...] + mod_ref[2:3, :] * _rms(_mm(merged, wo_ref[...]), ng_ref[1:2, :])
    x_o[...] = x1
    h_o[...] = (_rms(x1, ng_ref[2:3, :]) * (1.0 + mod_ref[4:5, :]) + mod_ref[3:4, :]).astype(h_o.dtype)


def _merge(ya, ys, zu, yc, gates, x2d, mod_l, mod_row, ng, ssm_d, w_glu, w_a, w_b, w_c, w_o):
    t = x2d.shape[0]
    tm = TOKEN_TILE
    row = lambda i: (i, 0)
    const = lambda i: (0, 0)
    r512 = pl.BlockSpec((tm, 512), row)
    wbr = pl.BlockSpec((512, D_MODEL), const)
    return pl.pallas_call(
        _merge_kernel,
        grid=(t // tm,),
        in_specs=[
            r512, r512, r512, r512,
            pl.BlockSpec((tm, 3 * D_MODEL), row),
            pl.BlockSpec((tm, D_MODEL), row),
            pl.BlockSpec((None, 6, D_MODEL), lambda i: (mod_row(i), 0, 0)),
            pl.BlockSpec((4, D_MODEL), const),
            pl.BlockSpec((1, 512), const),
            pl.BlockSpec((512, 512), const),
            wbr, wbr, wbr,
            pl.BlockSpec((D_MODEL, D_MODEL), const),
        ],
        out_specs=[pl.BlockSpec((tm, D_MODEL), row), pl.BlockSpec((tm, D_MODEL), row)],
        out_shape=[jax.ShapeDtypeStruct((t, D_MODEL), F32), jax.ShapeDtypeStruct((t, D_MODEL), MXU_DTYPE)],
        compiler_params=_params(("parallel",)),
        name="merge",
    )(ya, ys, zu, yc, gates, x2d, mod_l, ng, ssm_d, w_glu, w_a, w_b, w_c, w_o)


def _ffn_kernel(h_ref, hp_ref, hn_ref, x_ref, mod_ref, ng_ref, wa_ref, wg_ref, cwa_ref, cwg_ref,
                cba_ref, cbg_ref, wd_ref, x_o, acc_ref, ua_ref, ug_ref, act_ref, *, seq_len):
    i, j = pl.program_id(0), pl.program_id(1)
    tm = h_ref.shape[0]
    ft = wd_ref.shape[0]
    n = tm + 16
    hh = jnp.concatenate([hp_ref[...], h_ref[...], hn_ref[...]], axis=0)
    ua_ref[...] = jnp.dot(hh, wa_ref[...], preferred_element_type=F32)
    ug_ref[...] = jnp.dot(hh, wg_ref[...], preferred_element_type=F32)
    pos = jnp.bitwise_and(i * tm + lax.broadcasted_iota(jnp.int32, (tm, 1), 0), seq_len - 1)
    has_prev = (pos != 0).astype(F32)
    has_next = (pos != seq_len - 1).astype(F32)

    def conv(u_ref, cw_ref, cb_ref, lo):
        uc = u_ref[:, lo:lo + LANE]
        up = pltpu.roll(uc, 1, 0)[8:8 + tm] * has_prev
        un = pltpu.roll(uc, n - 1, 0)[8:8 + tm] * has_next
        return (cw_ref[0:1, lo:lo + LANE] * up + cw_ref[1:2, lo:lo + LANE] * uc[8:8 + tm]
                + cw_ref[2:3, lo:lo + LANE] * un + cb_ref[0:1, lo:lo + LANE])

    for kc in range(ft // LANE):
        a = conv(ua_ref, cwa_ref, cba_ref, kc * LANE)
        g = conv(ug_ref, cwg_ref, cbg_ref, kc * LANE)
        act_ref[:, kc * LANE:(kc + 1) * LANE] = (g * _sigmoid(g) * a).astype(act_ref.dtype)
    part = jnp.dot(act_ref[...], wd_ref[...], preferred_element_type=F32)

    @pl.when(j == 0)
    def _():
        acc_ref[...] = part

    @pl.when(j != 0)
    def _():
        acc_ref[...] += part

    @pl.when(j == pl.num_programs(1) - 1)
    def _():
        x_o[...] = x_ref[...] + mod_ref[5:6, :] * _rms(acc_ref[...], ng_ref[3:4, :])


def _conv_ffn(h2, x1, mod_l, mod_row, ng, w_up, conv_w, conv_b, w_down, seq_len):
    t = x1.shape[0]
    tm = TOKEN_TILE
    ft = FF_TILE
    nf = D_FF // ft
    nblk8 = t // 8
    assert seq_len & (seq_len - 1) == 0 and (seq_len % tm == 0 or tm % seq_len == 0)
    row = lambda i, j: (i, 0)
    return pl.pallas_call(
        functools.partial(_ffn_kernel, seq_len=seq_len),
        grid=(t // tm, nf),
        in_specs=[
            pl.BlockSpec((tm, D_MODEL), row),
            pl.BlockSpec((8, D_MODEL), lambda i, j: (jnp.maximum(i * (tm // 8) - 1, 0), 0)),
            pl.BlockSpec((8, D_MODEL), lambda i, j: (jnp.minimum((i + 1) * (tm // 8), nblk8 - 1), 0)),
            pl.BlockSpec((tm, D_MODEL), row),
            pl.BlockSpec((None, 6, D_MODEL), lambda i, j: (mod_row(i), 0, 0)),
            pl.BlockSpec((4, D_MODEL), lambda i, j: (0, 0)),
            pl.BlockSpec((D_MODEL, ft), lambda i, j: (0, j)),
            pl.BlockSpec((D_MODEL, ft), lambda i, j: (0, nf + j)),
            pl.BlockSpec((3, ft), lambda i, j: (0, j)),
            pl.BlockSpec((3, ft), lambda i, j: (0, nf + j)),
            pl.BlockSpec((1, ft), lambda i, j: (0, j)),
            pl.BlockSpec((1, ft), lambda i, j: (0, nf + j)),
            pl.BlockSpec((ft, D_MODEL), lambda i, j: (j, 0)),
        ],
        out_specs=pl.BlockSpec((tm, D_MODEL), row),
        out_shape=jax.ShapeDtypeStruct((t, D_MODEL), F32),
        scratch_shapes=[pltpu.VMEM((tm, D_MODEL), F32), pltpu.VMEM((tm + 16, ft), F32),
                        pltpu.VMEM((tm + 16, ft), F32), pltpu.VMEM((tm, ft), MXU_DTYPE)],
        compiler_params=_params(("parallel", "arbitrary")),
        name="conv_ffn",
    )(h2, h2, h2, x1, mod_l, ng, w_up, w_up, conv_w, conv_w, conv_b, conv_b, w_down)


_Q_HEAD_ORDER = (0, 4, 1, 5, 2, 6, 3, 7)


def _rope_tables(seq_len):
    nf = HEAD_DIM // 4
    t = np.arange(seq_len)
    pos = np.stack([t // GRID_W, t % GRID_W]).astype(np.float32)
    inv = jnp.asarray(ROPE_THETA, F32) ** (-jnp.arange(nf, dtype=F32) / nf)
    ang = jnp.asarray(pos)[:, :, None] * inv
    d = np.arange(HEAD_DIM)
    ang = ang[d // (2 * nf), :, d % nf].T
    second = jnp.asarray(((d % (2 * nf)) // nf) == 1)[None, :]
    cos, sin = jnp.cos(ang), jnp.sin(ang)
    tabs = (cos, jnp.where(second, 0.0, -sin), jnp.where(second, sin, 0.0))
    return tuple(jnp.tile(x, (1, LANE // HEAD_DIM)).astype(F32) for x in tabs)


def _layer_weights(w_in, qk_g, w_br_a):
    hd = HEAD_DIM
    w_in_p = jnp.concatenate([w_in[:, h * hd:(h + 1) * hd] for h in _Q_HEAD_ORDER] + [w_in[:, 512:]],
                             axis=1).astype(MXU_DTYPE)
    w_a_p = jnp.concatenate([w_br_a[h * hd:(h + 1) * hd] for h in _Q_HEAD_ORDER], axis=0).astype(MXU_DTYPE)
    qg = jnp.tile(qk_g[0], N_HEADS).reshape(1, 512).astype(F32)
    kg = jnp.tile(qk_g[1], GA_KV_HEADS).reshape(1, LANE).astype(F32)
    return w_in_p, w_a_p, qg, kg


def kernel(x_prompt, x_sample, c, cache_ga_k, cache_ga_v, cache_na_k, cache_na_v, state_ssm_re, state_ssm_im,
           c_ctx, w_mod, b_mod, norm_g, w_in, qk_norm_g, na_rpb, ssm_lam_re, ssm_lam_im, ssm_log_step,
           ssm_b_re, ssm_b_im, ssm_c_re, ssm_c_im, ssm_d, w_glu, w_br_a, w_br_b, w_br_c, w_out,
           w_up, conv_w, conv_b, w_down):
    depth = w_in.shape[0]
    bp, lp, _ = x_prompt.shape
    bs, ls, _ = x_sample.shape
    lc = cache_ga_k.shape[2]
    assert lp % 256 == 0 and ls % TOKEN_TILE == 0 and (bp * lp) % TOKEN_TILE == 0
    assert bs % 8 == 0 and bp % 8 == 0, "the scan keeps one batch row per sublane"

    rows = 1 + bs
    rows_p = -(-rows // 8) * 8
    cvec = jnp.concatenate([c_ctx[None], c, jnp.zeros((rows_p - rows, D_MODEL), F32)], axis=0)
    mod = _modulation(cvec, w_mod, b_mod).reshape(depth, rows_p, 6, D_MODEL)

    w_t, w_b, w_c, l16 = _ssm_operators(ssm_lam_re, ssm_lam_im, ssm_log_step, ssm_b_re, ssm_b_im,
                                        ssm_c_re, ssm_c_im)
    seg = jnp.asarray(np.kron(np.eye(N_HEADS), np.full((HEAD_DIM, HEAD_DIM), 1.0 / HEAD_DIM)), MXU_DTYPE)
    rope_tabs = _rope_tables(ls)
    tiles_per_sample = ls // TOKEN_TILE
    ctx_row = lambda i: 0
    lat_row = lambda i: 1 + i // tiles_per_sample

    y_p = x_prompt.reshape(bp * lp, D_MODEL)
    y_s = x_sample.reshape(bs * ls, D_MODEL)
    zero_state = jnp.zeros((SSM_PAIRS, bp, 512), F32)
    outs = [[] for _ in range(6)]
    for l in range(depth):
        w_in_p, w_a_p, qg, kg = _layer_weights(w_in[l], qk_norm_g[l], w_br_a[l])
        w_glu_l, w_b_l, w_c_l, w_o_l = (a[l].astype(MXU_DTYPE) for a in (w_glu, w_br_b, w_br_c, w_out))
        ffn_w = (w_up[l].astype(MXU_DTYPE), conv_w[l].astype(F32), conv_b[l].reshape(1, 2 * D_FF).astype(F32),
                 w_down[l].astype(MXU_DTYPE))
        d_l = ssm_d[l].reshape(1, SSM_WIDTH).astype(F32)
        ng = norm_g[l].astype(F32)
        ssm_ops = (w_t[l], w_b[l], w_c[l], l16[l])

        q, k, v, zu, nq, nk, nv, gates = _in_projection(
            y_p, mod[l], ctx_row, ng, w_in_p, qg, kg, seg, None, lp, F32)
        r3 = lambda a: a.reshape(bp, lp, a.shape[-1])
        ya = _attention(r3(q), r3(k), r3(v), None, None, lp, "ga_ctx")
        yc = _attention(r3(nq), r3(nk), r3(nv), None, None, lp, "na_ctx")
        ys, fin = _ssm_scan(zu, *ssm_ops, zero_state, bp, lp)
        x1, h2 = _merge(ya.reshape(-1, 512), ys, zu, yc.reshape(-1, 512), gates, y_p,
                        mod[l], ctx_row, ng, d_l, w_glu_l, w_a_p, w_b_l, w_c_l, w_o_l)
        y_p = _conv_ffn(h2, x1, mod[l], ctx_row, ng, *ffn_w, lp)
        f_re, f_im = _unpack_state(fin)
        for lst, a in zip(outs, (k.reshape(bp, lp, GA_KV_HEADS, HEAD_DIM), v.reshape(bp, lp, GA_KV_HEADS, HEAD_DIM),
                                 nk.reshape(bp, lp, N_HEADS, HEAD_DIM), nv.reshape(bp, lp, N_HEADS, HEAD_DIM),
                                 f_re, f_im)):
            lst.append(a)

        q, k, v, zu, nq, nk, nv, gates = _in_projection(
            y_s, mod[l], lat_row, ng, w_in_p, qg, kg, seg, rope_tabs, ls, MXU_DTYPE)
        r3 = lambda a: a.reshape(bs, ls, a.shape[-1])
        ck = cache_ga_k[:, l].reshape(bs, lc, LANE)
        cv = cache_ga_v[:, l].reshape(bs, lc, LANE)
        ya = _attention(r3(q), r3(k), r3(v), ck, cv, 2 * GRID_W, "ga_lat")
        nck = cache_na_k[:, l].reshape(bs, lc, 512)
        ncv = cache_na_v[:, l].reshape(bs, lc, 512)
        yc = _neighbourhood_attention(r3(nq), r3(nk), r3(nv), nck, ncv, na_rpb[l])
        h0 = _pack_state(state_ssm_re[:, l], state_ssm_im[:, l])
        ys, _ = _ssm_scan(zu, *ssm_ops, h0, bs, ls)
        x1, h2 = _merge(ya.reshape(-1, 512), ys, zu, yc.reshape(-1, 512), gates, y_s,
                        mod[l], lat_row, ng, d_l, w_glu_l, w_a_p, w_b_l, w_c_l, w_o_l)
        y_s = _conv_ffn(h2, x1, mod[l], lat_row, ng, *ffn_w, ls)

    new = [jnp.stack(lst, axis=1) for lst in outs]
    return (y_p.reshape(bp, lp, D_MODEL), y_s.reshape(bs, ls, D_MODEL), *new)
```

```python
import functools
import math

import numpy as np
import jax
import jax.numpy as jnp
from jax import lax
from jax.experimental import pallas as pl
from jax.experimental.pallas import tpu as pltpu

F32 = jnp.float32
MXU_DTYPE = jnp.bfloat16

D_MODEL = 1024
HEAD_DIM = 64
N_HEADS = 8
GA_KV_HEADS = 2
GRID_W = 64
NA_WIN_ROWS = 8
NA_WIN_COLS = 16
NA_KEY_ROWS = 10
SSM_WIDTH = 512
SSM_GROUPS = 32
SSM_GROUP_CH = 16
SSM_STATE = 64
SSM_CHUNK = 16
SSM_PAIRS = SSM_GROUPS // 2
D_FF = 2816
FF_TILE = 1408
ROPE_THETA = 10000.0
EPS = 1e-6
IN_WIDTH = 5888
NEG_BIG = -1e30

LANE = 128
TOKEN_TILE = 512
VMEM_LIMIT = 56 * 1024 * 1024

_Q0, _K0, _V0, _U0, _NQ0, _NK0, _NV0, _G0 = 0, 512, 640, 768, 1280, 1792, 2304, 2816


def _sigmoid(x):
    return 1.0 / (1.0 + jnp.exp(-x))


def _gelu_tanh(x):
    return 0.5 * x * (1.0 + jnp.tanh(math.sqrt(2.0 / math.pi) * (x + 0.044715 * (x * x * x))))


def _rms(x, g):
    ms = jnp.mean(x * x, axis=-1, keepdims=True)
    return (x * lax.rsqrt(ms + EPS)) * g


def _mm(a, b):
    return jnp.dot(a.astype(MXU_DTYPE), b.astype(MXU_DTYPE), preferred_element_type=F32)


def _mm_nt(a, b):
    return lax.dot_general(a.astype(MXU_DTYPE), b.astype(MXU_DTYPE), (((1,), (1,)), ((), ())),
                           preferred_element_type=F32)


def _params(sem):
    return pltpu.CompilerParams(dimension_semantics=sem, vmem_limit_bytes=VMEM_LIMIT)


def _mod_kernel(c_ref, w_ref, b_ref, o_ref):
    c = c_ref[...]
    o_ref[...] = _mm(c * _sigmoid(c), w_ref[...]) + b_ref[...]


def _modulation(cvec, w_mod, b_mod):
    depth = w_mod.shape[0]
    rows = cvec.shape[0]
    tn = 1536
    return pl.pallas_call(
        _mod_kernel,
        grid=(depth, 6 * D_MODEL // tn),
        in_specs=[
            pl.BlockSpec((rows, D_MODEL), lambda l, j: (0, 0)),
            pl.BlockSpec((None, D_MODEL, tn), lambda l, j: (l, 0, j)),
            pl.BlockSpec((None, 1, tn), lambda l, j: (l, 0, j)),
        ],
        out_specs=pl.BlockSpec((None, rows, tn), lambda l, j: (l, 0, j)),
        out_shape=jax.ShapeDtypeStruct((depth, rows, 6 * D_MODEL), F32),
        compiler_params=_params(("parallel", "parallel")),
        name="adaln_mod",
    )(cvec, w_mod, b_mod.reshape(depth, 1, 6 * D_MODEL))


def _head_rms(z, seg, gain):
    ms = jnp.dot((z * z).astype(MXU_DTYPE), seg, preferred_element_type=F32)
    return (z * lax.rsqrt(ms + EPS)) * gain


def _rope_tile(t, c, s_up, s_dn):
    return t * c + pltpu.roll(t, LANE - 16, 1) * s_up + pltpu.roll(t, 16, 1) * s_dn


def _inproj_kernel(*refs, rope):
    if rope:
        (x_ref, mod_ref, ng_ref, w_ref, qg_ref, kg_ref, seg_ref, cos_ref, sup_ref, sdn_ref,
         q_o, k_o, v_o, u_o, nq_o, nk_o, nv_o, g_o) = refs
    else:
        (x_ref, mod_ref, ng_ref, w_ref, qg_ref, kg_ref, seg_ref,
         q_o, k_o, v_o, u_o, nq_o, nk_o, nv_o, g_o) = refs
    x = x_ref[...]
    h = _rms(x, ng_ref[0:1, :]) * (1.0 + mod_ref[1:2, :]) + mod_ref[0:1, :]
    hb = h.astype(MXU_DTYPE)
    scale = HEAD_DIM ** -0.5

    def proj(lo, width):
        return jnp.dot(hb, w_ref[:, lo:lo + width], preferred_element_type=F32)

    def maybe_rope(z):
        if not rope:
            return z
        c, su, sd = cos_ref[...], sup_ref[...], sdn_ref[...]
        tiles = [_rope_tile(z[:, i * LANE:(i + 1) * LANE], c, su, sd) for i in range(z.shape[1] // LANE)]
        return tiles[0] if len(tiles) == 1 else jnp.concatenate(tiles, axis=1)

    q = maybe_rope(_head_rms(proj(_Q0, 512), seg_ref[...], qg_ref[...]))
    q_o[...] = (q * scale).astype(q_o.dtype)
    k = maybe_rope(_head_rms(proj(_K0, 128), seg_ref[0:LANE, 0:LANE], kg_ref[...]))
    k_o[...] = k.astype(k_o.dtype)
    v_o[...] = proj(_V0, 128).astype(v_o.dtype)
    u_o[...] = proj(_U0, 512).astype(u_o.dtype)
    nq_o[...] = (proj(_NQ0, 512) * scale).astype(nq_o.dtype)
    nk_o[...] = proj(_NK0, 512).astype(nk_o.dtype)
    nv_o[...] = proj(_NV0, 512).astype(nv_o.dtype)
    for i in range(3):
        g_o[:, i * D_MODEL:(i + 1) * D_MODEL] = _sigmoid(proj(_G0 + i * D_MODEL, D_MODEL)).astype(g_o.dtype)


def _in_projection(x2d, mod_l, mod_row, ng, w_in, qg, kg, seg, rope_tabs, seq_len, kv_dtype):
    t = x2d.shape[0]
    tm = TOKEN_TILE
    tiles_per_seq = max(seq_len // tm, 1)
    rope = rope_tabs is not None
    row = lambda i: (i, 0)
    const = lambda i: (0, 0)
    in_specs = [
        pl.BlockSpec((tm, D_MODEL), row),
        pl.BlockSpec((None, 6, D_MODEL), lambda i: (mod_row(i), 0, 0)),
        pl.BlockSpec((4, D_MODEL), const),
        pl.BlockSpec((D_MODEL, IN_WIDTH), const),
        pl.BlockSpec((1, 512), const),
        pl.BlockSpec((1, LANE), const),
        pl.BlockSpec((512, 512), const),
    ]
    args = [x2d, mod_l, ng, w_in, qg, kg, seg]
    if rope:
        in_specs += [pl.BlockSpec((tm, LANE), lambda i: (i % tiles_per_seq, 0))] * 3
        args += list(rope_tabs)
    widths = (512, 128, 128, 512, 512, 512, 512, 3 * D_MODEL)
    dtypes = (MXU_DTYPE, kv_dtype, kv_dtype, F32, MXU_DTYPE, kv_dtype, kv_dtype, MXU_DTYPE)
    return pl.pallas_call(
        functools.partial(_inproj_kernel, rope=rope),
        grid=(t // tm,),
        in_specs=in_specs,
        out_specs=[pl.BlockSpec((tm, w), row) for w in widths],
        out_shape=[jax.ShapeDtypeStruct((t, w), dt) for w, dt in zip(widths, dtypes)],
        compiler_params=_params(("parallel",)),
        name="in_proj_rope" if rope else "in_proj",
    )(*args)


def _lane_masks(dtype):
    lane = lax.broadcasted_iota(jnp.int32, (1, LANE), 1)
    lo = lane < HEAD_DIM
    return lo, lo.astype(dtype), (~lo).astype(dtype)


_KEY_BLOCK = 256
_Q_SUB = 128


def _softmax_pv(qs, key_blocks, s_ref):
    macc = None
    for bi, (score_fn, _) in enumerate(key_blocks):
        sj = score_fn(qs)
        s_ref[bi] = sj
        macc = sj if macc is None else jnp.maximum(macc, sj)
    mb = jnp.broadcast_to(jnp.max(macc, axis=-1, keepdims=True), macc.shape)
    lacc = jnp.zeros(macc.shape, F32)
    o = jnp.zeros((qs.shape[0], LANE), F32)
    for bi, (_, v_fn) in enumerate(key_blocks):
        p = jnp.exp(s_ref[bi] - mb)
        lacc = lacc + p
        vb = v_fn()
        o = o + _mm(p[:, :vb.shape[0]], vb)
    return o * (1.0 / jnp.sum(lacc, axis=-1, keepdims=True))


def _attn_kernel(*refs, kv_tiles, cached):
    if cached:
        q_ref, k_ref, v_ref, kc_ref, vc_ref, o_ref, s_ref = refs
    else:
        q_ref, k_ref, v_ref, o_ref, s_ref = refs
    tq = q_ref.shape[0]
    kb = _KEY_BLOCK
    lo, m_lo, m_hi = _lane_masks(MXU_DTYPE)
    pairs_per_kv = (N_HEADS // 2) // kv_tiles
    sources = [(k_ref, v_ref)] + ([(kc_ref, vc_ref)] if cached else [])
    for q0 in range(0, tq, _Q_SUB):
        for hp in range(N_HEADS // 2):
            ksl = slice((hp // pairs_per_kv) * LANE, (hp // pairs_per_kv + 1) * LANE)
            q2 = q_ref[q0:q0 + _Q_SUB, hp * LANE:(hp + 1) * LANE]
            qs = jnp.concatenate([q2 * m_lo, q2 * m_hi], axis=0)
            blocks = [(functools.partial(lambda x, kr, off, ksl: _mm_nt(x, kr[off:off + kb, ksl]),
                                         kr=kr, off=off, ksl=ksl),
                       functools.partial(lambda vr, off, ksl: vr[off:off + kb, ksl], vr=vr, off=off, ksl=ksl))
                      for kr, vr in sources for off in range(0, kr.shape[0], kb)]
            o = _softmax_pv(qs, blocks, s_ref)
            o_ref[q0:q0 + _Q_SUB, hp * LANE:(hp + 1) * LANE] = jnp.where(
                lo, o[:_Q_SUB], o[_Q_SUB:]).astype(o_ref.dtype)


def _attention(q, k, v, kc, vc, tq, name):
    b, lq, _ = q.shape
    lk, kw = k.shape[1], k.shape[2]
    cached = kc is not None
    qmap = lambda bi, ti: (bi, ti, 0)
    kmap = lambda bi, ti: (bi, 0, 0)
    in_specs = [
        pl.BlockSpec((None, tq, 512), qmap),
        pl.BlockSpec((None, lk, kw), kmap),
        pl.BlockSpec((None, lk, kw), kmap),
    ]
    args = [q, k, v]
    if cached:
        lc = kc.shape[1]
        in_specs += [pl.BlockSpec((None, lc, kw), kmap)] * 2
        args += [kc, vc]
    assert lk % _KEY_BLOCK == 0 and (not cached or kc.shape[1] % _KEY_BLOCK == 0) and tq % _Q_SUB == 0
    n_blocks = (lk + (kc.shape[1] if cached else 0)) // _KEY_BLOCK
    return pl.pallas_call(
        functools.partial(_attn_kernel, kv_tiles=kw // LANE, cached=cached),
        grid=(b, lq // tq),
        in_specs=in_specs,
        out_specs=pl.BlockSpec((None, tq, 512), qmap),
        out_shape=jax.ShapeDtypeStruct((b, lq, 512), MXU_DTYPE),
        scratch_shapes=[pltpu.VMEM((n_blocks, 2 * _Q_SUB, _KEY_BLOCK), F32)],
        compiler_params=_params(("parallel", "parallel")),
        name=name,
    )(*args)


def _na_geometry(seq_len):
    rows = seq_len // GRID_W
    n_tiles = rows // 2
    assert rows >= NA_KEY_ROWS and NA_WIN_ROWS <= rows and NA_KEY_ROWS % 2 == 0
    ws = np.clip(2 * np.arange(n_tiles) - NA_WIN_ROWS // 2, 0, rows - NA_KEY_ROWS)
    r = 2 * np.arange(n_tiles)[:, None, None] + np.arange(2)[None, :, None]
    key_r = ws[:, None, None] + np.arange(NA_KEY_ROWS)[None, None, :]
    r0 = np.clip(r - NA_WIN_ROWS // 2, 0, rows - NA_WIN_ROWS)
    valid = (key_r >= r0) & (key_r < r0 + NA_WIN_ROWS)
    dr = np.where(valid, key_r - r + NA_WIN_ROWS - 1, 2 * NA_WIN_ROWS - 1)
    assert (valid.sum(-1) == NA_WIN_ROWS).all()
    return ws.astype(np.int32), dr.reshape(-1).astype(np.int32)


def _na_bias_blocks(rpb):
    h = rpb.shape[0]
    nrel = 2 * NA_WIN_ROWS - 1
    zeros = jnp.zeros((h, nrel, LANE - (2 * NA_WIN_COLS - 1)), F32)
    v = jnp.concatenate([rpb[..., NA_WIN_COLS - 1:], zeros, rpb[..., :NA_WIN_COLS - 1]], axis=-1).astype(F32)
    t = jnp.tile(v, (1, 1, GRID_W))[..., :GRID_W * (LANE - 1)].reshape(h, nrel, GRID_W, LANE - 1)[..., :GRID_W]
    c = np.arange(GRID_W)
    c0 = np.clip(c - NA_WIN_COLS // 2, 0, GRID_W - NA_WIN_COLS)
    colmask = (c[None, :] >= c0[:, None]) & (c[None, :] < c0[:, None] + NA_WIN_COLS)
    t = jnp.where(jnp.asarray(colmask)[None, None], t, NEG_BIG)
    t = jnp.concatenate([t, jnp.full((h, 1, GRID_W, GRID_W), NEG_BIG, F32)], axis=1)
    pad = jnp.zeros_like(t)
    return jnp.concatenate([t, pad], axis=-1), jnp.concatenate([pad, t], axis=-1)


def _na_kernel(ws_ref, dr_ref, q_ref, k_ref, v_ref, kc_ref, vc_ref, bl_ref, br_ref, o_ref, s_ref):
    i = pl.program_id(1)
    start = pl.multiple_of(ws_ref[i] * GRID_W, GRID_W)
    nk = NA_KEY_ROWS * GRID_W
    kb = _KEY_BLOCK
    tq = q_ref.shape[0]
    lc = kc_ref.shape[0]
    lo, m_lo, m_hi = _lane_masks(MXU_DTYPE)

    def bias_block(hp, off, width):
        rows = []
        for h in (2 * hp, 2 * hp + 1):
            for qr in range(2):
                base = (i * 2 + qr) * NA_KEY_ROWS + off // GRID_W
                tiles = [bl_ref[h, dr_ref[base + 2 * kp]] + br_ref[h, dr_ref[base + 2 * kp + 1]]
                         for kp in range(width // LANE)]
                rows.append(tiles[0] if len(tiles) == 1 else jnp.concatenate(tiles, axis=1))
        return jnp.concatenate(rows, axis=0)

    def local_scores(x, hp, sl, off, width):
        s = _mm_nt(x, k_ref[pl.ds(start + off, width), sl]) + bias_block(hp, off, width)
        if width < kb:
            s = jnp.concatenate([s, jnp.full((s.shape[0], kb - width), NEG_BIG, F32)], axis=1)
        return s

    for hp in range(N_HEADS // 2):
        sl = slice(hp * LANE, (hp + 1) * LANE)
        q2 = q_ref[:, sl]
        qs = jnp.concatenate([q2 * m_lo, q2 * m_hi], axis=0)
        blocks = []
        for off in range(0, nk, kb):
            width = min(kb, nk - off)
            blocks.append((functools.partial(local_scores, hp=hp, sl=sl, off=off, width=width),
                           functools.partial(lambda sl, off, width: v_ref[pl.ds(start + off, width), sl],
                                             sl=sl, off=off, width=width)))
        for off in range(0, lc, kb):
            blocks.append((functools.partial(lambda x, sl, off: _mm_nt(x, kc_ref[off:off + kb, sl]), sl=sl, off=off),
                           functools.partial(lambda sl, off: vc_ref[off:off + kb, sl], sl=sl, off=off)))
        o = _softmax_pv(qs, blocks, s_ref)
        o_ref[:, sl] = jnp.where(lo, o[:tq], o[tq:]).astype(o_ref.dtype)


def _neighbourhood_attention(q, k, v, kc, vc, rpb):
    b, seq_len, _ = q.shape
    lc = kc.shape[1]
    ws, dr = _na_geometry(seq_len)
    b_left, b_right = _na_bias_blocks(rpb)
    n_tiles = len(ws)
    tq = 2 * GRID_W
    qmap = lambda bi, ti, ws_r, dr_r: (bi, ti, 0)
    kmap = lambda bi, ti, ws_r, dr_r: (bi, 0, 0)
    bmap = lambda bi, ti, ws_r, dr_r: (0, 0, 0, 0)
    grid_spec = pltpu.PrefetchScalarGridSpec(
        num_scalar_prefetch=2,
        grid=(b, n_tiles),
        in_specs=[
            pl.BlockSpec((None, tq, 512), qmap),
            pl.BlockSpec((None, seq_len, 512), kmap),
            pl.BlockSpec((None, seq_len, 512), kmap),
            pl.BlockSpec((None, lc, 512), kmap),
            pl.BlockSpec((None, lc, 512), kmap),
            pl.BlockSpec(b_left.shape, bmap),
            pl.BlockSpec(b_right.shape, bmap),
        ],
        out_specs=pl.BlockSpec((None, tq, 512), qmap),
        scratch_shapes=[pltpu.VMEM((-(-NA_KEY_ROWS * GRID_W // _KEY_BLOCK) + lc // _KEY_BLOCK, 2 * tq, _KEY_BLOCK),
                                   F32)],
    )
    assert lc % _KEY_BLOCK == 0
    return pl.pallas_call(
        _na_kernel,
        grid_spec=grid_spec,
        out_shape=jax.ShapeDtypeStruct((b, seq_len, 512), MXU_DTYPE),
        compiler_params=_params(("parallel", "arbitrary")),
        name="na_attn",
    )(jnp.asarray(ws), jnp.asarray(dr), q, k, v, kc, vc, b_left, b_right)


def _cmul(ar, ai, br, bi):
    return ar * br - ai * bi, ar * bi + ai * br


def _lam_bar(lr, li, ls):
    dt = jnp.exp(ls)
    mag = jnp.exp(lr * dt)
    return mag * jnp.cos(li * dt), mag * jnp.sin(li * dt)


def _zoh_coef(lr, li, zr, zi):
    nr, ni = zr - 1.0, zi
    den = 1.0 / (lr * lr + li * li)
    return (nr * lr + ni * li) * den, (ni * lr - nr * li) * den


def _squarings(zr, zi, n):
    out = [(zr, zi)]
    for _ in range(n - 1):
        zr, zi = _cmul(zr, zi, zr, zi)
        out.append((zr, zi))
    return out


def _cpow(squares, e):
    pr, pi = jnp.ones(e.shape, F32), jnp.zeros(e.shape, F32)
    for k, (zr, zi) in enumerate(squares):
        bit = jnp.bitwise_and(jnp.right_shift(e, k), 1) == 1
        nr, ni = _cmul(pr, pi, zr, zi)
        pr, pi = jnp.where(bit, nr, pr), jnp.where(bit, ni, pi)
    return pr, pi


def _ssm_ops_kernel(lrr_ref, lir_ref, lsr_ref, lrc_ref, lic_ref, lsc_ref, bt_ref, ct_ref, s0_ref, s1_ref,
                    wt_o, wb_o, wc_o, l16_o, tg_ref):
    tc, hg, p = SSM_CHUNK, SSM_GROUP_CH, SSM_STATE
    w = tc * hg
    lane_w = lax.broadcasted_iota(jnp.int32, (1, w), 1)
    lane_p = lax.broadcasted_iota(jnp.int32, (1, 2 * p), 1)
    row_w = lax.broadcasted_iota(jnp.int32, (2 * w, 1), 0)
    row_p = lax.broadcasted_iota(jnp.int32, (2 * p, 1), 0)
    tau_of_lane = jnp.right_shift(lane_w, 4)
    gl_of_lane = jnp.right_shift(lane_p, 6)
    step_of_row = jnp.right_shift(row_w, 5)
    same_group = jnp.bitwise_and(jnp.right_shift(row_w, 4), 1) == gl_of_lane
    first_rows = row_p < p
    tg_ref[...] = jnp.zeros_like(tg_ref)
    l16_rows = []
    for d in range(2):
        lr, li = lrr_ref[d], lir_ref[d]
        zr, zi = _lam_bar(lr, li, lsr_ref[d])
        cfr, cfi = _zoh_coef(lr, li, zr, zi)
        btr, bti = _cmul(cfr, cfi, bt_ref[d, 0], bt_ref[d, 1])
        sq = _squarings(zr, zi, 5)
        e_inj = (tc - 1 - step_of_row) if d == 0 else step_of_row
        pr, pi = _cpow(sq[:4], jnp.broadcast_to(e_inj, (2 * w, 2 * p)))
        ir, ii = _cmul(pr, pi, jnp.tile(btr, (2 * tc, 1)), jnp.tile(bti, (2 * tc, 1)))
        wb_o[:, (2 * d) * LANE:(2 * d + 1) * LANE] = jnp.where(same_group, ir, 0.0).astype(wb_o.dtype)
        wb_o[:, (2 * d + 1) * LANE:(2 * d + 2) * LANE] = jnp.where(same_group, ii, 0.0).astype(wb_o.dtype)
        l16_rows += [sq[4][0], sq[4][1]]
        lrc, lic = lrc_ref[d], lic_ref[d]
        zcr, zci = _lam_bar(lrc, lic, lsc_ref[d])
        tau = tau_of_lane if d == 0 else (tc - 1) - tau_of_lane
        pr, pi = _cpow(_squarings(zcr, zci, 4), jnp.broadcast_to(tau, (2 * p, w)))
        c0r, c0i = _cmul(ct_ref[d, 0], ct_ref[d, 1], pr, pi)
        c1r, c1i = _cmul(c0r, c0i, zcr, zci)
        for r, val in ((2 * d, c1r), (2 * d + 1, -c1i)):
            vb = val.astype(MXU_DTYPE)
            spread = jnp.where(first_rows, jnp.dot(vb, s0_ref[...], preferred_element_type=F32),
                               jnp.dot(vb, s1_ref[...], preferred_element_type=F32))
            wc_o[r * LANE:(r + 1) * LANE, :] = spread.astype(wc_o.dtype)
        for gl in range(2):
            in_group = gl_of_lane == gl
            kt = (jnp.dot(jnp.where(in_group, btr, 0.0), c0r, preferred_element_type=F32,
                          precision=lax.Precision.HIGHEST)
                  - jnp.dot(jnp.where(in_group, bti, 0.0), c0i, preferred_element_type=F32,
                            precision=lax.Precision.HIGHEST))
            for s in range(tc):
                if d == 0:
                    shift, keep = hg * s, lane_w >= hg * s
                else:
                    shift, keep = (w - hg * (tc - 1 - s)) % w, lane_w < hg * (s + 1)
                rolled = kt if shift == 0 else pltpu.roll(kt, shift, 1)
                tg_ref[gl, s * hg:(s + 1) * hg, :] += jnp.where(keep, rolled, 0.0)
    for gl, s_ref in enumerate((s0_ref, s1_ref)):
        spread = jnp.dot(tg_ref[gl].astype(MXU_DTYPE), s_ref[...], preferred_element_type=F32)
        for s in range(tc):
            wt_o[(2 * s + gl) * hg:(2 * s + gl + 1) * hg, :] = spread[s * hg:(s + 1) * hg].astype(wt_o.dtype)
    l16_o[...] = jnp.concatenate(l16_rows + [jnp.zeros((4, 2 * p), F32)], axis=0)


def _ssm_operators(lam_re, lam_im, log_step, b_re, b_im, c_re, c_im):
    depth = lam_re.shape[0]
    p, hg, tc, q = SSM_STATE, SSM_GROUP_CH, SSM_CHUNK, SSM_PAIRS
    assert (hg, p, tc) == (16, 64, 16), "lane/row index arithmetic in the kernel uses these as shifts"
    w = tc * hg

    def per_pair(a, tail):
        a = a.astype(F32).reshape((depth, 2, q, 2) + tail)
        return jnp.transpose(a, (0, 2, 1, 3) + tuple(range(4, 4 + len(tail))))

    lam_r, lam_i = per_pair(lam_re, (p,)), per_pair(lam_im, (p,))
    ls = jnp.broadcast_to(per_pair(log_step, ())[..., None], lam_r.shape)
    rows = [a.reshape(depth, q, 2, 1, 2 * p) for a in (lam_r, lam_i, ls)]
    cols = [a.reshape(depth, q, 2, 2 * p, 1) for a in (lam_r, lam_i, ls)]
    bt = jnp.stack([per_pair(b_re, (p, hg)), per_pair(b_im, (p, hg))], axis=3)
    bt = jnp.transpose(bt, (0, 1, 2, 3, 6, 4, 5)).reshape(depth, q, 2, 2, hg, 2 * p)
    ct = jnp.stack([per_pair(c_re, (hg, p)), per_pair(c_im, (hg, p))], axis=3)
    ct = jnp.transpose(ct, (0, 1, 2, 3, 4, 6, 5)).reshape(depth, q, 2, 2, 2 * p, hg)
    ct = jnp.tile(ct, (1, 1, 1, 1, 1, tc))
    r, c = np.arange(w)[:, None], np.arange(2 * w)[None, :]
    hit = (r // hg == c // (2 * hg)) & (r % hg == c % hg)
    spread = [jnp.asarray(hit & ((c // hg) % 2 == gl), MXU_DTYPE) for gl in range(2)]

    blk = lambda a: pl.BlockSpec((None, None) + a.shape[2:], lambda l, i: (l, i) + (0,) * (a.ndim - 2))
    const = pl.BlockSpec((w, 2 * w), lambda l, i: (0, 0))
    mat = pl.BlockSpec((None, None, 2 * w, 2 * w), lambda l, i: (l, i, 0, 0))
    args = rows + cols + [bt, ct]
    return pl.pallas_call(
        _ssm_ops_kernel,
        grid=(depth, q),
        in_specs=[blk(a) for a in args] + [const, const],
        out_specs=[mat, mat, mat, pl.BlockSpec((None, None, 8, 2 * p), lambda l, i: (l, i, 0, 0))],
        out_shape=[jax.ShapeDtypeStruct((depth, q, 2 * w, 2 * w), MXU_DTYPE)] * 3
        + [jax.ShapeDtypeStruct((depth, q, 8, 2 * p), F32)],
        scratch_shapes=[pltpu.VMEM((2, w, w), F32)],
        compiler_params=_params(("parallel", "parallel")),
        name="ssm_ops",
    )(*args, *spread)


_PAIRS_PER_TILE = 4
_SSM_ROW_BLOCK = 64


def _ssm_kernel(zu_ref, wt_ref, wb_ref, wc_ref, l16_ref, h0_ref, y_o, fin_o, u_ref, yp_ref, *state_refs,
                batch, seq_len):
    tc, npair = SSM_CHUNK, _PAIRS_PER_TILE
    n_chunks = seq_len // tc
    nrows = batch * n_chunks
    rb = _SSM_ROW_BLOCK
    slot_w = LANE // npair
    slot = jnp.right_shift(lax.broadcasted_iota(jnp.int32, (1, LANE), 1), 5)
    dx_refs, xs_refs = state_refs[:4], state_refs[4:]

    def place(pieces, src_slot):
        offset = src_slot
        out = None
        for j, piece in enumerate(pieces):
            shift = (slot_w * (j - offset[j])) % LANE
            r = piece if shift == 0 else pltpu.roll(piece, shift, 1)
            out = r if out is None else jnp.where(slot == j, r, out)
        return out

    def gather_block(i, carry):
        r0 = pl.multiple_of(i * rb, rb)
        steps = [zu_ref[pl.ds(r0 * tc + s, rb, stride=tc), :] for s in range(tc)]
        for p in range(npair):
            tiles = [place(steps[4 * k:4 * k + 4], [p] * 4) for k in range(tc // 4)]
            u_ref[p, pl.ds(r0, rb), :] = jnp.concatenate(tiles, axis=1).astype(u_ref.dtype)
        return carry

    lax.fori_loop(0, nrows // rb, gather_block, 0)

    for p in range(npair):
        u = u_ref[p]
        y_intra = jnp.dot(u, wt_ref[p], preferred_element_type=F32)
        dx = jnp.dot(u, wb_ref[p], preferred_element_type=F32)
        for r in range(4):
            dx_refs[r][...] = dx[:, r * LANE:(r + 1) * LANE]
        lfr, lfi, lbr, lbi = (l16_ref[p, r:r + 1, :] for r in range(4))

        def body(c, carry):
            fr, fi, br, bi = carry
            fwd = pl.ds(c, batch, stride=n_chunks)
            bwd = pl.ds(n_chunks - 1 - c, batch, stride=n_chunks)
            xs_refs[0][fwd, :] = fr
            xs_refs[1][fwd, :] = fi
            xs_refs[2][bwd, :] = br
            xs_refs[3][bwd, :] = bi
            nfr = lfr * fr - lfi * fi + dx_refs[0][fwd, :]
            nfi = lfr * fi + lfi * fr + dx_refs[1][fwd, :]
            nbr = lbr * br - lbi * bi + dx_refs[2][bwd, :]
            nbi = lbr * bi + lbi * br + dx_refs[3][bwd, :]
            return nfr, nfi, nbr, nbi

        fin = lax.fori_loop(0, n_chunks, body, tuple(h0_ref[p, :, r * LANE:(r + 1) * LANE] for r in range(4)),
                            unroll=4)
        for r in range(4):
            fin_o[p, :, r * LANE:(r + 1) * LANE] = fin[r]
        xs = jnp.concatenate([x[...] for x in xs_refs], axis=1).astype(MXU_DTYPE)
        yp_ref[p] = y_intra + jnp.dot(xs, wc_ref[p], preferred_element_type=F32)

    def scatter_block(i, carry):
        r0 = pl.multiple_of(i * rb, rb)
        for k in range(tc // 4):
            pieces = [yp_ref[p, pl.ds(r0, rb), k * LANE:(k + 1) * LANE] for p in range(npair)]
            for j in range(4):
                y_o[pl.ds(r0 * tc + 4 * k + j, rb, stride=tc), :] = place(pieces, [j] * npair)
        return carry

    lax.fori_loop(0, nrows // rb, scatter_block, 0)


def _ssm_scan(zu2d, w_t, w_b, w_c, l16, h0, batch, seq_len):
    t = zu2d.shape[0]
    nrows = t // SSM_CHUNK
    npair = _PAIRS_PER_TILE
    assert nrows % _SSM_ROW_BLOCK == 0 and seq_len % SSM_CHUNK == 0
    once = pl.Buffered(1)
    wspec = pl.BlockSpec((npair, 512, 512), lambda i: (i, 0, 0))
    return pl.pallas_call(
        functools.partial(_ssm_kernel, batch=batch, seq_len=seq_len),
        grid=(SSM_PAIRS // npair,),
        in_specs=[
            pl.BlockSpec((t, LANE), lambda i: (0, i), pipeline_mode=once),
            wspec, wspec, wspec,
            pl.BlockSpec((npair, 8, LANE), lambda i: (i, 0, 0)),
            pl.BlockSpec((npair, batch, 512), lambda i: (i, 0, 0)),
        ],
        out_specs=[
            pl.BlockSpec((t, LANE), lambda i: (0, i), pipeline_mode=once),
            pl.BlockSpec((npair, batch, 512), lambda i: (i, 0, 0)),
        ],
        out_shape=[jax.ShapeDtypeStruct((t, SSM_WIDTH), F32),
                   jax.ShapeDtypeStruct((SSM_PAIRS, batch, 512), F32)],
        scratch_shapes=[pltpu.VMEM((npair, nrows, 512), MXU_DTYPE), pltpu.VMEM((npair, nrows, 512), F32)]
        + [pltpu.VMEM((nrows, LANE), F32)] * 8,
        compiler_params=_params(("arbitrary",)),
        name="ssm_scan",
    )(zu2d, w_t, w_b, w_c, l16, h0)


def _pack_state(s_re, s_im):
    b = s_re.shape[0]
    a = jnp.stack([s_re, s_im], axis=2).reshape(b, 2, 2, SSM_PAIRS, 2 * SSM_STATE)
    return jnp.transpose(a, (3, 0, 1, 2, 4)).reshape(SSM_PAIRS, b, 4 * 2 * SSM_STATE).astype(F32)


def _unpack_state(fin):
    b = fin.shape[1]
    a = jnp.transpose(fin.reshape(SSM_PAIRS, b, 2, 2, 2 * SSM_STATE), (1, 2, 3, 0, 4))
    a = a.reshape(b, 2, 2, SSM_GROUPS, SSM_STATE)
    return a[:, :, 0], a[:, :, 1]


def _merge_kernel(ya_ref, ys_ref, u_ref, yc_ref, g_ref, x_ref, mod_ref, ng_ref, d_ref,
                  wglu_ref, wa_ref, wb_ref, wc_ref, wo_ref, x_o, h_o):
    y = ys_ref[...].astype(F32) + d_ref[...] * u_ref[...]
    gl = _gelu_tanh(y)
    yb = gl * _sigmoid(_mm(gl, wglu_ref[...]))
    merged = (g_ref[:, 0:D_MODEL].astype(F32) * _mm(ya_ref[...], wa_ref[...])
              + g_ref[:, D_MODEL:2 * D_MODEL].astype(F32) * _mm(yb, wb_ref[...])
              + g_ref[:, 2 * D_MODEL:3 * D_MODEL].astype(F32) * _mm(yc_ref[...], wc_ref[...]))
    x1 = x_ref[...] + mod_ref[2:3, :] * _rms(_mm(merged, wo_ref[...]), ng_ref[1:2, :])
    x_o[...] = x1
    h_o[...] = (_rms(x1, ng_ref[2:3, :]) * (1.0 + mod_ref[4:5, :]) + mod_ref[3:4, :]).astype(h_o.dtype)


def _merge(ya, ys, zu, yc, gates, x2d, mod_l, mod_row, ng, ssm_d, w_glu, w_a, w_b, w_c, w_o):
    t = x2d.shape[0]
    tm = TOKEN_TILE
    row = lambda i: (i, 0)
    const = lambda i: (0, 0)
    r512 = pl.BlockSpec((tm, 512), row)
    wbr = pl.BlockSpec((512, D_MODEL), const)
    return pl.pallas_call(
        _merge_kernel,
        grid=(t // tm,),
        in_specs=[
            r512, r512, r512, r512,
            pl.BlockSpec((tm, 3 * D_MODEL), row),
            pl.BlockSpec((tm, D_MODEL), row),
            pl.BlockSpec((None, 6, D_MODEL), lambda i: (mod_row(i), 0, 0)),
            pl.BlockSpec((4, D_MODEL), const),
            pl.BlockSpec((1, 512), const),
            pl.BlockSpec((512, 512), const),
            wbr, wbr, wbr,
            pl.BlockSpec((D_MODEL, D_MODEL), const),
        ],
        out_specs=[pl.BlockSpec((tm, D_MODEL), row), pl.BlockSpec((tm, D_MODEL), row)],
        out_shape=[jax.ShapeDtypeStruct((t, D_MODEL), F32), jax.ShapeDtypeStruct((t, D_MODEL), MXU_DTYPE)],
        compiler_params=_params(("parallel",)),
        name="merge",
    )(ya, ys, zu, yc, gates, x2d, mod_l, ng, ssm_d, w_glu, w_a, w_b, w_c, w_o)


def _ffn_kernel(*refs, seq_len, first, last):
    refs = list(refs)
    h_ref, hp_ref, hn_ref = refs[:3]
    del refs[:3]
    part_ref = None if first else refs.pop(0)
    if last:
        x_ref, mod_ref, ng_ref = refs[:3]
        del refs[:3]
    wa_ref, wg_ref, cwa_ref, cwg_ref, cba_ref, cbg_ref, wd_ref, out_ref, ua_ref, ug_ref, act_ref = refs
    i = pl.program_id(0)
    tm = h_ref.shape[0]
    ft = wd_ref.shape[0]
    n = tm + 16
    hh = jnp.concatenate([hp_ref[...], h_ref[...], hn_ref[...]], axis=0)
    ua_ref[...] = jnp.dot(hh, wa_ref[...], preferred_element_type=F32)
    ug_ref[...] = jnp.dot(hh, wg_ref[...], preferred_element_type=F32)
    pos = jnp.bitwise_and(i * tm + lax.broadcasted_iota(jnp.int32, (tm, 1), 0), seq_len - 1)
    has_prev = (pos != 0).astype(F32)
    has_next = (pos != seq_len - 1).astype(F32)

    def conv(u_ref, cw_ref, cb_ref, lo):
        uc = u_ref[:, lo:lo + LANE]
        up = pltpu.roll(uc, 1, 0)[8:8 + tm] * has_prev
        un = pltpu.roll(uc, n - 1, 0)[8:8 + tm] * has_next
        return (cw_ref[0:1, lo:lo + LANE] * up + cw_ref[1:2, lo:lo + LANE] * uc[8:8 + tm]
                + cw_ref[2:3, lo:lo + LANE] * un + cb_ref[0:1, lo:lo + LANE])

    for kc in range(ft // LANE):
        a = conv(ua_ref, cwa_ref, cba_ref, kc * LANE)
        g = conv(ug_ref, cwg_ref, cbg_ref, kc * LANE)
        act_ref[:, kc * LANE:(kc + 1) * LANE] = (g * _sigmoid(g) * a).astype(act_ref.dtype)
    total = jnp.dot(act_ref[...], wd_ref[...], preferred_element_type=F32)
    if not first:
        total = total + part_ref[...]
    if last:
        out_ref[...] = x_ref[...] + mod_ref[5:6, :] * _rms(total, ng_ref[3:4, :])
    else:
        out_ref[...] = total


def _conv_ffn(h2, x1, mod_l, mod_row, ng, w_up, conv_w, conv_b, w_down, seq_len):
    t = x1.shape[0]
    tm = TOKEN_TILE
    ft = FF_TILE
    nf = D_FF // ft
    nblk8 = t // 8
    assert seq_len & (seq_len - 1) == 0 and (seq_len % tm == 0 or tm % seq_len == 0)
    row = pl.BlockSpec((tm, D_MODEL), lambda i: (i, 0))
    part = None
    for j in range(nf):
        first, last = j == 0, j == nf - 1
        in_specs = [
            row,
            pl.BlockSpec((8, D_MODEL), lambda i: (jnp.maximum(i * (tm // 8) - 1, 0), 0)),
            pl.BlockSpec((8, D_MODEL), lambda i: (jnp.minimum((i + 1) * (tm // 8), nblk8 - 1), 0)),
        ]
        args = [h2, h2, h2]
        if not first:
            in_specs.append(row)
            args.append(part)
        if last:
            in_specs += [row, pl.BlockSpec((None, 6, D_MODEL), lambda i: (mod_row(i), 0, 0)),
                         pl.BlockSpec((4, D_MODEL), lambda i: (0, 0))]
            args += [x1, mod_l, ng]
        in_specs += [
            pl.BlockSpec((D_MODEL, ft), lambda i, j=j: (0, j)),
            pl.BlockSpec((D_MODEL, ft), lambda i, j=j: (0, nf + j)),
            pl.BlockSpec((3, ft), lambda i, j=j: (0, j)),
            pl.BlockSpec((3, ft), lambda i, j=j: (0, nf + j)),
            pl.BlockSpec((1, ft), lambda i, j=j: (0, j)),
            pl.BlockSpec((1, ft), lambda i, j=j: (0, nf + j)),
            pl.BlockSpec((ft, D_MODEL), lambda i, j=j: (j, 0)),
        ]
        args += [w_up, w_up, conv_w, conv_w, conv_b, conv_b, w_down]
        part = pl.pallas_call(
            functools.partial(_ffn_kernel, seq_len=seq_len, first=first, last=last),
            grid=(t // tm,),
            in_specs=in_specs,
            out_specs=row,
            out_shape=jax.ShapeDtypeStruct((t, D_MODEL), F32),
            scratch_shapes=[pltpu.VMEM((tm + 16, ft), F32), pltpu.VMEM((tm + 16, ft), F32),
                            pltpu.VMEM((tm, ft), MXU_DTYPE)],
            compiler_params=_params(("parallel",)),
            name="conv_ffn_last" if last else "conv_ffn_part",
        )(*args)
    return part


_Q_HEAD_ORDER = (0, 4, 1, 5, 2, 6, 3, 7)


def _rope_tables(seq_len):
    nf = HEAD_DIM // 4
    t = np.arange(seq_len)
    pos = np.stack([t // GRID_W, t % GRID_W]).astype(np.float32)
    inv = jnp.asarray(ROPE_THETA, F32) ** (-jnp.arange(nf, dtype=F32) / nf)
    ang = jnp.asarray(pos)[:, :, None] * inv
    d = np.arange(HEAD_DIM)
    ang = ang[d // (2 * nf), :, d % nf].T
    second = jnp.asarray(((d % (2 * nf)) // nf) == 1)[None, :]
    cos, sin = jnp.cos(ang), jnp.sin(ang)
    tabs = (cos, jnp.where(second, 0.0, -sin), jnp.where(second, sin, 0.0))
    return tuple(jnp.tile(x, (1, LANE // HEAD_DIM)).astype(F32) for x in tabs)


def _layer_weights(w_in, qk_g, w_br_a):
    hd = HEAD_DIM
    w_in_p = jnp.concatenate([w_in[:, h * hd:(h + 1) * hd] for h in _Q_HEAD_ORDER] + [w_in[:, 512:]],
                             axis=1).astype(MXU_DTYPE)
    w_a_p = jnp.concatenate([w_br_a[h * hd:(h + 1) * hd] for h in _Q_HEAD_ORDER], axis=0).astype(MXU_DTYPE)
    qg = jnp.tile(qk_g[0], N_HEADS).reshape(1, 512).astype(F32)
    kg = jnp.tile(qk_g[1], GA_KV_HEADS).reshape(1, LANE).astype(F32)
    return w_in_p, w_a_p, qg, kg


def kernel(x_prompt, x_sample, c, cache_ga_k, cache_ga_v, cache_na_k, cache_na_v, state_ssm_re, state_ssm_im,
           c_ctx, w_mod, b_mod, norm_g, w_in, qk_norm_g, na_rpb, ssm_lam_re, ssm_lam_im, ssm_log_step,
           ssm_b_re, ssm_b_im, ssm_c_re, ssm_c_im, ssm_d, w_glu, w_br_a, w_br_b, w_br_c, w_out,
           w_up, conv_w, conv_b, w_down):
    depth = w_in.shape[0]
    bp, lp, _ = x_prompt.shape
    bs, ls, _ = x_sample.shape
    lc = cache_ga_k.shape[2]
    assert lp % 256 == 0 and ls % TOKEN_TILE == 0 and (bp * lp) % TOKEN_TILE == 0
    assert bs % 8 == 0 and bp % 8 == 0, "the scan keeps one batch row per sublane"

    rows = 1 + bs
    rows_p = -(-rows // 8) * 8
    cvec = jnp.concatenate([c_ctx[None], c, jnp.zeros((rows_p - rows, D_MODEL), F32)], axis=0)
    mod = _modulation(cvec, w_mod, b_mod).reshape(depth, rows_p, 6, D_MODEL)

    w_t, w_b, w_c, l16 = _ssm_operators(ssm_lam_re, ssm_lam_im, ssm_log_step, ssm_b_re, ssm_b_im,
                                        ssm_c_re, ssm_c_im)
    seg = jnp.asarray(np.kron(np.eye(N_HEADS), np.full((HEAD_DIM, HEAD_DIM), 1.0 / HEAD_DIM)), MXU_DTYPE)
    rope_tabs = _rope_tables(ls)
    tiles_per_sample = ls // TOKEN_TILE
    ctx_row = lambda i: 0
    lat_row = lambda i: 1 + i // tiles_per_sample

    y_p = x_prompt.reshape(bp * lp, D_MODEL)
    y_s = x_sample.reshape(bs * ls, D_MODEL)
    zero_state = jnp.zeros((SSM_PAIRS, bp, 512), F32)
    outs = [[] for _ in range(6)]
    for l in range(depth):
        w_in_p, w_a_p, qg, kg = _layer_weights(w_in[l], qk_norm_g[l], w_br_a[l])
        w_glu_l, w_b_l, w_c_l, w_o_l = (a[l].astype(MXU_DTYPE) for a in (w_glu, w_br_b, w_br_c, w_out))
        ffn_w = (w_up[l].astype(MXU_DTYPE), conv_w[l].astype(F32), conv_b[l].reshape(1, 2 * D_FF).astype(F32),
                 w_down[l].astype(MXU_DTYPE))
        d_l = ssm_d[l].reshape(1, SSM_WIDTH).astype(F32)
        ng = norm_g[l].astype(F32)
        ssm_ops = (w_t[l], w_b[l], w_c[l], l16[l])

        q, k, v, zu, nq, nk, nv, gates = _in_projection(
            y_p, mod[l], ctx_row, ng, w_in_p, qg, kg, seg, None, lp, F32)
        r3 = lambda a: a.reshape(bp, lp, a.shape[-1])
        ya = _attention(r3(q), r3(k), r3(v), None, None, lp, "ga_ctx")
        yc = _attention(r3(nq), r3(nk), r3(nv), None, None, lp, "na_ctx")
        ys, fin = _ssm_scan(zu, *ssm_ops, zero_state, bp, lp)
        x1, h2 = _merge(ya.reshape(-1, 512), ys, zu, yc.reshape(-1, 512), gates, y_p,
                        mod[l], ctx_row, ng, d_l, w_glu_l, w_a_p, w_b_l, w_c_l, w_o_l)
        y_p = _conv_ffn(h2, x1, mod[l], ctx_row, ng, *ffn_w, lp)
        f_re, f_im = _unpack_state(fin)
        for lst, a in zip(outs, (k.reshape(bp, lp, GA_KV_HEADS, HEAD_DIM), v.reshape(bp, lp, GA_KV_HEADS, HEAD_DIM),
                                 nk.reshape(bp, lp, N_HEADS, HEAD_DIM), nv.reshape(bp, lp, N_HEADS, HEAD_DIM),
                                 f_re, f_im)):
            lst.append(a)

        q, k, v, zu, nq, nk, nv, gates = _in_projection(
            y_s, mod[l], lat_row, ng, w_in_p, qg, kg, seg, rope_tabs, ls, MXU_DTYPE)
        r3 = lambda a: a.reshape(bs, ls, a.shape[-1])
        ck = cache_ga_k[:, l].reshape(bs, lc, LANE)
        cv = cache_ga_v[:, l].reshape(bs, lc, LANE)
        ya = _attention(r3(q), r3(k), r3(v), ck, cv, 2 * GRID_W, "ga_lat")
        nck = cache_na_k[:, l].reshape(bs, lc, 512)
        ncv = cache_na_v[:, l].reshape(bs, lc, 512)
        yc = _neighbourhood_attention(r3(nq), r3(nk), r3(nv), nck, ncv, na_rpb[l])
        h0 = _pack_state(state_ssm_re[:, l], state_ssm_im[:, l])
        ys, _ = _ssm_scan(zu, *ssm_ops, h0, bs, ls)
        x1, h2 = _merge(ya.reshape(-1, 512), ys, zu, yc.reshape(-1, 512), gates, y_s,
                        mod[l], lat_row, ng, d_l, w_glu_l, w_a_p, w_b_l, w_c_l, w_o_l)
        y_s = _conv_ffn(h2, x1, mod[l], lat_row, ng, *ffn_w, ls)

    new = [jnp.stack(lst, axis=1) for lst in outs]
    return (y_p.reshape(bp, lp, D_MODEL), y_s.reshape(bs, ls, D_MODEL), *new)
```

```python
import functools
import math

import numpy as np
import jax
import jax.numpy as jnp
from jax import lax
from jax.experimental import pallas as pl
from jax.experimental.pallas import tpu as pltpu

F32 = jnp.float32
MXU_DTYPE = jnp.bfloat16

D_MODEL = 1024
HEAD_DIM = 64
N_HEADS = 8
GA_KV_HEADS = 2
GRID_W = 64
NA_WIN_ROWS = 8
NA_WIN_COLS = 16
NA_KEY_ROWS = 10
SSM_WIDTH = 512
SSM_GROUPS = 32
SSM_GROUP_CH = 16
SSM_STATE = 64
SSM_CHUNK = 16
SSM_PAIRS = SSM_GROUPS // 2
D_FF = 2816
FF_TILE = 1408
ROPE_THETA = 10000.0
EPS = 1e-6
IN_WIDTH = 5888
NEG_BIG = -1e30

LANE = 128
TOKEN_TILE = 512
FFN_TOKEN_TILE = 1024
VMEM_LIMIT = 56 * 1024 * 1024

_Q0, _K0, _V0, _U0, _NQ0, _NK0, _NV0, _G0 = 0, 512, 640, 768, 1280, 1792, 2304, 2816


def _sigmoid(x):
    return 1.0 / (1.0 + jnp.exp(-x))


def _gelu_tanh(x):
    return 0.5 * x * (1.0 + jnp.tanh(math.sqrt(2.0 / math.pi) * (x + 0.044715 * (x * x * x))))


def _rms(x, g):
    ms = jnp.mean(x * x, axis=-1, keepdims=True)
    return (x * lax.rsqrt(ms + EPS)) * g


def _mm(a, b):
    return jnp.dot(a.astype(MXU_DTYPE), b.astype(MXU_DTYPE), preferred_element_type=F32)


def _mm_nt(a, b):
    return lax.dot_general(a.astype(MXU_DTYPE), b.astype(MXU_DTYPE), (((1,), (1,)), ((), ())),
                           preferred_element_type=F32)


def _params(sem):
    return pltpu.CompilerParams(dimension_semantics=sem, vmem_limit_bytes=VMEM_LIMIT)


def _mod_kernel(c_ref, w_ref, b_ref, o_ref):
    c = c_ref[...]
    o_ref[...] = _mm(c * _sigmoid(c), w_ref[...]) + b_ref[...]


def _modulation(cvec, w_mod, b_mod):
    depth = w_mod.shape[0]
    rows = cvec.shape[0]
    tn = 1536
    return pl.pallas_call(
        _mod_kernel,
        grid=(depth, 6 * D_MODEL // tn),
        in_specs=[
            pl.BlockSpec((rows, D_MODEL), lambda l, j: (0, 0)),
            pl.BlockSpec((None, D_MODEL, tn), lambda l, j: (l, 0, j)),
            pl.BlockSpec((None, 1, tn), lambda l, j: (l, 0, j)),
        ],
        out_specs=pl.BlockSpec((None, rows, tn), lambda l, j: (l, 0, j)),
        out_shape=jax.ShapeDtypeStruct((depth, rows, 6 * D_MODEL), F32),
        compiler_params=_params(("parallel", "parallel")),
        name="adaln_mod",
    )(cvec, w_mod, b_mod.reshape(depth, 1, 6 * D_MODEL))


def _head_rms(z, seg, gain):
    ms = jnp.dot((z * z).astype(MXU_DTYPE), seg, preferred_element_type=F32)
    return (z * lax.rsqrt(ms + EPS)) * gain


def _rope_tile(t, c, s_up, s_dn):
    return t * c + pltpu.roll(t, LANE - 16, 1) * s_up + pltpu.roll(t, 16, 1) * s_dn


def _inproj_kernel(*refs, rope):
    if rope:
        (x_ref, mod_ref, ng_ref, w_ref, qg_ref, kg_ref, seg_ref, cos_ref, sup_ref, sdn_ref,
         q_o, k_o, v_o, u_o, nq_o, nk_o, nv_o, g_o) = refs
    else:
        (x_ref, mod_ref, ng_ref, w_ref, qg_ref, kg_ref, seg_ref,
         q_o, k_o, v_o, u_o, nq_o, nk_o, nv_o, g_o) = refs
    x = x_ref[...]
    h = _rms(x, ng_ref[0:1, :]) * (1.0 + mod_ref[1:2, :]) + mod_ref[0:1, :]
    hb = h.astype(MXU_DTYPE)
    scale = HEAD_DIM ** -0.5

    def proj(lo, width):
        return jnp.dot(hb, w_ref[:, lo:lo + width], preferred_element_type=F32)

    def maybe_rope(z):
        if not rope:
            return z
        c, su, sd = cos_ref[...], sup_ref[...], sdn_ref[...]
        tiles = [_rope_tile(z[:, i * LANE:(i + 1) * LANE], c, su, sd) for i in range(z.shape[1] // LANE)]
        return tiles[0] if len(tiles) == 1 else jnp.concatenate(tiles, axis=1)

    q = maybe_rope(_head_rms(proj(_Q0, 512), seg_ref[...], qg_ref[...]))
    q_o[...] = (q * scale).astype(q_o.dtype)
    k = maybe_rope(_head_rms(proj(_K0, 128), seg_ref[0:LANE, 0:LANE], kg_ref[...]))
    k_o[...] = k.astype(k_o.dtype)
    v_o[...] = proj(_V0, 128).astype(v_o.dtype)
    u_o[...] = proj(_U0, 512).astype(u_o.dtype)
    nq_o[...] = (proj(_NQ0, 512) * scale).astype(nq_o.dtype)
    nk_o[...] = proj(_NK0, 512).astype(nk_o.dtype)
    nv_o[...] = proj(_NV0, 512).astype(nv_o.dtype)
    for i in range(3):
        g_o[:, i * D_MODEL:(i + 1) * D_MODEL] = _sigmoid(proj(_G0 + i * D_MODEL, D_MODEL)).astype(g_o.dtype)


def _in_projection(x2d, mod_l, mod_row, ng, w_in, qg, kg, seg, rope_tabs, seq_len, kv_dtype):
    t = x2d.shape[0]
    tm = TOKEN_TILE
    tiles_per_seq = max(seq_len // tm, 1)
    rope = rope_tabs is not None
    row = lambda i: (i, 0)
    const = lambda i: (0, 0)
    in_specs = [
        pl.BlockSpec((tm, D_MODEL), row),
        pl.BlockSpec((None, 6, D_MODEL), lambda i: (mod_row(i, tm), 0, 0)),
        pl.BlockSpec((4, D_MODEL), const),
        pl.BlockSpec((D_MODEL, IN_WIDTH), const),
        pl.BlockSpec((1, 512), const),
        pl.BlockSpec((1, LANE), const),
        pl.BlockSpec((512, 512), const),
    ]
    args = [x2d, mod_l, ng, w_in, qg, kg, seg]
    if rope:
        in_specs += [pl.BlockSpec((tm, LANE), lambda i: (i % tiles_per_seq, 0))] * 3
        args += list(rope_tabs)
    widths = (512, 128, 128, 512, 512, 512, 512, 3 * D_MODEL)
    dtypes = (MXU_DTYPE, kv_dtype, kv_dtype, F32, MXU_DTYPE, kv_dtype, kv_dtype, MXU_DTYPE)
    return pl.pallas_call(
        functools.partial(_inproj_kernel, rope=rope),
        grid=(t // tm,),
        in_specs=in_specs,
        out_specs=[pl.BlockSpec((tm, w), row) for w in widths],
        out_shape=[jax.ShapeDtypeStruct((t, w), dt) for w, dt in zip(widths, dtypes)],
        compiler_params=_params(("parallel",)),
        name="in_proj_rope" if rope else "in_proj",
    )(*args)


def _lane_masks(dtype):
    lane = lax.broadcasted_iota(jnp.int32, (1, LANE), 1)
    lo = lane < HEAD_DIM
    return lo, lo.astype(dtype), (~lo).astype(dtype)


_KEY_BLOCK = 256
_Q_SUB = 128


def _softmax_pv(qs, key_blocks, s_ref):
    macc = None
    for bi, (score_fn, _) in enumerate(key_blocks):
        sj = score_fn(qs)
        s_ref[bi] = sj
        macc = sj if macc is None else jnp.maximum(macc, sj)
    mb = jnp.broadcast_to(jnp.max(macc, axis=-1, keepdims=True), macc.shape)
    lacc = jnp.zeros(macc.shape, F32)
    o = jnp.zeros((qs.shape[0], LANE), F32)
    for bi, (_, v_fn) in enumerate(key_blocks):
        p = jnp.exp(s_ref[bi] - mb)
        lacc = lacc + p
        vb = v_fn()
        o = o + _mm(p[:, :vb.shape[0]], vb)
    return o * (1.0 / jnp.sum(lacc, axis=-1, keepdims=True))


def _attn_kernel(*refs, kv_tiles, cached):
    if cached:
        q_ref, k_ref, v_ref, kc_ref, vc_ref, o_ref, s_ref = refs
    else:
        q_ref, k_ref, v_ref, o_ref, s_ref = refs
    tq = q_ref.shape[0]
    kb = _KEY_BLOCK
    lo, m_lo, m_hi = _lane_masks(MXU_DTYPE)
    pairs_per_kv = (N_HEADS // 2) // kv_tiles
    sources = [(k_ref, v_ref)] + ([(kc_ref, vc_ref)] if cached else [])
    for q0 in range(0, tq, _Q_SUB):
        for hp in range(N_HEADS // 2):
            ksl = slice((hp // pairs_per_kv) * LANE, (hp // pairs_per_kv + 1) * LANE)
            q2 = q_ref[q0:q0 + _Q_SUB, hp * LANE:(hp + 1) * LANE]
            qs = jnp.concatenate([q2 * m_lo, q2 * m_hi], axis=0)
            blocks = [(functools.partial(lambda x, kr, off, ksl: _mm_nt(x, kr[off:off + kb, ksl]),
                                         kr=kr, off=off, ksl=ksl),
                       functools.partial(lambda vr, off, ksl: vr[off:off + kb, ksl], vr=vr, off=off, ksl=ksl))
                      for kr, vr in sources for off in range(0, kr.shape[0], kb)]
            o = _softmax_pv(qs, blocks, s_ref)
            o_ref[q0:q0 + _Q_SUB, hp * LANE:(hp + 1) * LANE] = jnp.where(
                lo, o[:_Q_SUB], o[_Q_SUB:]).astype(o_ref.dtype)


def _attention(q, k, v, kc, vc, tq, name):
    b, lq, _ = q.shape
    lk, kw = k.shape[1], k.shape[2]
    cached = kc is not None
    qmap = lambda bi, ti: (bi, ti, 0)
    kmap = lambda bi, ti: (bi, 0, 0)
    in_specs = [
        pl.BlockSpec((None, tq, 512), qmap),
        pl.BlockSpec((None, lk, kw), kmap),
        pl.BlockSpec((None, lk, kw), kmap),
    ]
    args = [q, k, v]
    if cached:
        lc = kc.shape[1]
        in_specs += [pl.BlockSpec((None, lc, kw), kmap)] * 2
        args += [kc, vc]
    assert lk % _KEY_BLOCK == 0 and (not cached or kc.shape[1] % _KEY_BLOCK == 0) and tq % _Q_SUB == 0
    n_blocks = (lk + (kc.shape[1] if cached else 0)) // _KEY_BLOCK
    return pl.pallas_call(
        functools.partial(_attn_kernel, kv_tiles=kw // LANE, cached=cached),
        grid=(b, lq // tq),
        in_specs=in_specs,
        out_specs=pl.BlockSpec((None, tq, 512), qmap),
        out_shape=jax.ShapeDtypeStruct((b, lq, 512), MXU_DTYPE),
        scratch_shapes=[pltpu.VMEM((n_blocks, 2 * _Q_SUB, _KEY_BLOCK), F32)],
        compiler_params=_params(("parallel", "parallel")),
        name=name,
    )(*args)


def _na_geometry(seq_len):
    rows = seq_len // GRID_W
    n_tiles = rows // 2
    assert rows >= NA_KEY_ROWS and NA_WIN_ROWS <= rows and NA_KEY_ROWS % 2 == 0
    ws = np.clip(2 * np.arange(n_tiles) - NA_WIN_ROWS // 2, 0, rows - NA_KEY_ROWS)
    r = 2 * np.arange(n_tiles)[:, None, None] + np.arange(2)[None, :, None]
    key_r = ws[:, None, None] + np.arange(NA_KEY_ROWS)[None, None, :]
    r0 = np.clip(r - NA_WIN_ROWS // 2, 0, rows - NA_WIN_ROWS)
    valid = (key_r >= r0) & (key_r < r0 + NA_WIN_ROWS)
    dr = np.where(valid, key_r - r + NA_WIN_ROWS - 1, 2 * NA_WIN_ROWS - 1)
    assert (valid.sum(-1) == NA_WIN_ROWS).all()
    return ws.astype(np.int32), dr.reshape(-1).astype(np.int32)


def _na_bias_blocks(rpb):
    h = rpb.shape[0]
    nrel = 2 * NA_WIN_ROWS - 1
    zeros = jnp.zeros((h, nrel, LANE - (2 * NA_WIN_COLS - 1)), F32)
    v = jnp.concatenate([rpb[..., NA_WIN_COLS - 1:], zeros, rpb[..., :NA_WIN_COLS - 1]], axis=-1).astype(F32)
    t = jnp.tile(v, (1, 1, GRID_W))[..., :GRID_W * (LANE - 1)].reshape(h, nrel, GRID_W, LANE - 1)[..., :GRID_W]
    c = np.arange(GRID_W)
    c0 = np.clip(c - NA_WIN_COLS // 2, 0, GRID_W - NA_WIN_COLS)
    colmask = (c[None, :] >= c0[:, None]) & (c[None, :] < c0[:, None] + NA_WIN_COLS)
    t = jnp.where(jnp.asarray(colmask)[None, None], t, NEG_BIG)
    t = jnp.concatenate([t, jnp.full((h, 1, GRID_W, GRID_W), NEG_BIG, F32)], axis=1)
    pad = jnp.zeros_like(t)
    return jnp.concatenate([t, pad], axis=-1), jnp.concatenate([pad, t], axis=-1)


def _na_kernel(ws_ref, dr_ref, q_ref, k_ref, v_ref, kc_ref, vc_ref, bl_ref, br_ref, o_ref, s_ref):
    i = pl.program_id(1)
    start = pl.multiple_of(ws_ref[i] * GRID_W, GRID_W)
    nk = NA_KEY_ROWS * GRID_W
    kb = _KEY_BLOCK
    tq = q_ref.shape[0]
    lc = kc_ref.shape[0]
    lo, m_lo, m_hi = _lane_masks(MXU_DTYPE)

    def bias_block(hp, off, width):
        rows = []
        for h in (2 * hp, 2 * hp + 1):
            for qr in range(2):
                base = (i * 2 + qr) * NA_KEY_ROWS + off // GRID_W
                tiles = [bl_ref[h, dr_ref[base + 2 * kp]] + br_ref[h, dr_ref[base + 2 * kp + 1]]
                         for kp in range(width // LANE)]
                rows.append(tiles[0] if len(tiles) == 1 else jnp.concatenate(tiles, axis=1))
        return jnp.concatenate(rows, axis=0)

    def local_scores(x, hp, sl, off, width):
        s = _mm_nt(x, k_ref[pl.ds(start + off, width), sl]) + bias_block(hp, off, width)
        if width < kb:
            s = jnp.concatenate([s, jnp.full((s.shape[0], kb - width), NEG_BIG, F32)], axis=1)
        return s

    for hp in range(N_HEADS // 2):
        sl = slice(hp * LANE, (hp + 1) * LANE)
        q2 = q_ref[:, sl]
        qs = jnp.concatenate([q2 * m_lo, q2 * m_hi], axis=0)
        blocks = []
        for off in range(0, nk, kb):
            width = min(kb, nk - off)
            blocks.append((functools.partial(local_scores, hp=hp, sl=sl, off=off, width=width),
                           functools.partial(lambda sl, off, width: v_ref[pl.ds(start + off, width), sl],
                                             sl=sl, off=off, width=width)))
        for off in range(0, lc, kb):
            blocks.append((functools.partial(lambda x, sl, off: _mm_nt(x, kc_ref[off:off + kb, sl]), sl=sl, off=off),
                           functools.partial(lambda sl, off: vc_ref[off:off + kb, sl], sl=sl, off=off)))
        o = _softmax_pv(qs, blocks, s_ref)
        o_ref[:, sl] = jnp.where(lo, o[:tq], o[tq:]).astype(o_ref.dtype)


def _neighbourhood_attention(q, k, v, kc, vc, rpb):
    b, seq_len, _ = q.shape
    lc = kc.shape[1]
    ws, dr = _na_geometry(seq_len)
    b_left, b_right = _na_bias_blocks(rpb)
    n_tiles = len(ws)
    tq = 2 * GRID_W
    qmap = lambda bi, ti, ws_r, dr_r: (bi, ti, 0)
    kmap = lambda bi, ti, ws_r, dr_r: (bi, 0, 0)
    bmap = lambda bi, ti, ws_r, dr_r: (0, 0, 0, 0)
    grid_spec = pltpu.PrefetchScalarGridSpec(
        num_scalar_prefetch=2,
        grid=(b, n_tiles),
        in_specs=[
            pl.BlockSpec((None, tq, 512), qmap),
            pl.BlockSpec((None, seq_len, 512), kmap),
            pl.BlockSpec((None, seq_len, 512), kmap),
            pl.BlockSpec((None, lc, 512), kmap),
            pl.BlockSpec((None, lc, 512), kmap),
            pl.BlockSpec(b_left.shape, bmap),
            pl.BlockSpec(b_right.shape, bmap),
        ],
        out_specs=pl.BlockSpec((None, tq, 512), qmap),
        scratch_shapes=[pltpu.VMEM((-(-NA_KEY_ROWS * GRID_W // _KEY_BLOCK) + lc // _KEY_BLOCK, 2 * tq, _KEY_BLOCK),
                                   F32)],
    )
    assert lc % _KEY_BLOCK == 0
    return pl.pallas_call(
        _na_kernel,
        grid_spec=grid_spec,
        out_shape=jax.ShapeDtypeStruct((b, seq_len, 512), MXU_DTYPE),
        compiler_params=_params(("parallel", "arbitrary")),
        name="na_attn",
    )(jnp.asarray(ws), jnp.asarray(dr), q, k, v, kc, vc, b_left, b_right)


def _cmul(ar, ai, br, bi):
    return ar * br - ai * bi, ar * bi + ai * br


def _lam_bar(lr, li, ls):
    dt = jnp.exp(ls)
    mag = jnp.exp(lr * dt)
    return mag * jnp.cos(li * dt), mag * jnp.sin(li * dt)


def _zoh_coef(lr, li, zr, zi):
    nr, ni = zr - 1.0, zi
    den = 1.0 / (lr * lr + li * li)
    return (nr * lr + ni * li) * den, (ni * lr - nr * li) * den


def _squarings(zr, zi, n):
    out = [(zr, zi)]
    for _ in range(n - 1):
        zr, zi = _cmul(zr, zi, zr, zi)
        out.append((zr, zi))
    return out


def _cpow(squares, e):
    pr, pi = jnp.ones(e.shape, F32), jnp.zeros(e.shape, F32)
    for k, (zr, zi) in enumerate(squares):
        bit = jnp.bitwise_and(jnp.right_shift(e, k), 1) == 1
        nr, ni = _cmul(pr, pi, zr, zi)
        pr, pi = jnp.where(bit, nr, pr), jnp.where(bit, ni, pi)
    return pr, pi


def _ssm_ops_kernel(lrr_ref, lir_ref, lsr_ref, lrc_ref, lic_ref, lsc_ref, bt_ref, ct_ref, s0_ref, s1_ref,
                    wt_o, wb_o, wc_o, l16_o, tg_ref):
    tc, hg, p = SSM_CHUNK, SSM_GROUP_CH, SSM_STATE
    w = tc * hg
    lane_w = lax.broadcasted_iota(jnp.int32, (1, w), 1)
    lane_p = lax.broadcasted_iota(jnp.int32, (1, 2 * p), 1)
    row_w = lax.broadcasted_iota(jnp.int32, (2 * w, 1), 0)
    row_p = lax.broadcasted_iota(jnp.int32, (2 * p, 1), 0)
    tau_of_lane = jnp.right_shift(lane_w, 4)
    gl_of_lane = jnp.right_shift(lane_p, 6)
    step_of_row = jnp.right_shift(row_w, 5)
    same_group = jnp.bitwise_and(jnp.right_shift(row_w, 4), 1) == gl_of_lane
    first_rows = row_p < p
    tg_ref[...] = jnp.zeros_like(tg_ref)
    l16_rows = []
    for d in range(2):
        lr, li = lrr_ref[d], lir_ref[d]
        zr, zi = _lam_bar(lr, li, lsr_ref[d])
        cfr, cfi = _zoh_coef(lr, li, zr, zi)
        btr, bti = _cmul(cfr, cfi, bt_ref[d, 0], bt_ref[d, 1])
        sq = _squarings(zr, zi, 5)
        e_inj = (tc - 1 - step_of_row) if d == 0 else step_of_row
        pr, pi = _cpow(sq[:4], jnp.broadcast_to(e_inj, (2 * w, 2 * p)))
        ir, ii = _cmul(pr, pi, jnp.tile(btr, (2 * tc, 1)), jnp.tile(bti, (2 * tc, 1)))
        wb_o[:, (2 * d) * LANE:(2 * d + 1) * LANE] = jnp.where(same_group, ir, 0.0).astype(wb_o.dtype)
        wb_o[:, (2 * d + 1) * LANE:(2 * d + 2) * LANE] = jnp.where(same_group, ii, 0.0).astype(wb_o.dtype)
        l16_rows += [sq[4][0], sq[4][1]]
        lrc, lic = lrc_ref[d], lic_ref[d]
        zcr, zci = _lam_bar(lrc, lic, lsc_ref[d])
        tau = tau_of_lane if d == 0 else (tc - 1) - tau_of_lane
        pr, pi = _cpow(_squarings(zcr, zci, 4), jnp.broadcast_to(tau, (2 * p, w)))
        c0r, c0i = _cmul(ct_ref[d, 0], ct_ref[d, 1], pr, pi)
        c1r, c1i = _cmul(c0r, c0i, zcr, zci)
        for r, val in ((2 * d, c1r), (2 * d + 1, -c1i)):
            vb = val.astype(MXU_DTYPE)
            spread = jnp.where(first_rows, jnp.dot(vb, s0_ref[...], preferred_element_type=F32),
                               jnp.dot(vb, s1_ref[...], preferred_element_type=F32))
            wc_o[r * LANE:(r + 1) * LANE, :] = spread.astype(wc_o.dtype)
        for gl in range(2):
            in_group = gl_of_lane == gl
            kt = (jnp.dot(jnp.where(in_group, btr, 0.0), c0r, preferred_element_type=F32,
                          precision=lax.Precision.HIGHEST)
                  - jnp.dot(jnp.where(in_group, bti, 0.0), c0i, preferred_element_type=F32,
                            precision=lax.Precision.HIGHEST))
            for s in range(tc):
                if d == 0:
                    shift, keep = hg * s, lane_w >= hg * s
                else:
                    shift, keep = (w - hg * (tc - 1 - s)) % w, lane_w < hg * (s + 1)
                rolled = kt if shift == 0 else pltpu.roll(kt, shift, 1)
                tg_ref[gl, s * hg:(s + 1) * hg, :] += jnp.where(keep, rolled, 0.0)
    for gl, s_ref in enumerate((s0_ref, s1_ref)):
        spread = jnp.dot(tg_ref[gl].astype(MXU_DTYPE), s_ref[...], preferred_element_type=F32)
        for s in range(tc):
            wt_o[(2 * s + gl) * hg:(2 * s + gl + 1) * hg, :] = spread[s * hg:(s + 1) * hg].astype(wt_o.dtype)
    l16_o[...] = jnp.concatenate(l16_rows + [jnp.zeros((4, 2 * p), F32)], axis=0)


def _ssm_operators(lam_re, lam_im, log_step, b_re, b_im, c_re, c_im):
    depth = lam_re.shape[0]
    p, hg, tc, q = SSM_STATE, SSM_GROUP_CH, SSM_CHUNK, SSM_PAIRS
    assert (hg, p, tc) == (16, 64, 16), "lane/row index arithmetic in the kernel uses these as shifts"
    w = tc * hg

    def per_pair(a, tail):
        a = a.astype(F32).reshape((depth, 2, q, 2) + tail)
        return jnp.transpose(a, (0, 2, 1, 3) + tuple(range(4, 4 + len(tail))))

    lam_r, lam_i = per_pair(lam_re, (p,)), per_pair(lam_im, (p,))
    ls = jnp.broadcast_to(per_pair(log_step, ())[..., None], lam_r.shape)
    rows = [a.reshape(depth, q, 2, 1, 2 * p) for a in (lam_r, lam_i, ls)]
    cols = [a.reshape(depth, q, 2, 2 * p, 1) for a in (lam_r, lam_i, ls)]
    bt = jnp.stack([per_pair(b_re, (p, hg)), per_pair(b_im, (p, hg))], axis=3)
    bt = jnp.transpose(bt, (0, 1, 2, 3, 6, 4, 5)).reshape(depth, q, 2, 2, hg, 2 * p)
    ct = jnp.stack([per_pair(c_re, (hg, p)), per_pair(c_im, (hg, p))], axis=3)
    ct = jnp.transpose(ct, (0, 1, 2, 3, 4, 6, 5)).reshape(depth, q, 2, 2, 2 * p, hg)
    ct = jnp.tile(ct, (1, 1, 1, 1, 1, tc))
    r, c = np.arange(w)[:, None], np.arange(2 * w)[None, :]
    hit = (r // hg == c // (2 * hg)) & (r % hg == c % hg)
    spread = [jnp.asarray(hit & ((c // hg) % 2 == gl), MXU_DTYPE) for gl in range(2)]

    blk = lambda a: pl.BlockSpec((None, None) + a.shape[2:], lambda l, i: (l, i) + (0,) * (a.ndim - 2))
    const = pl.BlockSpec((w, 2 * w), lambda l, i: (0, 0))
    mat = pl.BlockSpec((None, None, 2 * w, 2 * w), lambda l, i: (l, i, 0, 0))
    args = rows + cols + [bt, ct]
    return pl.pallas_call(
        _ssm_ops_kernel,
        grid=(depth, q),
        in_specs=[blk(a) for a in args] + [const, const],
        out_specs=[mat, mat, mat, pl.BlockSpec((None, None, 8, 2 * p), lambda l, i: (l, i, 0, 0))],
        out_shape=[jax.ShapeDtypeStruct((depth, q, 2 * w, 2 * w), MXU_DTYPE)] * 3
        + [jax.ShapeDtypeStruct((depth, q, 8, 2 * p), F32)],
        scratch_shapes=[pltpu.VMEM((2, w, w), F32)],
        compiler_params=_params(("parallel", "parallel")),
        name="ssm_ops",
    )(*args, *spread)


_PAIRS_PER_TILE = 4
_SSM_ROW_BLOCK = 64


def _ssm_kernel(zu_ref, wt_ref, wb_ref, wc_ref, l16_ref, h0_ref, y_o, fin_o, u_ref, yp_ref, *state_refs,
                batch, seq_len):
    tc, npair = SSM_CHUNK, _PAIRS_PER_TILE
    n_chunks = seq_len // tc
    nrows = batch * n_chunks
    rb = _SSM_ROW_BLOCK
    slot_w = LANE // npair
    slot = jnp.right_shift(lax.broadcasted_iota(jnp.int32, (1, LANE), 1), 5)
    dx_refs, xs_refs = state_refs[:4], state_refs[4:]

    def place(pieces, src_slot):
        offset = src_slot
        out = None
        for j, piece in enumerate(pieces):
            shift = (slot_w * (j - offset[j])) % LANE
            r = piece if shift == 0 else pltpu.roll(piece, shift, 1)
            out = r if out is None else jnp.where(slot == j, r, out)
        return out

    def gather_block(i, carry):
        r0 = pl.multiple_of(i * rb, rb)
        steps = [zu_ref[pl.ds(r0 * tc + s, rb, stride=tc), :] for s in range(tc)]
        for p in range(npair):
            tiles = [place(steps[4 * k:4 * k + 4], [p] * 4) for k in range(tc // 4)]
            u_ref[p, pl.ds(r0, rb), :] = jnp.concatenate(tiles, axis=1).astype(u_ref.dtype)
        return carry

    lax.fori_loop(0, nrows // rb, gather_block, 0)

    for p in range(npair):
        u = u_ref[p]
        y_intra = jnp.dot(u, wt_ref[p], preferred_element_type=F32)
        dx = jnp.dot(u, wb_ref[p], preferred_element_type=F32)
        for r in range(4):
            dx_refs[r][...] = dx[:, r * LANE:(r + 1) * LANE]
        lfr, lfi, lbr, lbi = (l16_ref[p, r:r + 1, :] for r in range(4))

        def body(c, carry):
            fr, fi, br, bi = carry
            fwd = pl.ds(c, batch, stride=n_chunks)
            bwd = pl.ds(n_chunks - 1 - c, batch, stride=n_chunks)
            xs_refs[0][fwd, :] = fr
            xs_refs[1][fwd, :] = fi
            xs_refs[2][bwd, :] = br
            xs_refs[3][bwd, :] = bi
            nfr = lfr * fr - lfi * fi + dx_refs[0][fwd, :]
            nfi = lfr * fi + lfi * fr + dx_refs[1][fwd, :]
            nbr = lbr * br - lbi * bi + dx_refs[2][bwd, :]
            nbi = lbr * bi + lbi * br + dx_refs[3][bwd, :]
            return nfr, nfi, nbr, nbi

        fin = lax.fori_loop(0, n_chunks, body, tuple(h0_ref[p, :, r * LANE:(r + 1) * LANE] for r in range(4)),
                            unroll=4)
        for r in range(4):
            fin_o[p, :, r * LANE:(r + 1) * LANE] = fin[r]
        xs = jnp.concatenate([x[...] for x in xs_refs], axis=1).astype(MXU_DTYPE)
        yp_ref[p] = y_intra + jnp.dot(xs, wc_ref[p], preferred_element_type=F32)

    def scatter_block(i, carry):
        r0 = pl.multiple_of(i * rb, rb)
        for k in range(tc // 4):
            pieces = [yp_ref[p, pl.ds(r0, rb), k * LANE:(k + 1) * LANE] for p in range(npair)]
            for j in range(4):
                y_o[pl.ds(r0 * tc + 4 * k + j, rb, stride=tc), :] = place(pieces, [j] * npair)
        return carry

    lax.fori_loop(0, nrows // rb, scatter_block, 0)


def _ssm_scan(zu2d, w_t, w_b, w_c, l16, h0, batch, seq_len):
    t = zu2d.shape[0]
    nrows = t // SSM_CHUNK
    npair = _PAIRS_PER_TILE
    assert nrows % _SSM_ROW_BLOCK == 0 and seq_len % SSM_CHUNK == 0
    once = pl.Buffered(1)
    wspec = pl.BlockSpec((npair, 512, 512), lambda i: (i, 0, 0))
    return pl.pallas_call(
        functools.partial(_ssm_kernel, batch=batch, seq_len=seq_len),
        grid=(SSM_PAIRS // npair,),
        in_specs=[
            pl.BlockSpec((t, LANE), lambda i: (0, i), pipeline_mode=once),
            wspec, wspec, wspec,
            pl.BlockSpec((npair, 8, LANE), lambda i: (i, 0, 0)),
            pl.BlockSpec((npair, batch, 512), lambda i: (i, 0, 0)),
        ],
        out_specs=[
            pl.BlockSpec((t, LANE), lambda i: (0, i), pipeline_mode=once),
            pl.BlockSpec((npair, batch, 512), lambda i: (i, 0, 0)),
        ],
        out_shape=[jax.ShapeDtypeStruct((t, SSM_WIDTH), F32),
                   jax.ShapeDtypeStruct((SSM_PAIRS, batch, 512), F32)],
        scratch_shapes=[pltpu.VMEM((npair, nrows, 512), MXU_DTYPE), pltpu.VMEM((npair, nrows, 512), F32)]
        + [pltpu.VMEM((nrows, LANE), F32)] * 8,
        compiler_params=_params(("arbitrary",)),
        name="ssm_scan",
    )(zu2d, w_t, w_b, w_c, l16, h0)


def _pack_state(s_re, s_im):
    b = s_re.shape[0]
    a = jnp.stack([s_re, s_im], axis=2).reshape(b, 2, 2, SSM_PAIRS, 2 * SSM_STATE)
    return jnp.transpose(a, (3, 0, 1, 2, 4)).reshape(SSM_PAIRS, b, 4 * 2 * SSM_STATE).astype(F32)


def _unpack_state(fin):
    b = fin.shape[1]
    a = jnp.transpose(fin.reshape(SSM_PAIRS, b, 2, 2, 2 * SSM_STATE), (1, 2, 3, 0, 4))
    a = a.reshape(b, 2, 2, SSM_GROUPS, SSM_STATE)
    return a[:, :, 0], a[:, :, 1]


def _merge_kernel(ya_ref, ys_ref, u_ref, yc_ref, g_ref, x_ref, mod_ref, ng_ref, d_ref,
                  wglu_ref, wa_ref, wb_ref, wc_ref, wo_ref, x_o, h_o):
    y = ys_ref[...].astype(F32) + d_ref[...] * u_ref[...]
    gl = _gelu_tanh(y)
    yb = gl * _sigmoid(_mm(gl, wglu_ref[...]))
    merged = (g_ref[:, 0:D_MODEL].astype(F32) * _mm(ya_ref[...], wa_ref[...])
              + g_ref[:, D_MODEL:2 * D_MODEL].astype(F32) * _mm(yb, wb_ref[...])
              + g_ref[:, 2 * D_MODEL:3 * D_MODEL].astype(F32) * _mm(yc_ref[...], wc_ref[...]))
    x1 = x_ref[...] + mod_ref[2:3, :] * _rms(_mm(merged, wo_ref[...]), ng_ref[1:2, :])
    x_o[...] = x1
    h_o[...] = (_rms(x1, ng_ref[2:3, :]) * (1.0 + mod_ref[4:5, :]) + mod_ref[3:4, :]).astype(h_o.dtype)


def _merge(ya, ys, zu, yc, gates, x2d, mod_l, mod_row, ng, ssm_d, w_glu, w_a, w_b, w_c, w_o):
    t = x2d.shape[0]
    tm = TOKEN_TILE
    row = lambda i: (i, 0)
    const = lambda i: (0, 0)
    r512 = pl.BlockSpec((tm, 512), row)
    wbr = pl.BlockSpec((512, D_MODEL), const)
    return pl.pallas_call(
        _merge_kernel,
        grid=(t // tm,),
        in_specs=[
            r512, r512, r512, r512,
            pl.BlockSpec((tm, 3 * D_MODEL), row),
            pl.BlockSpec((tm, D_MODEL), row),
            pl.BlockSpec((None, 6, D_MODEL), lambda i: (mod_row(i, tm), 0, 0)),
            pl.BlockSpec((4, D_MODEL), const),
            pl.BlockSpec((1, 512), const),
            pl.BlockSpec((512, 512), const),
            wbr, wbr, wbr,
            pl.BlockSpec((D_MODEL, D_MODEL), const),
        ],
        out_specs=[pl.BlockSpec((tm, D_MODEL), row), pl.BlockSpec((tm, D_MODEL), row)],
        out_shape=[jax.ShapeDtypeStruct((t, D_MODEL), F32), jax.ShapeDtypeStruct((t, D_MODEL), MXU_DTYPE)],
        compiler_params=_params(("parallel",)),
        name="merge",
    )(ya, ys, zu, yc, gates, x2d, mod_l, ng, ssm_d, w_glu, w_a, w_b, w_c, w_o)


def _ffn_kernel(*refs, seq_len, first, last):
    refs = list(refs)
    h_ref, hp_ref, hn_ref = refs[:3]
    del refs[:3]
    part_ref = None if first else refs.pop(0)
    if last:
        x_ref, mod_ref, ng_ref = refs[:3]
        del refs[:3]
    wa_ref, wg_ref, cwa_ref, cwg_ref, cba_ref, cbg_ref, wd_ref, out_ref, ua_ref, ug_ref, act_ref = refs
    i = pl.program_id(0)
    tm = h_ref.shape[0]
    ft = wd_ref.shape[0]
    n = tm + 16
    hh = jnp.concatenate([hp_ref[...], h_ref[...], hn_ref[...]], axis=0)
    ua_ref[...] = jnp.dot(hh, wa_ref[...], preferred_element_type=F32)
    ug_ref[...] = jnp.dot(hh, wg_ref[...], preferred_element_type=F32)
    pos = jnp.bitwise_and(i * tm + lax.broadcasted_iota(jnp.int32, (tm, 1), 0), seq_len - 1)
    has_prev = (pos != 0).astype(F32)
    has_next = (pos != seq_len - 1).astype(F32)

    def conv(u_ref, cw_ref, cb_ref, lo):
        uc = u_ref[:, lo:lo + LANE]
        up = pltpu.roll(uc, 1, 0)[8:8 + tm] * has_prev
        un = pltpu.roll(uc, n - 1, 0)[8:8 + tm] * has_next
        return (cw_ref[0:1, lo:lo + LANE] * up + cw_ref[1:2, lo:lo + LANE] * uc[8:8 + tm]
                + cw_ref[2:3, lo:lo + LANE] * un + cb_ref[0:1, lo:lo + LANE])

    for kc in range(ft // LANE):
        a = conv(ua_ref, cwa_ref, cba_ref, kc * LANE)
        g = conv(ug_ref, cwg_ref, cbg_ref, kc * LANE)
        act_ref[:, kc * LANE:(kc + 1) * LANE] = (g * _sigmoid(g) * a).astype(act_ref.dtype)
    total = jnp.dot(act_ref[...], wd_ref[...], preferred_element_type=F32)
    if not first:
        total = total + part_ref[...]
    if last:
        out_ref[...] = x_ref[...] + mod_ref[5:6, :] * _rms(total, ng_ref[3:4, :])
    else:
        out_ref[...] = total


def _conv_ffn(h2, x1, mod_l, mod_row, ng, w_up, conv_w, conv_b, w_down, seq_len):
    t = x1.shape[0]
    tm = FFN_TOKEN_TILE
    ft = FF_TILE
    nf = D_FF // ft
    nblk8 = t // 8
    assert seq_len & (seq_len - 1) == 0 and (seq_len % tm == 0 or tm % seq_len == 0) and t % tm == 0
    row = pl.BlockSpec((tm, D_MODEL), lambda i: (i, 0))
    once = pl.Buffered(1)
    part = None
    for j in range(nf):
        first, last = j == 0, j == nf - 1
        in_specs = [
            row,
            pl.BlockSpec((8, D_MODEL), lambda i: (jnp.maximum(i * (tm // 8) - 1, 0), 0)),
            pl.BlockSpec((8, D_MODEL), lambda i: (jnp.minimum((i + 1) * (tm // 8), nblk8 - 1), 0)),
        ]
        args = [h2, h2, h2]
        if not first:
            in_specs.append(row)
            args.append(part)
        if last:
            in_specs += [row, pl.BlockSpec((None, 6, D_MODEL), lambda i: (mod_row(i, tm), 0, 0)),
                         pl.BlockSpec((4, D_MODEL), lambda i: (0, 0))]
            args += [x1, mod_l, ng]
        in_specs += [
            pl.BlockSpec((D_MODEL, ft), lambda i, j=j: (0, j), pipeline_mode=once),
            pl.BlockSpec((D_MODEL, ft), lambda i, j=j: (0, nf + j), pipeline_mode=once),
            pl.BlockSpec((3, ft), lambda i, j=j: (0, j)),
            pl.BlockSpec((3, ft), lambda i, j=j: (0, nf + j)),
            pl.BlockSpec((1, ft), lambda i, j=j: (0, j)),
            pl.BlockSpec((1, ft), lambda i, j=j: (0, nf + j)),
            pl.BlockSpec((ft, D_MODEL), lambda i, j=j: (j, 0), pipeline_mode=once),
        ]
        args += [w_up, w_up, conv_w, conv_w, conv_b, conv_b, w_down]
        part = pl.pallas_call(
            functools.partial(_ffn_kernel, seq_len=seq_len, first=first, last=last),
            grid=(t // tm,),
            in_specs=in_specs,
            out_specs=row,
            out_shape=jax.ShapeDtypeStruct((t, D_MODEL), F32),
            scratch_shapes=[pltpu.VMEM((tm + 16, ft), F32), pltpu.VMEM((tm + 16, ft), F32),
                            pltpu.VMEM((tm, ft), MXU_DTYPE)],
            compiler_params=_params(("parallel",)),
            name="conv_ffn_last" if last else "conv_ffn_part",
        )(*args)
    return part


_Q_HEAD_ORDER = (0, 4, 1, 5, 2, 6, 3, 7)


def _rope_tables(seq_len):
    nf = HEAD_DIM // 4
    t = np.arange(seq_len)
    pos = np.stack([t // GRID_W, t % GRID_W]).astype(np.float32)
    inv = jnp.asarray(ROPE_THETA, F32) ** (-jnp.arange(nf, dtype=F32) / nf)
    ang = jnp.asarray(pos)[:, :, None] * inv
    d = np.arange(HEAD_DIM)
    ang = ang[d // (2 * nf), :, d % nf].T
    second = jnp.asarray(((d % (2 * nf)) // nf) == 1)[None, :]
    cos, sin = jnp.cos(ang), jnp.sin(ang)
    tabs = (cos, jnp.where(second, 0.0, -sin), jnp.where(second, sin, 0.0))
    return tuple(jnp.tile(x, (1, LANE // HEAD_DIM)).astype(F32) for x in tabs)


def _layer_weights(w_in, qk_g, w_br_a):
    hd = HEAD_DIM
    w_in_p = jnp.concatenate([w_in[:, h * hd:(h + 1) * hd] for h in _Q_HEAD_ORDER] + [w_in[:, 512:]],
                             axis=1).astype(MXU_DTYPE)
    w_a_p = jnp.concatenate([w_br_a[h * hd:(h + 1) * hd] for h in _Q_HEAD_ORDER], axis=0).astype(MXU_DTYPE)
    qg = jnp.tile(qk_g[0], N_HEADS).reshape(1, 512).astype(F32)
    kg = jnp.tile(qk_g[1], GA_KV_HEADS).reshape(1, LANE).astype(F32)
    return w_in_p, w_a_p, qg, kg


def kernel(x_prompt, x_sample, c, cache_ga_k, cache_ga_v, cache_na_k, cache_na_v, state_ssm_re, state_ssm_im,
           c_ctx, w_mod, b_mod, norm_g, w_in, qk_norm_g, na_rpb, ssm_lam_re, ssm_lam_im, ssm_log_step,
           ssm_b_re, ssm_b_im, ssm_c_re, ssm_c_im, ssm_d, w_glu, w_br_a, w_br_b, w_br_c, w_out,
           w_up, conv_w, conv_b, w_down):
    depth = w_in.shape[0]
    bp, lp, _ = x_prompt.shape
    bs, ls, _ = x_sample.shape
    lc = cache_ga_k.shape[2]
    assert lp % 256 == 0 and ls % FFN_TOKEN_TILE == 0 and (bp * lp) % FFN_TOKEN_TILE == 0
    assert FFN_TOKEN_TILE % TOKEN_TILE == 0
    assert bs % 8 == 0 and bp % 8 == 0, "the scan keeps one batch row per sublane"

    rows = 1 + bs
    rows_p = -(-rows // 8) * 8
    cvec = jnp.concatenate([c_ctx[None], c, jnp.zeros((rows_p - rows, D_MODEL), F32)], axis=0)
    mod = _modulation(cvec, w_mod, b_mod).reshape(depth, rows_p, 6, D_MODEL)

    w_t, w_b, w_c, l16 = _ssm_operators(ssm_lam_re, ssm_lam_im, ssm_log_step, ssm_b_re, ssm_b_im,
                                        ssm_c_re, ssm_c_im)
    seg = jnp.asarray(np.kron(np.eye(N_HEADS), np.full((HEAD_DIM, HEAD_DIM), 1.0 / HEAD_DIM)), MXU_DTYPE)
    rope_tabs = _rope_tables(ls)
    ctx_row = lambda i, tm: 0
    lat_row = lambda i, tm: 1 + (i * tm) // ls

    y_p = x_prompt.reshape(bp * lp, D_MODEL)
    y_s = x_sample.reshape(bs * ls, D_MODEL)
    zero_state = jnp.zeros((SSM_PAIRS, bp, 512), F32)
    outs = [[] for _ in range(6)]
    for l in range(depth):
        w_in_p, w_a_p, qg, kg = _layer_weights(w_in[l], qk_norm_g[l], w_br_a[l])
        w_glu_l, w_b_l, w_c_l, w_o_l = (a[l].astype(MXU_DTYPE) for a in (w_glu, w_br_b, w_br_c, w_out))
        ffn_w = (w_up[l].astype(MXU_DTYPE), conv_w[l].astype(F32), conv_b[l].reshape(1, 2 * D_FF).astype(F32),
                 w_down[l].astype(MXU_DTYPE))
        d_l = ssm_d[l].reshape(1, SSM_WIDTH).astype(F32)
        ng = norm_g[l].astype(F32)
        ssm_ops = (w_t[l], w_b[l], w_c[l], l16[l])

        q, k, v, zu, nq, nk, nv, gates = _in_projection(
            y_p, mod[l], ctx_row, ng, w_in_p, qg, kg, seg, None, lp, F32)
        r3 = lambda a: a.reshape(bp, lp, a.shape[-1])
        ya = _attention(r3(q), r3(k), r3(v), None, None, lp, "ga_ctx")
        yc = _attention(r3(nq), r3(nk), r3(nv), None, None, lp, "na_ctx")
        ys, fin = _ssm_scan(zu, *ssm_ops, zero_state, bp, lp)
        x1, h2 = _merge(ya.reshape(-1, 512), ys, zu, yc.reshape(-1, 512), gates, y_p,
                        mod[l], ctx_row, ng, d_l, w_glu_l, w_a_p, w_b_l, w_c_l, w_o_l)
        y_p = _conv_ffn(h2, x1, mod[l], ctx_row, ng, *ffn_w, lp)
        f_re, f_im = _unpack_state(fin)
        for lst, a in zip(outs, (k.reshape(bp, lp, GA_KV_HEADS, HEAD_DIM), v.reshape(bp, lp, GA_KV_HEADS, HEAD_DIM),
                                 nk.reshape(bp, lp, N_HEADS, HEAD_DIM), nv.reshape(bp, lp, N_HEADS, HEAD_DIM),
                                 f_re, f_im)):
            lst.append(a)

        q, k, v, zu, nq, nk, nv, gates = _in_projection(
            y_s, mod[l], lat_row, ng, w_in_p, qg, kg, seg, rope_tabs, ls, MXU_DTYPE)
        r3 = lambda a: a.reshape(bs, ls, a.shape[-1])
        ck = cache_ga_k[:, l].reshape(bs, lc, LANE)
        cv = cache_ga_v[:, l].reshape(bs, lc, LANE)
        ya = _attention(r3(q), r3(k), r3(v), ck, cv, 2 * GRID_W, "ga_lat")
        nck = cache_na_k[:, l].reshape(bs, lc, 512)
        ncv = cache_na_v[:, l].reshape(bs, lc, 512)
        yc = _neighbourhood_attention(r3(nq), r3(nk), r3(nv), nck, ncv, na_rpb[l])
        h0 = _pack_state(state_ssm_re[:, l], state_ssm_im[:, l])
        ys, _ = _ssm_scan(zu, *ssm_ops, h0, bs, ls)
        x1, h2 = _merge(ya.reshape(-1, 512), ys, zu, yc.reshape(-1, 512), gates, y_s,
                        mod[l], lat_row, ng, d_l, w_glu_l, w_a_p, w_b_l, w_c_l, w_o_l)
        y_s = _conv_ffn(h2, x1, mod[l], lat_row, ng, *ffn_w, ls)

    new = [jnp.stack(lst, axis=1) for lst in outs]
    return (y_p.reshape(bp, lp, D_MODEL), y_s.reshape(bs, ls, D_MODEL), *new)
```

```python
import functools
import math

import numpy as np
import jax
import jax.numpy as jnp
from jax import lax
from jax.experimental import pallas as pl
from jax.experimental.pallas import tpu as pltpu

F32 = jnp.float32
MXU_DTYPE = jnp.bfloat16

D_MODEL = 1024
HEAD_DIM = 64
N_HEADS = 8
GA_KV_HEADS = 2
GRID_W = 64
NA_WIN_ROWS = 8
NA_WIN_COLS = 16
NA_KEY_ROWS = 10
SSM_WIDTH = 512
SSM_GROUPS = 32
SSM_GROUP_CH = 16
SSM_STATE = 64
SSM_CHUNK = 16
SSM_PAIRS = SSM_GROUPS // 2
D_FF = 2816
FF_TILE = 1408
ROPE_THETA = 10000.0
EPS = 1e-6
IN_WIDTH = 5888
NEG_BIG = -1e30

LANE = 128
TOKEN_TILE = 512
FFN_TOKEN_TILE = 1024
VMEM_LIMIT = 56 * 1024 * 1024

_Q0, _K0, _V0, _U0, _NQ0, _NK0, _NV0, _G0 = 0, 512, 640, 768, 1280, 1792, 2304, 2816


def _sigmoid(x):
    return 1.0 / (1.0 + jnp.exp(-x))


def _gelu_tanh(x):
    return 0.5 * x * (1.0 + jnp.tanh(math.sqrt(2.0 / math.pi) * (x + 0.044715 * (x * x * x))))


def _rms(x, g):
    ms = jnp.mean(x * x, axis=-1, keepdims=True)
    return (x * lax.rsqrt(ms + EPS)) * g


def _mm(a, b):
    return jnp.dot(a.astype(MXU_DTYPE), b.astype(MXU_DTYPE), preferred_element_type=F32)


def _mm_nt(a, b):
    return lax.dot_general(a.astype(MXU_DTYPE), b.astype(MXU_DTYPE), (((1,), (1,)), ((), ())),
                           preferred_element_type=F32)


def _params(sem):
    return pltpu.CompilerParams(dimension_semantics=sem, vmem_limit_bytes=VMEM_LIMIT)


def _mod_kernel(c_ref, w_ref, b_ref, o_ref):
    c = c_ref[...]
    o_ref[...] = _mm(c * _sigmoid(c), w_ref[...]) + b_ref[...]


def _modulation(cvec, w_mod, b_mod):
    depth = w_mod.shape[0]
    rows = cvec.shape[0]
    tn = 1536
    return pl.pallas_call(
        _mod_kernel,
        grid=(depth, 6 * D_MODEL // tn),
        in_specs=[
            pl.BlockSpec((rows, D_MODEL), lambda l, j: (0, 0)),
            pl.BlockSpec((None, D_MODEL, tn), lambda l, j: (l, 0, j)),
            pl.BlockSpec((None, 1, tn), lambda l, j: (l, 0, j)),
        ],
        out_specs=pl.BlockSpec((None, rows, tn), lambda l, j: (l, 0, j)),
        out_shape=jax.ShapeDtypeStruct((depth, rows, 6 * D_MODEL), F32),
        compiler_params=_params(("parallel", "parallel")),
        name="adaln_mod",
    )(cvec, w_mod, b_mod.reshape(depth, 1, 6 * D_MODEL))


def _head_rms(z, seg, gain):
    ms = jnp.dot((z * z).astype(MXU_DTYPE), seg, preferred_element_type=F32)
    return (z * lax.rsqrt(ms + EPS)) * gain


def _rope_tile(t, c, s_up, s_dn):
    return t * c + pltpu.roll(t, LANE - 16, 1) * s_up + pltpu.roll(t, 16, 1) * s_dn


def _inproj_kernel(*refs, rope, n_carried):
    n_in = 10 if rope else 7
    refs = refs[:n_in] + refs[n_in + n_carried:]
    if rope:
        (x_ref, mod_ref, ng_ref, w_ref, qg_ref, kg_ref, seg_ref, cos_ref, sup_ref, sdn_ref,
         q_o, k_o, v_o, u_o, nq_o, nk_o, nv_o, g_o) = refs
    else:
        (x_ref, mod_ref, ng_ref, w_ref, qg_ref, kg_ref, seg_ref,
         q_o, k_o, v_o, u_o, nq_o, nk_o, nv_o, g_o) = refs
    x = x_ref[...]
    h = _rms(x, ng_ref[0:1, :]) * (1.0 + mod_ref[1:2, :]) + mod_ref[0:1, :]
    hb = h.astype(MXU_DTYPE)
    scale = HEAD_DIM ** -0.5

    def proj(lo, width):
        return jnp.dot(hb, w_ref[:, lo:lo + width], preferred_element_type=F32)

    def maybe_rope(z):
        if not rope:
            return z
        c, su, sd = cos_ref[...], sup_ref[...], sdn_ref[...]
        tiles = [_rope_tile(z[:, i * LANE:(i + 1) * LANE], c, su, sd) for i in range(z.shape[1] // LANE)]
        return tiles[0] if len(tiles) == 1 else jnp.concatenate(tiles, axis=1)

    q = maybe_rope(_head_rms(proj(_Q0, 512), seg_ref[...], qg_ref[...]))
    q_o[...] = (q * scale).astype(q_o.dtype)
    k = maybe_rope(_head_rms(proj(_K0, 128), seg_ref[0:LANE, 0:LANE], kg_ref[...]))
    k_o[...] = k.astype(k_o.dtype).reshape(k_o.shape)
    v_o[...] = proj(_V0, 128).astype(v_o.dtype).reshape(v_o.shape)
    u_o[...] = proj(_U0, 512).astype(u_o.dtype)
    nq_o[...] = (proj(_NQ0, 512) * scale).astype(nq_o.dtype)
    nk_o[...] = proj(_NK0, 512).astype(nk_o.dtype).reshape(nk_o.shape)
    nv_o[...] = proj(_NV0, 512).astype(nv_o.dtype).reshape(nv_o.shape)
    for i in range(3):
        g_o[:, i * D_MODEL:(i + 1) * D_MODEL] = _sigmoid(proj(_G0 + i * D_MODEL, D_MODEL)).astype(g_o.dtype)


def _in_projection(x2d, mod_l, mod_row, ng, w_in, qg, kg, seg, rope_tabs, seq_len, kv_dtype, stack=None):
    t = x2d.shape[0]
    tm = TOKEN_TILE
    tiles_per_seq = max(seq_len // tm, 1)
    rope = rope_tabs is not None
    row = lambda i: (i, 0)
    const = lambda i: (0, 0)
    in_specs = [
        pl.BlockSpec((tm, D_MODEL), row),
        pl.BlockSpec((None, 6, D_MODEL), lambda i: (mod_row(i, tm), 0, 0)),
        pl.BlockSpec((4, D_MODEL), const),
        pl.BlockSpec((D_MODEL, IN_WIDTH), const),
        pl.BlockSpec((1, 512), const),
        pl.BlockSpec((1, LANE), const),
        pl.BlockSpec((512, 512), const),
    ]
    args = [x2d, mod_l, ng, w_in, qg, kg, seg]
    if rope:
        in_specs += [pl.BlockSpec((tm, LANE), lambda i: (i % tiles_per_seq, 0))] * 3
        args += list(rope_tabs)
    widths = (512, 128, 128, 512, 512, 512, 512, 3 * D_MODEL)
    dtypes = (MXU_DTYPE, kv_dtype, kv_dtype, F32, MXU_DTYPE, kv_dtype, kv_dtype, MXU_DTYPE)
    out_specs = [pl.BlockSpec((tm, w), row) for w in widths]
    out_shape = [jax.ShapeDtypeStruct((t, w), dt) for w, dt in zip(widths, dtypes)]
    aliases, n_carried = {}, 0
    if stack is not None:
        layer, depth, carried = stack
        assert tm % seq_len == 0
        for o in (1, 2, 5, 6):
            out_specs[o] = pl.BlockSpec((tm // seq_len, None, seq_len, widths[o]), lambda i: (i, layer, 0, 0))
            out_shape[o] = jax.ShapeDtypeStruct((t // seq_len, depth, seq_len, widths[o]), dtypes[o])
        if carried is not None:
            n_carried = len(carried)
            aliases = {len(args) + n: o for n, o in enumerate((1, 2, 5, 6))}
            in_specs += [pl.BlockSpec(memory_space=pl.ANY)] * n_carried
            args += list(carried)
    return pl.pallas_call(
        functools.partial(_inproj_kernel, rope=rope, n_carried=n_carried),
        grid=(t // tm,),
        in_specs=in_specs,
        out_specs=out_specs,
        out_shape=out_shape,
        input_output_aliases=aliases,
        compiler_params=_params(("parallel",)),
        name="in_proj_rope" if rope else "in_proj",
    )(*args)


def _lane_masks(dtype):
    lane = lax.broadcasted_iota(jnp.int32, (1, LANE), 1)
    lo = lane < HEAD_DIM
    return lo, lo.astype(dtype), (~lo).astype(dtype)


_KEY_BLOCK = 256
_Q_SUB = 128


def _softmax_pv(qs, key_blocks, s_ref):
    macc = None
    for bi, (score_fn, _) in enumerate(key_blocks):
        sj = score_fn(qs)
        s_ref[bi] = sj
        macc = sj if macc is None else jnp.maximum(macc, sj)
    mb = jnp.broadcast_to(jnp.max(macc, axis=-1, keepdims=True), macc.shape)
    lacc = jnp.zeros(macc.shape, F32)
    o = jnp.zeros((qs.shape[0], LANE), F32)
    for bi, (_, v_fn) in enumerate(key_blocks):
        p = jnp.exp(s_ref[bi] - mb)
        lacc = lacc + p
        vb = v_fn()
        o = o + _mm(p[:, :vb.shape[0]], vb)
    return o * (1.0 / jnp.sum(lacc, axis=-1, keepdims=True))


def _attn_kernel(*refs, kv_tiles, cached):
    if cached:
        q_ref, k_ref, v_ref, kc_ref, vc_ref, o_ref, s_ref = refs
    else:
        q_ref, k_ref, v_ref, o_ref, s_ref = refs
    tq = q_ref.shape[0]
    kb = _KEY_BLOCK
    lo, m_lo, m_hi = _lane_masks(MXU_DTYPE)
    pairs_per_kv = (N_HEADS // 2) // kv_tiles
    sources = [(k_ref, v_ref)] + ([(kc_ref, vc_ref)] if cached else [])
    for q0 in range(0, tq, _Q_SUB):
        for hp in range(N_HEADS // 2):
            ksl = slice((hp // pairs_per_kv) * LANE, (hp // pairs_per_kv + 1) * LANE)
            q2 = q_ref[q0:q0 + _Q_SUB, hp * LANE:(hp + 1) * LANE]
            qs = jnp.concatenate([q2 * m_lo, q2 * m_hi], axis=0)
            blocks = [(functools.partial(lambda x, kr, off, ksl: _mm_nt(x, kr[off:off + kb, ksl]),
                                         kr=kr, off=off, ksl=ksl),
                       functools.partial(lambda vr, off, ksl: vr[off:off + kb, ksl], vr=vr, off=off, ksl=ksl))
                      for kr, vr in sources for off in range(0, kr.shape[0], kb)]
            o = _softmax_pv(qs, blocks, s_ref)
            o_ref[q0:q0 + _Q_SUB, hp * LANE:(hp + 1) * LANE] = jnp.where(
                lo, o[:_Q_SUB], o[_Q_SUB:]).astype(o_ref.dtype)


def _attention(q, k, v, kc, vc, tq, name, kv_layer=None):
    b, lq, _ = q.shape
    lk, kw = k.shape[-2], k.shape[-1]
    cached = kc is not None
    qmap = lambda bi, ti: (bi, ti, 0)
    kmap = lambda bi, ti: (bi, 0, 0)
    if kv_layer is None:
        kv_spec = pl.BlockSpec((None, lk, kw), kmap)
    else:
        kv_spec = pl.BlockSpec((None, None, lk, kw), lambda bi, ti: (bi, kv_layer, 0, 0))
    in_specs = [pl.BlockSpec((None, tq, 512), qmap), kv_spec, kv_spec]
    args = [q, k, v]
    if cached:
        lc = kc.shape[1]
        in_specs += [pl.BlockSpec((None, lc, kw), kmap)] * 2
        args += [kc, vc]
    assert lk % _KEY_BLOCK == 0 and (not cached or kc.shape[1] % _KEY_BLOCK == 0) and tq % _Q_SUB == 0
    n_blocks = (lk + (kc.shape[1] if cached else 0)) // _KEY_BLOCK
    return pl.pallas_call(
        functools.partial(_attn_kernel, kv_tiles=kw // LANE, cached=cached),
        grid=(b, lq // tq),
        in_specs=in_specs,
        out_specs=pl.BlockSpec((None, tq, 512), qmap),
        out_shape=jax.ShapeDtypeStruct((b, lq, 512), MXU_DTYPE),
        scratch_shapes=[pltpu.VMEM((n_blocks, 2 * _Q_SUB, _KEY_BLOCK), F32)],
        compiler_params=_params(("parallel", "parallel")),
        name=name,
    )(*args)


def _na_geometry(seq_len):
    rows = seq_len // GRID_W
    n_tiles = rows // 2
    assert rows >= NA_KEY_ROWS and NA_WIN_ROWS <= rows and NA_KEY_ROWS % 2 == 0
    ws = np.clip(2 * np.arange(n_tiles) - NA_WIN_ROWS // 2, 0, rows - NA_KEY_ROWS)
    r = 2 * np.arange(n_tiles)[:, None, None] + np.arange(2)[None, :, None]
    key_r = ws[:, None, None] + np.arange(NA_KEY_ROWS)[None, None, :]
    r0 = np.clip(r - NA_WIN_ROWS // 2, 0, rows - NA_WIN_ROWS)
    valid = (key_r >= r0) & (key_r < r0 + NA_WIN_ROWS)
    dr = np.where(valid, key_r - r + NA_WIN_ROWS - 1, 2 * NA_WIN_ROWS - 1)
    assert (valid.sum(-1) == NA_WIN_ROWS).all()
    return ws.astype(np.int32), dr.reshape(-1).astype(np.int32)


def _na_bias_blocks(rpb):
    h = rpb.shape[0]
    nrel = 2 * NA_WIN_ROWS - 1
    zeros = jnp.zeros((h, nrel, LANE - (2 * NA_WIN_COLS - 1)), F32)
    v = jnp.concatenate([rpb[..., NA_WIN_COLS - 1:], zeros, rpb[..., :NA_WIN_COLS - 1]], axis=-1).astype(F32)
    t = jnp.tile(v, (1, 1, GRID_W))[..., :GRID_W * (LANE - 1)].reshape(h, nrel, GRID_W, LANE - 1)[..., :GRID_W]
    c = np.arange(GRID_W)
    c0 = np.clip(c - NA_WIN_COLS // 2, 0, GRID_W - NA_WIN_COLS)
    colmask = (c[None, :] >= c0[:, None]) & (c[None, :] < c0[:, None] + NA_WIN_COLS)
    t = jnp.where(jnp.asarray(colmask)[None, None], t, NEG_BIG)
    t = jnp.concatenate([t, jnp.full((h, 1, GRID_W, GRID_W), NEG_BIG, F32)], axis=1)
    pad = jnp.zeros_like(t)
    return jnp.concatenate([t, pad], axis=-1), jnp.concatenate([pad, t], axis=-1)


def _na_kernel(ws_ref, dr_ref, q_ref, k_ref, v_ref, kc_ref, vc_ref, bl_ref, br_ref, o_ref, s_ref):
    i = pl.program_id(1)
    start = pl.multiple_of(ws_ref[i] * GRID_W, GRID_W)
    nk = NA_KEY_ROWS * GRID_W
    kb = _KEY_BLOCK
    tq = q_ref.shape[0]
    lc = kc_ref.shape[0]
    lo, m_lo, m_hi = _lane_masks(MXU_DTYPE)

    def bias_block(hp, off, width):
        rows = []
        for h in (2 * hp, 2 * hp + 1):
            for qr in range(2):
                base = (i * 2 + qr) * NA_KEY_ROWS + off // GRID_W
                tiles = [bl_ref[h, dr_ref[base + 2 * kp]] + br_ref[h, dr_ref[base + 2 * kp + 1]]
                         for kp in range(width // LANE)]
                rows.append(tiles[0] if len(tiles) == 1 else jnp.concatenate(tiles, axis=1))
        return jnp.concatenate(rows, axis=0)

    def local_scores(x, hp, sl, off, width):
        s = _mm_nt(x, k_ref[pl.ds(start + off, width), sl]) + bias_block(hp, off, width)
        if width < kb:
            s = jnp.concatenate([s, jnp.full((s.shape[0], kb - width), NEG_BIG, F32)], axis=1)
        return s

    for hp in range(N_HEADS // 2):
        sl = slice(hp * LANE, (hp + 1) * LANE)
        q2 = q_ref[:, sl]
        qs = jnp.concatenate([q2 * m_lo, q2 * m_hi], axis=0)
        blocks = []
        for off in range(0, nk, kb):
            width = min(kb, nk - off)
            blocks.append((functools.partial(local_scores, hp=hp, sl=sl, off=off, width=width),
                           functools.partial(lambda sl, off, width: v_ref[pl.ds(start + off, width), sl],
                                             sl=sl, off=off, width=width)))
        for off in range(0, lc, kb):
            blocks.append((functools.partial(lambda x, sl, off: _mm_nt(x, kc_ref[off:off + kb, sl]), sl=sl, off=off),
                           functools.partial(lambda sl, off: vc_ref[off:off + kb, sl], sl=sl, off=off)))
        o = _softmax_pv(qs, blocks, s_ref)
        o_ref[:, sl] = jnp.where(lo, o[:tq], o[tq:]).astype(o_ref.dtype)


def _neighbourhood_attention(q, k, v, kc, vc, rpb):
    b, seq_len, _ = q.shape
    lc = kc.shape[1]
    ws, dr = _na_geometry(seq_len)
    b_left, b_right = _na_bias_blocks(rpb)
    n_tiles = len(ws)
    tq = 2 * GRID_W
    qmap = lambda bi, ti, ws_r, dr_r: (bi, ti, 0)
    kmap = lambda bi, ti, ws_r, dr_r: (bi, 0, 0)
    bmap = lambda bi, ti, ws_r, dr_r: (0, 0, 0, 0)
    grid_spec = pltpu.PrefetchScalarGridSpec(
        num_scalar_prefetch=2,
        grid=(b, n_tiles),
        in_specs=[
            pl.BlockSpec((None, tq, 512), qmap),
            pl.BlockSpec((None, seq_len, 512), kmap),
            pl.BlockSpec((None, seq_len, 512), kmap),
            pl.BlockSpec((None, lc, 512), kmap),
            pl.BlockSpec((None, lc, 512), kmap),
            pl.BlockSpec(b_left.shape, bmap),
            pl.BlockSpec(b_right.shape, bmap),
        ],
        out_specs=pl.BlockSpec((None, tq, 512), qmap),
        scratch_shapes=[pltpu.VMEM((-(-NA_KEY_ROWS * GRID_W // _KEY_BLOCK) + lc // _KEY_BLOCK, 2 * tq, _KEY_BLOCK),
                                   F32)],
    )
    assert lc % _KEY_BLOCK == 0
    return pl.pallas_call(
        _na_kernel,
        grid_spec=grid_spec,
        out_shape=jax.ShapeDtypeStruct((b, seq_len, 512), MXU_DTYPE),
        compiler_params=_params(("parallel", "arbitrary")),
        name="na_attn",
    )(jnp.asarray(ws), jnp.asarray(dr), q, k, v, kc, vc, b_left, b_right)


def _cmul(ar, ai, br, bi):
    return ar * br - ai * bi, ar * bi + ai * br


def _lam_bar(lr, li, ls):
    dt = jnp.exp(ls)
    mag = jnp.exp(lr * dt)
    return mag * jnp.cos(li * dt), mag * jnp.sin(li * dt)


def _zoh_coef(lr, li, zr, zi):
    nr, ni = zr - 1.0, zi
    den = 1.0 / (lr * lr + li * li)
    return (nr * lr + ni * li) * den, (ni * lr - nr * li) * den


def _squarings(zr, zi, n):
    out = [(zr, zi)]
    for _ in range(n - 1):
        zr, zi = _cmul(zr, zi, zr, zi)
        out.append((zr, zi))
    return out


def _cpow(squares, e):
    pr, pi = jnp.ones(e.shape, F32), jnp.zeros(e.shape, F32)
    for k, (zr, zi) in enumerate(squares):
        bit = jnp.bitwise_and(jnp.right_shift(e, k), 1) == 1
        nr, ni = _cmul(pr, pi, zr, zi)
        pr, pi = jnp.where(bit, nr, pr), jnp.where(bit, ni, pi)
    return pr, pi


def _ssm_ops_kernel(lrr_ref, lir_ref, lsr_ref, lrc_ref, lic_ref, lsc_ref, bt_ref, ct_ref, s0_ref, s1_ref,
                    wt_o, wb_o, wc_o, l16_o, tg_ref):
    tc, hg, p = SSM_CHUNK, SSM_GROUP_CH, SSM_STATE
    w = tc * hg
    lane_w = lax.broadcasted_iota(jnp.int32, (1, w), 1)
    lane_p = lax.broadcasted_iota(jnp.int32, (1, 2 * p), 1)
    row_w = lax.broadcasted_iota(jnp.int32, (2 * w, 1), 0)
    row_p = lax.broadcasted_iota(jnp.int32, (2 * p, 1), 0)
    tau_of_lane = jnp.right_shift(lane_w, 4)
    gl_of_lane = jnp.right_shift(lane_p, 6)
    step_of_row = jnp.right_shift(row_w, 5)
    same_group = jnp.bitwise_and(jnp.right_shift(row_w, 4), 1) == gl_of_lane
    first_rows = row_p < p
    tg_ref[...] = jnp.zeros_like(tg_ref)
    l16_rows = []
    for d in range(2):
        lr, li = lrr_ref[d], lir_ref[d]
        zr, zi = _lam_bar(lr, li, lsr_ref[d])
        cfr, cfi = _zoh_coef(lr, li, zr, zi)
        btr, bti = _cmul(cfr, cfi, bt_ref[d, 0], bt_ref[d, 1])
        sq = _squarings(zr, zi, 5)
        e_inj = (tc - 1 - step_of_row) if d == 0 else step_of_row
        pr, pi = _cpow(sq[:4], jnp.broadcast_to(e_inj, (2 * w, 2 * p)))
        ir, ii = _cmul(pr, pi, jnp.tile(btr, (2 * tc, 1)), jnp.tile(bti, (2 * tc, 1)))
        wb_o[:, (2 * d) * LANE:(2 * d + 1) * LANE] = jnp.where(same_group, ir, 0.0).astype(wb_o.dtype)
        wb_o[:, (2 * d + 1) * LANE:(2 * d + 2) * LANE] = jnp.where(same_group, ii, 0.0).astype(wb_o.dtype)
        l16_rows += [sq[4][0], sq[4][1]]
        lrc, lic = lrc_ref[d], lic_ref[d]
        zcr, zci = _lam_bar(lrc, lic, lsc_ref[d])
        tau = tau_of_lane if d == 0 else (tc - 1) - tau_of_lane
        pr, pi = _cpow(_squarings(zcr, zci, 4), jnp.broadcast_to(tau, (2 * p, w)))
        c0r, c0i = _cmul(ct_ref[d, 0], ct_ref[d, 1], pr, pi)
        c1r, c1i = _cmul(c0r, c0i, zcr, zci)
        for r, val in ((2 * d, c1r), (2 * d + 1, -c1i)):
            vb = val.astype(MXU_DTYPE)
            spread = jnp.where(first_rows, jnp.dot(vb, s0_ref[...], preferred_element_type=F32),
                               jnp.dot(vb, s1_ref[...], preferred_element_type=F32))
            wc_o[r * LANE:(r + 1) * LANE, :] = spread.astype(wc_o.dtype)
        for gl in range(2):
            in_group = gl_of_lane == gl
            kt = (jnp.dot(jnp.where(in_group, btr, 0.0), c0r, preferred_element_type=F32,
                          precision=lax.Precision.HIGHEST)
                  - jnp.dot(jnp.where(in_group, bti, 0.0), c0i, preferred_element_type=F32,
                            precision=lax.Precision.HIGHEST))
            for s in range(tc):
                if d == 0:
                    shift, keep = hg * s, lane_w >= hg * s
                else:
                    shift, keep = (w - hg * (tc - 1 - s)) % w, lane_w < hg * (s + 1)
                rolled = kt if shift == 0 else pltpu.roll(kt, shift, 1)
                tg_ref[gl, s * hg:(s + 1) * hg, :] += jnp.where(keep, rolled, 0.0)
    for gl, s_ref in enumerate((s0_ref, s1_ref)):
        spread = jnp.dot(tg_ref[gl].astype(MXU_DTYPE), s_ref[...], preferred_element_type=F32)
        for s in range(tc):
            wt_o[(2 * s + gl) * hg:(2 * s + gl + 1) * hg, :] = spread[s * hg:(s + 1) * hg].astype(wt_o.dtype)
    l16_o[...] = jnp.concatenate(l16_rows + [jnp.zeros((4, 2 * p), F32)], axis=0)


def _ssm_operators(lam_re, lam_im, log_step, b_re, b_im, c_re, c_im):
    depth = lam_re.shape[0]
    p, hg, tc, q = SSM_STATE, SSM_GROUP_CH, SSM_CHUNK, SSM_PAIRS
    assert (hg, p, tc) == (16, 64, 16), "lane/row index arithmetic in the kernel uses these as shifts"
    w = tc * hg

    def per_pair(a, tail):
        a = a.astype(F32).reshape((depth, 2, q, 2) + tail)
        return jnp.transpose(a, (0, 2, 1, 3) + tuple(range(4, 4 + len(tail))))

    lam_r, lam_i = per_pair(lam_re, (p,)), per_pair(lam_im, (p,))
    ls = jnp.broadcast_to(per_pair(log_step, ())[..., None], lam_r.shape)
    rows = [a.reshape(depth, q, 2, 1, 2 * p) for a in (lam_r, lam_i, ls)]
    cols = [a.reshape(depth, q, 2, 2 * p, 1) for a in (lam_r, lam_i, ls)]
    bt = jnp.stack([per_pair(b_re, (p, hg)), per_pair(b_im, (p, hg))], axis=3)
    bt = jnp.transpose(bt, (0, 1, 2, 3, 6, 4, 5)).reshape(depth, q, 2, 2, hg, 2 * p)
    ct = jnp.stack([per_pair(c_re, (hg, p)), per_pair(c_im, (hg, p))], axis=3)
    ct = jnp.transpose(ct, (0, 1, 2, 3, 4, 6, 5)).reshape(depth, q, 2, 2, 2 * p, hg)
    ct = jnp.tile(ct, (1, 1, 1, 1, 1, tc))
    r, c = np.arange(w)[:, None], np.arange(2 * w)[None, :]
    hit = (r // hg == c // (2 * hg)) & (r % hg == c % hg)
    spread = [jnp.asarray(hit & ((c // hg) % 2 == gl), MXU_DTYPE) for gl in range(2)]

    blk = lambda a: pl.BlockSpec((None, None) + a.shape[2:], lambda l, i: (l, i) + (0,) * (a.ndim - 2))
    const = pl.BlockSpec((w, 2 * w), lambda l, i: (0, 0))
    mat = pl.BlockSpec((None, None, 2 * w, 2 * w), lambda l, i: (l, i, 0, 0))
    args = rows + cols + [bt, ct]
    return pl.pallas_call(
        _ssm_ops_kernel,
        grid=(depth, q),
        in_specs=[blk(a) for a in args] + [const, const],
        out_specs=[mat, mat, mat, pl.BlockSpec((None, None, 8, 2 * p), lambda l, i: (l, i, 0, 0))],
        out_shape=[jax.ShapeDtypeStruct((depth, q, 2 * w, 2 * w), MXU_DTYPE)] * 3
        + [jax.ShapeDtypeStruct((depth, q, 8, 2 * p), F32)],
        scratch_shapes=[pltpu.VMEM((2, w, w), F32)],
        compiler_params=_params(("parallel", "parallel")),
        name="ssm_ops",
    )(*args, *spread)


_PAIRS_PER_TILE = 4
_SSM_ROW_BLOCK = 64
_SCAN_ROW_PAD = 8


def _ssm_kernel(zu_ref, wt_ref, wb_ref, wc_ref, l16_ref, h0_ref, y_o, fin_o, u_ref, yp_ref, *state_refs,
                batch, seq_len):
    tc, npair = SSM_CHUNK, _PAIRS_PER_TILE
    n_chunks = seq_len // tc
    nrows = batch * n_chunks
    rb = _SSM_ROW_BLOCK
    rs = n_chunks + _SCAN_ROW_PAD
    slot_w = LANE // npair
    slot = jnp.right_shift(lax.broadcasted_iota(jnp.int32, (1, LANE), 1), 5)
    dx_refs, xs_refs = state_refs[:4], state_refs[4:]

    def place(pieces, src_slot):
        offset = src_slot
        out = None
        for j, piece in enumerate(pieces):
            shift = (slot_w * (j - offset[j])) % LANE
            r = piece if shift == 0 else pltpu.roll(piece, shift, 1)
            out = r if out is None else jnp.where(slot == j, r, out)
        return out

    def gather_block(i, carry):
        r0 = pl.multiple_of(i * rb, rb)
        steps = [zu_ref[pl.ds(r0 * tc + s, rb, stride=tc), :] for s in range(tc)]
        for p in range(npair):
            tiles = [place(steps[4 * k:4 * k + 4], [p] * 4) for k in range(tc // 4)]
            u_ref[p, pl.ds(r0, rb), :] = jnp.concatenate(tiles, axis=1).astype(u_ref.dtype)
        return carry

    lax.fori_loop(0, nrows // rb, gather_block, 0)

    for p in range(npair):
        u = u_ref[p]
        y_intra = jnp.dot(u, wt_ref[p], preferred_element_type=F32)
        dx = jnp.dot(u, wb_ref[p], preferred_element_type=F32)
        for r in range(4):
            for b in range(batch):
                dx_refs[r][b * rs:b * rs + n_chunks, :] = dx[b * n_chunks:(b + 1) * n_chunks, r * LANE:(r + 1) * LANE]
        lfr, lfi, lbr, lbi = (l16_ref[p, r:r + 1, :] for r in range(4))

        def body(c, carry):
            fr, fi, br, bi = carry
            fwd = pl.ds(c, batch, stride=rs)
            bwd = pl.ds(n_chunks - 1 - c, batch, stride=rs)
            xs_refs[0][fwd, :] = fr
            xs_refs[1][fwd, :] = fi
            xs_refs[2][bwd, :] = br
            xs_refs[3][bwd, :] = bi
            nfr = lfr * fr - lfi * fi + dx_refs[0][fwd, :]
            nfi = lfr * fi + lfi * fr + dx_refs[1][fwd, :]
            nbr = lbr * br - lbi * bi + dx_refs[2][bwd, :]
            nbi = lbr * bi + lbi * br + dx_refs[3][bwd, :]
            return nfr, nfi, nbr, nbi

        fin = lax.fori_loop(0, n_chunks, body, tuple(h0_ref[p, :, r * LANE:(r + 1) * LANE] for r in range(4)),
                            unroll=4)
        for r in range(4):
            fin_o[p, :, r * LANE:(r + 1) * LANE] = fin[r]
        xs = jnp.concatenate(
            [jnp.concatenate([x[b * rs:b * rs + n_chunks, :] for b in range(batch)], axis=0) for x in xs_refs],
            axis=1).astype(MXU_DTYPE)
        yp_ref[p] = y_intra + jnp.dot(xs, wc_ref[p], preferred_element_type=F32)

    def scatter_block(i, carry):
        r0 = pl.multiple_of(i * rb, rb)
        for k in range(tc // 4):
            pieces = [yp_ref[p, pl.ds(r0, rb), k * LANE:(k + 1) * LANE] for p in range(npair)]
            for j in range(4):
                y_o[pl.ds(r0 * tc + 4 * k + j, rb, stride=tc), :] = place(pieces, [j] * npair)
        return carry

    lax.fori_loop(0, nrows // rb, scatter_block, 0)


def _ssm_scan(zu2d, w_t, w_b, w_c, l16, h0, batch, seq_len):
    t = zu2d.shape[0]
    nrows = t // SSM_CHUNK
    npair = _PAIRS_PER_TILE
    assert nrows % _SSM_ROW_BLOCK == 0 and seq_len % SSM_CHUNK == 0
    once = pl.Buffered(1)
    wspec = pl.BlockSpec((npair, 512, 512), lambda i: (i, 0, 0))
    return pl.pallas_call(
        functools.partial(_ssm_kernel, batch=batch, seq_len=seq_len),
        grid=(SSM_PAIRS // npair,),
        in_specs=[
            pl.BlockSpec((t, LANE), lambda i: (0, i), pipeline_mode=once),
            wspec, wspec, wspec,
            pl.BlockSpec((npair, 8, LANE), lambda i: (i, 0, 0)),
            pl.BlockSpec((npair, batch, 512), lambda i: (i, 0, 0)),
        ],
        out_specs=[
            pl.BlockSpec((t, LANE), lambda i: (0, i), pipeline_mode=once),
            pl.BlockSpec((npair, batch, 512), lambda i: (i, 0, 0)),
        ],
        out_shape=[jax.ShapeDtypeStruct((t, SSM_WIDTH), F32),
                   jax.ShapeDtypeStruct((SSM_PAIRS, batch, 512), F32)],
        scratch_shapes=[pltpu.VMEM((npair, nrows, 512), MXU_DTYPE), pltpu.VMEM((npair, nrows, 512), F32)]
        + [pltpu.VMEM((batch * (seq_len // SSM_CHUNK + _SCAN_ROW_PAD), LANE), F32)] * 8,
        compiler_params=_params(("arbitrary",)),
        name="ssm_scan",
    )(zu2d, w_t, w_b, w_c, l16, h0)


def _pack_state(s_re, s_im):
    b = s_re.shape[0]
    a = jnp.stack([s_re, s_im], axis=2).reshape(b, 2, 2, SSM_PAIRS, 2 * SSM_STATE)
    return jnp.transpose(a, (3, 0, 1, 2, 4)).reshape(SSM_PAIRS, b, 4 * 2 * SSM_STATE).astype(F32)


def _unpack_state(fin):
    b = fin.shape[1]
    a = jnp.transpose(fin.reshape(SSM_PAIRS, b, 2, 2, 2 * SSM_STATE), (1, 2, 3, 0, 4))
    a = a.reshape(b, 2, 2, SSM_GROUPS, SSM_STATE)
    return a[:, :, 0], a[:, :, 1]


def _merge_kernel(ya_ref, ys_ref, u_ref, yc_ref, g_ref, x_ref, mod_ref, ng_ref, d_ref,
                  wglu_ref, wa_ref, wb_ref, wc_ref, wo_ref, x_o, h_o):
    y = ys_ref[...].astype(F32) + d_ref[...] * u_ref[...]
    gl = _gelu_tanh(y)
    yb = gl * _sigmoid(_mm(gl, wglu_ref[...]))
    merged = (g_ref[:, 0:D_MODEL].astype(F32) * _mm(ya_ref[...], wa_ref[...])
              + g_ref[:, D_MODEL:2 * D_MODEL].astype(F32) * _mm(yb, wb_ref[...])
              + g_ref[:, 2 * D_MODEL:3 * D_MODEL].astype(F32) * _mm(yc_ref[...], wc_ref[...]))
    x1 = x_ref[...] + mod_ref[2:3, :] * _rms(_mm(merged, wo_ref[...]), ng_ref[1:2, :])
    x_o[...] = x1
    h_o[...] = (_rms(x1, ng_ref[2:3, :]) * (1.0 + mod_ref[4:5, :]) + mod_ref[3:4, :]).astype(h_o.dtype)


def _merge(ya, ys, zu, yc, gates, x2d, mod_l, mod_row, ng, ssm_d, w_glu, w_a, w_b, w_c, w_o):
    t = x2d.shape[0]
    tm = TOKEN_TILE
    row = lambda i: (i, 0)
    const = lambda i: (0, 0)
    r512 = pl.BlockSpec((tm, 512), row)
    wbr = pl.BlockSpec((512, D_MODEL), const)
    return pl.pallas_call(
        _merge_kernel,
        grid=(t // tm,),
        in_specs=[
            r512, r512, r512, r512,
            pl.BlockSpec((tm, 3 * D_MODEL), row),
            pl.BlockSpec((tm, D_MODEL), row),
            pl.BlockSpec((None, 6, D_MODEL), lambda i: (mod_row(i, tm), 0, 0)),
            pl.BlockSpec((4, D_MODEL), const),
            pl.BlockSpec((1, 512), const),
            pl.BlockSpec((512, 512), const),
            wbr, wbr, wbr,
            pl.BlockSpec((D_MODEL, D_MODEL), const),
        ],
        out_specs=[pl.BlockSpec((tm, D_MODEL), row), pl.BlockSpec((tm, D_MODEL), row)],
        out_shape=[jax.ShapeDtypeStruct((t, D_MODEL), F32), jax.ShapeDtypeStruct((t, D_MODEL), MXU_DTYPE)],
        compiler_params=_params(("parallel",)),
        name="merge",
    )(ya, ys, zu, yc, gates, x2d, mod_l, ng, ssm_d, w_glu, w_a, w_b, w_c, w_o)


def _ffn_kernel(*refs, seq_len, first, last):
    refs = list(refs)
    h_ref, hp_ref, hn_ref = refs[:3]
    del refs[:3]
    part_ref = None if first else refs.pop(0)
    if last:
        x_ref, mod_ref, ng_ref = refs[:3]
        del refs[:3]
    wa_ref, wg_ref, cwa_ref, cwg_ref, cba_ref, cbg_ref, wd_ref, out_ref, ua_ref, ug_ref, act_ref = refs
    i = pl.program_id(0)
    tm = h_ref.shape[0]
    ft = wd_ref.shape[0]
    n = tm + 16
    start = jnp.bitwise_and(i * tm, seq_len - 1)
    keep_prev = (start != 0).astype(F32)
    keep_next = (jnp.bitwise_and(start + tm, seq_len - 1) != 0).astype(F32)
    hh = jnp.concatenate([(hp_ref[...] * keep_prev).astype(h_ref.dtype), h_ref[...],
                          (hn_ref[...] * keep_next).astype(h_ref.dtype)], axis=0)
    ua_ref[...] = jnp.dot(hh, wa_ref[...], preferred_element_type=F32)
    ug_ref[...] = jnp.dot(hh, wg_ref[...], preferred_element_type=F32)
    interior = tm > seq_len
    if interior:
        pos = jnp.bitwise_and(lax.broadcasted_iota(jnp.int32, (tm, 1), 0), seq_len - 1)
        has_prev = (pos != 0).astype(F32)
        has_next = (pos != seq_len - 1).astype(F32)

    def conv(u_ref, cw_ref, cb_ref, lo):
        uc = u_ref[:, lo:lo + LANE]
        up = pltpu.roll(uc, 1, 0)[8:8 + tm]
        un = pltpu.roll(uc, n - 1, 0)[8:8 + tm]
        if interior:
            up, un = up * has_prev, un * has_next
        return (cw_ref[0:1, lo:lo + LANE] * up + cw_ref[1:2, lo:lo + LANE] * uc[8:8 + tm]
                + cw_ref[2:3, lo:lo + LANE] * un + cb_ref[0:1, lo:lo + LANE])

    for kc in range(ft // LANE):
        a = conv(ua_ref, cwa_ref, cba_ref, kc * LANE)
        g = conv(ug_ref, cwg_ref, cbg_ref, kc * LANE)
        act_ref[:, kc * LANE:(kc + 1) * LANE] = (g * _sigmoid(g) * a).astype(act_ref.dtype)
    total = jnp.dot(act_ref[...], wd_ref[...], preferred_element_type=F32)
    if not first:
        total = total + part_ref[...]
    if last:
        out_ref[...] = x_ref[...] + mod_ref[5:6, :] * _rms(total, ng_ref[3:4, :])
    else:
        out_ref[...] = total


def _conv_ffn(h2, x1, mod_l, mod_row, ng, w_up, conv_w, conv_b, w_down, seq_len):
    t = x1.shape[0]
    tm = FFN_TOKEN_TILE
    ft = FF_TILE
    nf = D_FF // ft
    nblk8 = t // 8
    assert seq_len & (seq_len - 1) == 0 and (seq_len % tm == 0 or tm % seq_len == 0) and t % tm == 0
    row = pl.BlockSpec((tm, D_MODEL), lambda i: (i, 0))
    once = pl.Buffered(1)
    part = None
    for j in range(nf):
        first, last = j == 0, j == nf - 1
        in_specs = [
            row,
            pl.BlockSpec((8, D_MODEL), lambda i: (jnp.maximum(i * (tm // 8) - 1, 0), 0)),
            pl.BlockSpec((8, D_MODEL), lambda i: (jnp.minimum((i + 1) * (tm // 8), nblk8 - 1), 0)),
        ]
        args = [h2, h2, h2]
        if not first:
            in_specs.append(row)
            args.append(part)
        if last:
            in_specs += [row, pl.BlockSpec((None, 6, D_MODEL), lambda i: (mod_row(i, tm), 0, 0)),
                         pl.BlockSpec((4, D_MODEL), lambda i: (0, 0))]
            args += [x1, mod_l, ng]
        in_specs += [
            pl.BlockSpec((D_MODEL, ft), lambda i, j=j: (0, j), pipeline_mode=once),
            pl.BlockSpec((D_MODEL, ft), lambda i, j=j: (0, nf + j), pipeline_mode=once),
            pl.BlockSpec((3, ft), lambda i, j=j: (0, j)),
            pl.BlockSpec((3, ft), lambda i, j=j: (0, nf + j)),
            pl.BlockSpec((1, ft), lambda i, j=j: (0, j)),
            pl.BlockSpec((1, ft), lambda i, j=j: (0, nf + j)),
            pl.BlockSpec((ft, D_MODEL), lambda i, j=j: (j, 0), pipeline_mode=once),
        ]
        args += [w_up, w_up, conv_w, conv_w, conv_b, conv_b, w_down]
        part = pl.pallas_call(
            functools.partial(_ffn_kernel, seq_len=seq_len, first=first, last=last),
            grid=(t // tm,),
            in_specs=in_specs,
            out_specs=row,
            out_shape=jax.ShapeDtypeStruct((t, D_MODEL), F32),
            scratch_shapes=[pltpu.VMEM((tm + 16, ft), F32), pltpu.VMEM((tm + 16, ft), F32),
                            pltpu.VMEM((tm, ft), MXU_DTYPE)],
            compiler_params=_params(("parallel",)),
            name="conv_ffn_last" if last else "conv_ffn_part",
        )(*args)
    return part


_Q_HEAD_ORDER = (0, 4, 1, 5, 2, 6, 3, 7)


def _rope_tables(seq_len):
    nf = HEAD_DIM // 4
    t = np.arange(seq_len)
    pos = np.stack([t // GRID_W, t % GRID_W]).astype(np.float32)
    inv = jnp.asarray(ROPE_THETA, F32) ** (-jnp.arange(nf, dtype=F32) / nf)
    ang = jnp.asarray(pos)[:, :, None] * inv
    d = np.arange(HEAD_DIM)
    ang = ang[d // (2 * nf), :, d % nf].T
    second = jnp.asarray(((d % (2 * nf)) // nf) == 1)[None, :]
    cos, sin = jnp.cos(ang), jnp.sin(ang)
    tabs = (cos, jnp.where(second, 0.0, -sin), jnp.where(second, sin, 0.0))
    return tuple(jnp.tile(x, (1, LANE // HEAD_DIM)).astype(F32) for x in tabs)


def _layer_weights(w_in, qk_g, w_br_a):
    hd = HEAD_DIM
    w_in_p = jnp.concatenate([w_in[:, h * hd:(h + 1) * hd] for h in _Q_HEAD_ORDER] + [w_in[:, 512:]],
                             axis=1).astype(MXU_DTYPE)
    w_a_p = jnp.concatenate([w_br_a[h * hd:(h + 1) * hd] for h in _Q_HEAD_ORDER], axis=0).astype(MXU_DTYPE)
    qg = jnp.tile(qk_g[0], N_HEADS).reshape(1, 512).astype(F32)
    kg = jnp.tile(qk_g[1], GA_KV_HEADS).reshape(1, LANE).astype(F32)
    return w_in_p, w_a_p, qg, kg


def kernel(x_prompt, x_sample, c, cache_ga_k, cache_ga_v, cache_na_k, cache_na_v, state_ssm_re, state_ssm_im,
           c_ctx, w_mod, b_mod, norm_g, w_in, qk_norm_g, na_rpb, ssm_lam_re, ssm_lam_im, ssm_log_step,
           ssm_b_re, ssm_b_im, ssm_c_re, ssm_c_im, ssm_d, w_glu, w_br_a, w_br_b, w_br_c, w_out,
           w_up, conv_w, conv_b, w_down):
    depth = w_in.shape[0]
    bp, lp, _ = x_prompt.shape
    bs, ls, _ = x_sample.shape
    lc = cache_ga_k.shape[2]
    assert lp % 256 == 0 and ls % FFN_TOKEN_TILE == 0 and (bp * lp) % FFN_TOKEN_TILE == 0
    assert FFN_TOKEN_TILE % TOKEN_TILE == 0
    assert bs % 8 == 0 and bp % 8 == 0, "the scan keeps one batch row per sublane"

    rows = 1 + bs
    rows_p = -(-rows // 8) * 8
    cvec = jnp.concatenate([c_ctx[None], c, jnp.zeros((rows_p - rows, D_MODEL), F32)], axis=0)
    mod = _modulation(cvec, w_mod, b_mod).reshape(depth, rows_p, 6, D_MODEL)

    w_t, w_b, w_c, l16 = _ssm_operators(ssm_lam_re, ssm_lam_im, ssm_log_step, ssm_b_re, ssm_b_im,
                                        ssm_c_re, ssm_c_im)
    seg = jnp.asarray(np.kron(np.eye(N_HEADS), np.full((HEAD_DIM, HEAD_DIM), 1.0 / HEAD_DIM)), MXU_DTYPE)
    rope_tabs = _rope_tables(ls)
    ctx_row = lambda i, tm: 0
    lat_row = lambda i, tm: 1 + (i * tm) // ls

    y_p = x_prompt.reshape(bp * lp, D_MODEL)
    y_s = x_sample.reshape(bs * ls, D_MODEL)
    zero_state = jnp.zeros((SSM_PAIRS, bp, 512), F32)
    states = ([], [])
    kv_stacks = None
    for l in range(depth):
        w_in_p, w_a_p, qg, kg = _layer_weights(w_in[l], qk_norm_g[l], w_br_a[l])
        w_glu_l, w_b_l, w_c_l, w_o_l = (a[l].astype(MXU_DTYPE) for a in (w_glu, w_br_b, w_br_c, w_out))
        ffn_w = (w_up[l].astype(MXU_DTYPE), conv_w[l].astype(F32), conv_b[l].reshape(1, 2 * D_FF).astype(F32),
                 w_down[l].astype(MXU_DTYPE))
        d_l = ssm_d[l].reshape(1, SSM_WIDTH).astype(F32)
        ng = norm_g[l].astype(F32)
        ssm_ops = (w_t[l], w_b[l], w_c[l], l16[l])

        q, k, v, zu, nq, nk, nv, gates = _in_projection(
            y_p, mod[l], ctx_row, ng, w_in_p, qg, kg, seg, None, lp, F32, stack=(l, depth, kv_stacks))
        kv_stacks = (k, v, nk, nv)
        r3 = lambda a: a.reshape(bp, lp, a.shape[-1])
        ya = _attention(r3(q), k, v, None, None, lp, "ga_ctx", kv_layer=l)
        yc = _attention(r3(nq), nk, nv, None, None, lp, "na_ctx", kv_layer=l)
        ys, fin = _ssm_scan(zu, *ssm_ops, zero_state, bp, lp)
        x1, h2 = _merge(ya.reshape(-1, 512), ys, zu, yc.reshape(-1, 512), gates, y_p,
                        mod[l], ctx_row, ng, d_l, w_glu_l, w_a_p, w_b_l, w_c_l, w_o_l)
        y_p = _conv_ffn(h2, x1, mod[l], ctx_row, ng, *ffn_w, lp)
        f_re, f_im = _unpack_state(fin)
        states[0].append(f_re)
        states[1].append(f_im)

        q, k, v, zu, nq, nk, nv, gates = _in_projection(
            y_s, mod[l], lat_row, ng, w_in_p, qg, kg, seg, rope_tabs, ls, MXU_DTYPE)
        r3 = lambda a: a.reshape(bs, ls, a.shape[-1])
        ck = cache_ga_k[:, l].reshape(bs, lc, LANE)
        cv = cache_ga_v[:, l].reshape(bs, lc, LANE)
        ya = _attention(r3(q), r3(k), r3(v), ck, cv, 2 * GRID_W, "ga_lat")
        nck = cache_na_k[:, l].reshape(bs, lc, 512)
        ncv = cache_na_v[:, l].reshape(bs, lc, 512)
        yc = _neighbourhood_attention(r3(nq), r3(nk), r3(nv), nck, ncv, na_rpb[l])
        h0 = _pack_state(state_ssm_re[:, l], state_ssm_im[:, l])
        ys, _ = _ssm_scan(zu, *ssm_ops, h0, bs, ls)
        x1, h2 = _merge(ya.reshape(-1, 512), ys, zu, yc.reshape(-1, 512), gates, y_s,
                        mod[l], lat_row, ng, d_l, w_glu_l, w_a_p, w_b_l, w_c_l, w_o_l)
        y_s = _conv_ffn(h2, x1, mod[l], lat_row, ng, *ffn_w, ls)

    k, v, nk, nv = kv_stacks
    return (y_p.reshape(bp, lp, D_MODEL), y_s.reshape(bs, ls, D_MODEL),
            k.reshape(bp, depth, lp, GA_KV_HEADS, HEAD_DIM), v.reshape(bp, depth, lp, GA_KV_HEADS, HEAD_DIM),
            nk.reshape(bp, depth, lp, N_HEADS, HEAD_DIM), nv.reshape(bp, depth, lp, N_HEADS, HEAD_DIM),
            jnp.stack(states[0], axis=1), jnp.stack(states[1], axis=1))
```

```python
import functools
import math

import numpy as np
import jax
import jax.numpy as jnp
from jax import lax
from jax.experimental import pallas as pl
from jax.experimental.pallas import tpu as pltpu

F32 = jnp.float32
MXU_DTYPE = jnp.bfloat16

D_MODEL = 1024
HEAD_DIM = 64
N_HEADS = 8
GA_KV_HEADS = 2
GRID_W = 64
NA_WIN_ROWS = 8
NA_WIN_COLS = 16
NA_KEY_ROWS = 10
SSM_WIDTH = 512
SSM_GROUPS = 32
SSM_GROUP_CH = 16
SSM_STATE = 64
SSM_CHUNK = 16
SSM_PAIRS = SSM_GROUPS // 2
D_FF = 2816
FF_TILE = 1408
ROPE_THETA = 10000.0
EPS = 1e-6
IN_WIDTH = 5888
NEG_BIG = -1e30

LANE = 128
TOKEN_TILE = 512
FFN_TOKEN_TILE = 1024
VMEM_LIMIT = 56 * 1024 * 1024

_Q0, _K0, _V0, _U0, _NQ0, _NK0, _NV0, _G0 = 0, 512, 640, 768, 1280, 1792, 2304, 2816


def _sigmoid(x):
    return 1.0 / (1.0 + jnp.exp(-x))


def _gelu_tanh(x):
    return 0.5 * x * (1.0 + jnp.tanh(math.sqrt(2.0 / math.pi) * (x + 0.044715 * (x * x * x))))


def _rms(x, g):
    ms = jnp.mean(x * x, axis=-1, keepdims=True)
    return (x * lax.rsqrt(ms + EPS)) * g


def _mm(a, b):
    return jnp.dot(a.astype(MXU_DTYPE), b.astype(MXU_DTYPE), preferred_element_type=F32)


def _mm_nt(a, b):
    return lax.dot_general(a.astype(MXU_DTYPE), b.astype(MXU_DTYPE), (((1,), (1,)), ((), ())),
                           preferred_element_type=F32)


def _params(sem):
    return pltpu.CompilerParams(dimension_semantics=sem, vmem_limit_bytes=VMEM_LIMIT)


def _mod_kernel(c_ref, w_ref, b_ref, o_ref):
    c = c_ref[...]
    o_ref[...] = _mm(c * _sigmoid(c), w_ref[...]) + b_ref[...]


def _modulation(cvec, w_mod, b_mod):
    depth = w_mod.shape[0]
    rows = cvec.shape[0]
    tn = 1536
    return pl.pallas_call(
        _mod_kernel,
        grid=(depth, 6 * D_MODEL // tn),
        in_specs=[
            pl.BlockSpec((rows, D_MODEL), lambda l, j: (0, 0)),
            pl.BlockSpec((None, D_MODEL, tn), lambda l, j: (l, 0, j)),
            pl.BlockSpec((None, 1, tn), lambda l, j: (l, 0, j)),
        ],
        out_specs=pl.BlockSpec((None, rows, tn), lambda l, j: (l, 0, j)),
        out_shape=jax.ShapeDtypeStruct((depth, rows, 6 * D_MODEL), F32),
        compiler_params=_params(("parallel", "parallel")),
        name="adaln_mod",
    )(cvec, w_mod, b_mod.reshape(depth, 1, 6 * D_MODEL))


def _head_rms(z, seg, gain):
    ms = jnp.dot((z * z).astype(MXU_DTYPE), seg, preferred_element_type=F32)
    return (z * lax.rsqrt(ms + EPS)) * gain


def _rope_tile(t, c, s_up, s_dn):
    return t * c + pltpu.roll(t, LANE - 16, 1) * s_up + pltpu.roll(t, 16, 1) * s_dn


def _inproj_kernel(*refs, rope, n_carried):
    n_in = 10 if rope else 7
    refs = refs[:n_in] + refs[n_in + n_carried:]
    if rope:
        (x_ref, mod_ref, ng_ref, w_ref, qg_ref, kg_ref, seg_ref, cos_ref, sup_ref, sdn_ref,
         q_o, k_o, v_o, u_o, nq_o, nk_o, nv_o, g_o) = refs
    else:
        (x_ref, mod_ref, ng_ref, w_ref, qg_ref, kg_ref, seg_ref,
         q_o, k_o, v_o, u_o, nq_o, nk_o, nv_o, g_o) = refs
    x = x_ref[...]
    h = _rms(x, ng_ref[0:1, :]) * (1.0 + mod_ref[1:2, :]) + mod_ref[0:1, :]
    hb = h.astype(MXU_DTYPE)
    scale = HEAD_DIM ** -0.5

    def proj(lo, width):
        return jnp.dot(hb, w_ref[:, lo:lo + width], preferred_element_type=F32)

    def maybe_rope(z):
        if not rope:
            return z
        c, su, sd = cos_ref[...], sup_ref[...], sdn_ref[...]
        tiles = [_rope_tile(z[:, i * LANE:(i + 1) * LANE], c, su, sd) for i in range(z.shape[1] // LANE)]
        return tiles[0] if len(tiles) == 1 else jnp.concatenate(tiles, axis=1)

    q = maybe_rope(_head_rms(proj(_Q0, 512), seg_ref[...], qg_ref[...]))
    q_o[...] = (q * scale).astype(q_o.dtype)
    k = maybe_rope(_head_rms(proj(_K0, 128), seg_ref[0:LANE, 0:LANE], kg_ref[...]))
    k_o[...] = k.astype(k_o.dtype).reshape(k_o.shape)
    v_o[...] = proj(_V0, 128).astype(v_o.dtype).reshape(v_o.shape)
    u_o[...] = proj(_U0, 512).astype(u_o.dtype)
    nq_o[...] = (proj(_NQ0, 512) * scale).astype(nq_o.dtype)
    nk_o[...] = proj(_NK0, 512).astype(nk_o.dtype).reshape(nk_o.shape)
    nv_o[...] = proj(_NV0, 512).astype(nv_o.dtype).reshape(nv_o.shape)
    for i in range(3):
        g_o[:, i * D_MODEL:(i + 1) * D_MODEL] = _sigmoid(proj(_G0 + i * D_MODEL, D_MODEL)).astype(g_o.dtype)


def _in_projection(x2d, mod_l, mod_row, ng, w_in, qg, kg, seg, rope_tabs, seq_len, kv_dtype, stack=None):
    t = x2d.shape[0]
    tm = TOKEN_TILE
    tiles_per_seq = max(seq_len // tm, 1)
    rope = rope_tabs is not None
    row = lambda i: (i, 0)
    const = lambda i: (0, 0)
    in_specs = [
        pl.BlockSpec((tm, D_MODEL), row),
        pl.BlockSpec((None, 6, D_MODEL), lambda i: (mod_row(i, tm), 0, 0)),
        pl.BlockSpec((4, D_MODEL), const),
        pl.BlockSpec((D_MODEL, IN_WIDTH), const),
        pl.BlockSpec((1, 512), const),
        pl.BlockSpec((1, LANE), const),
        pl.BlockSpec((512, 512), const),
    ]
    args = [x2d, mod_l, ng, w_in, qg, kg, seg]
    if rope:
        in_specs += [pl.BlockSpec((tm, LANE), lambda i: (i % tiles_per_seq, 0))] * 3
        args += list(rope_tabs)
    widths = (512, 128, 128, 512, 512, 512, 512, 3 * D_MODEL)
    dtypes = (MXU_DTYPE, kv_dtype, kv_dtype, F32, MXU_DTYPE, kv_dtype, kv_dtype, MXU_DTYPE)
    out_specs = [pl.BlockSpec((tm, w), row) for w in widths]
    out_shape = [jax.ShapeDtypeStruct((t, w), dt) for w, dt in zip(widths, dtypes)]
    aliases, n_carried = {}, 0
    if stack is not None:
        layer, depth, carried = stack
        assert tm % seq_len == 0
        for o in (1, 2, 5, 6):
            out_specs[o] = pl.BlockSpec((tm // seq_len, None, seq_len, widths[o]), lambda i: (i, layer, 0, 0))
            out_shape[o] = jax.ShapeDtypeStruct((t // seq_len, depth, seq_len, widths[o]), dtypes[o])
        if carried is not None:
            n_carried = len(carried)
            aliases = {len(args) + n: o for n, o in enumerate((1, 2, 5, 6))}
            in_specs += [pl.BlockSpec(memory_space=pl.ANY)] * n_carried
            args += list(carried)
    return pl.pallas_call(
        functools.partial(_inproj_kernel, rope=rope, n_carried=n_carried),
        grid=(t // tm,),
        in_specs=in_specs,
        out_specs=out_specs,
        out_shape=out_shape,
        input_output_aliases=aliases,
        compiler_params=_params(("parallel",)),
        name="in_proj_rope" if rope else "in_proj",
    )(*args)


def _lane_masks(dtype):
    lane = lax.broadcasted_iota(jnp.int32, (1, LANE), 1)
    lo = lane < HEAD_DIM
    return lo, lo.astype(dtype), (~lo).astype(dtype)


_NA_TILES_PER_STEP = 2
_KEY_BLOCK = 256
_Q_SUB = 128


def _softmax_pv(qs, key_blocks, s_ref):
    macc = None
    for bi, (score_fn, _) in enumerate(key_blocks):
        sj = score_fn(qs)
        s_ref[bi] = sj
        macc = sj if macc is None else jnp.maximum(macc, sj)
    mb = jnp.broadcast_to(jnp.max(macc, axis=-1, keepdims=True), macc.shape)
    lacc = jnp.zeros(macc.shape, F32)
    o = jnp.zeros((qs.shape[0], LANE), F32)
    for bi, (_, v_fn) in enumerate(key_blocks):
        p = jnp.exp(s_ref[bi] - mb)
        lacc = lacc + p
        vb = v_fn()
        o = o + _mm(p[:, :vb.shape[0]], vb)
    return o * (1.0 / jnp.sum(lacc, axis=-1, keepdims=True))


def _attn_kernel(*refs, kv_tiles, cached):
    if cached:
        q_ref, k_ref, v_ref, kc_ref, vc_ref, o_ref, s_ref = refs
    else:
        q_ref, k_ref, v_ref, o_ref, s_ref = refs
    tq = q_ref.shape[0]
    kb = _KEY_BLOCK
    lo, m_lo, m_hi = _lane_masks(MXU_DTYPE)
    pairs_per_kv = (N_HEADS // 2) // kv_tiles
    sources = [(k_ref, v_ref)] + ([(kc_ref, vc_ref)] if cached else [])
    for q0 in range(0, tq, _Q_SUB):
        for hp in range(N_HEADS // 2):
            ksl = slice((hp // pairs_per_kv) * LANE, (hp // pairs_per_kv + 1) * LANE)
            q2 = q_ref[q0:q0 + _Q_SUB, hp * LANE:(hp + 1) * LANE]
            qs = jnp.concatenate([q2 * m_lo, q2 * m_hi], axis=0)
            blocks = [(functools.partial(lambda x, kr, off, ksl: _mm_nt(x, kr[off:off + kb, ksl]),
                                         kr=kr, off=off, ksl=ksl),
                       functools.partial(lambda vr, off, ksl: vr[off:off + kb, ksl], vr=vr, off=off, ksl=ksl))
                      for kr, vr in sources for off in range(0, kr.shape[0], kb)]
            o = _softmax_pv(qs, blocks, s_ref)
            o_ref[q0:q0 + _Q_SUB, hp * LANE:(hp + 1) * LANE] = jnp.where(
                lo, o[:_Q_SUB], o[_Q_SUB:]).astype(o_ref.dtype)


def _attention(q, k, v, kc, vc, tq, name, kv_layer=None, cache_layer=None):
    b, lq, _ = q.shape
    lk, kw = k.shape[-2], k.shape[-1]
    cached = kc is not None
    qmap = lambda bi, ti: (bi, ti, 0)

    def key_spec(n, layer):
        if layer is None:
            return pl.BlockSpec((None, n, kw), lambda bi, ti: (bi, 0, 0))
        return pl.BlockSpec((None, None, n, kw), lambda bi, ti: (bi, layer, 0, 0))

    in_specs = [pl.BlockSpec((None, tq, 512), qmap)] + [key_spec(lk, kv_layer)] * 2
    args = [q, k, v]
    lc = kc.shape[-2] if cached else 0
    if cached:
        in_specs += [key_spec(lc, cache_layer)] * 2
        args += [kc, vc]
    assert lk % _KEY_BLOCK == 0 and lc % _KEY_BLOCK == 0 and tq % _Q_SUB == 0
    n_blocks = (lk + lc) // _KEY_BLOCK
    return pl.pallas_call(
        functools.partial(_attn_kernel, kv_tiles=kw // LANE, cached=cached),
        grid=(b, lq // tq),
        in_specs=in_specs,
        out_specs=pl.BlockSpec((None, tq, 512), qmap),
        out_shape=jax.ShapeDtypeStruct((b, lq, 512), MXU_DTYPE),
        scratch_shapes=[pltpu.VMEM((n_blocks, 2 * _Q_SUB, _KEY_BLOCK), F32)],
        compiler_params=_params(("parallel", "parallel")),
        name=name,
    )(*args)


def _na_geometry(seq_len):
    rows = seq_len // GRID_W
    n_tiles = rows // 2
    assert rows >= NA_KEY_ROWS and NA_WIN_ROWS <= rows and NA_KEY_ROWS % 2 == 0
    ws = np.clip(2 * np.arange(n_tiles) - NA_WIN_ROWS // 2, 0, rows - NA_KEY_ROWS)
    r = 2 * np.arange(n_tiles)[:, None, None] + np.arange(2)[None, :, None]
    key_r = ws[:, None, None] + np.arange(NA_KEY_ROWS)[None, None, :]
    r0 = np.clip(r - NA_WIN_ROWS // 2, 0, rows - NA_WIN_ROWS)
    valid = (key_r >= r0) & (key_r < r0 + NA_WIN_ROWS)
    dr = np.where(valid, key_r - r + NA_WIN_ROWS - 1, 2 * NA_WIN_ROWS - 1)
    assert (valid.sum(-1) == NA_WIN_ROWS).all()
    return ws.astype(np.int32), dr.reshape(-1).astype(np.int32)


def _na_bias_blocks(rpb):
    h = rpb.shape[0]
    nrel = 2 * NA_WIN_ROWS - 1
    zeros = jnp.zeros((h, nrel, LANE - (2 * NA_WIN_COLS - 1)), F32)
    v = jnp.concatenate([rpb[..., NA_WIN_COLS - 1:], zeros, rpb[..., :NA_WIN_COLS - 1]], axis=-1).astype(F32)
    t = jnp.tile(v, (1, 1, GRID_W))[..., :GRID_W * (LANE - 1)].reshape(h, nrel, GRID_W, LANE - 1)[..., :GRID_W]
    c = np.arange(GRID_W)
    c0 = np.clip(c - NA_WIN_COLS // 2, 0, GRID_W - NA_WIN_COLS)
    colmask = (c[None, :] >= c0[:, None]) & (c[None, :] < c0[:, None] + NA_WIN_COLS)
    t = jnp.where(jnp.asarray(colmask)[None, None], t, NEG_BIG)
    t = jnp.concatenate([t, jnp.full((h, 1, GRID_W, GRID_W), NEG_BIG, F32)], axis=1)
    pad = jnp.zeros_like(t)
    return jnp.concatenate([t, pad], axis=-1), jnp.concatenate([pad, t], axis=-1)


def _na_kernel(ws_ref, dr_ref, q_ref, k_ref, v_ref, kc_ref, vc_ref, bl_ref, br_ref, o_ref, s_ref):
    for tt in range(_NA_TILES_PER_STEP):
        _na_tile(pl.program_id(1) * _NA_TILES_PER_STEP + tt, tt * 2 * GRID_W,
                 ws_ref, dr_ref, q_ref, k_ref, v_ref, kc_ref, vc_ref, bl_ref, br_ref, o_ref, s_ref)


def _na_tile(i, q0, ws_ref, dr_ref, q_ref, k_ref, v_ref, kc_ref, vc_ref, bl_ref, br_ref, o_ref, s_ref):
    start = pl.multiple_of(ws_ref[i] * GRID_W, GRID_W)
    nk = NA_KEY_ROWS * GRID_W
    kb = _KEY_BLOCK
    tq = 2 * GRID_W
    lc = kc_ref.shape[0]
    lo, m_lo, m_hi = _lane_masks(MXU_DTYPE)

    def bias_block(hp, off, width):
        rows = []
        for h in (2 * hp, 2 * hp + 1):
            for qr in range(2):
                base = (i * 2 + qr) * NA_KEY_ROWS + off // GRID_W
                tiles = [bl_ref[h, dr_ref[base + 2 * kp]] + br_ref[h, dr_ref[base + 2 * kp + 1]]
                         for kp in range(width // LANE)]
                rows.append(tiles[0] if len(tiles) == 1 else jnp.concatenate(tiles, axis=1))
        return jnp.concatenate(rows, axis=0)

    def local_scores(x, hp, sl, off, width):
        s = _mm_nt(x, k_ref[pl.ds(start + off, width), sl]) + bias_block(hp, off, width)
        if width < kb:
            s = jnp.concatenate([s, jnp.full((s.shape[0], kb - width), NEG_BIG, F32)], axis=1)
        return s

    for hp in range(N_HEADS // 2):
        sl = slice(hp * LANE, (hp + 1) * LANE)
        q2 = q_ref[q0:q0 + tq, sl]
        qs = jnp.concatenate([q2 * m_lo, q2 * m_hi], axis=0)
        blocks = []
        for off in range(0, nk, kb):
            width = min(kb, nk - off)
            blocks.append((functools.partial(local_scores, hp=hp, sl=sl, off=off, width=width),
                           functools.partial(lambda sl, off, width: v_ref[pl.ds(start + off, width), sl],
                                             sl=sl, off=off, width=width)))
        for off in range(0, lc, kb):
            blocks.append((functools.partial(lambda x, sl, off: _mm_nt(x, kc_ref[off:off + kb, sl]), sl=sl, off=off),
                           functools.partial(lambda sl, off: vc_ref[off:off + kb, sl], sl=sl, off=off)))
        o = _softmax_pv(qs, blocks, s_ref)
        o_ref[q0:q0 + tq, sl] = jnp.where(lo, o[:tq], o[tq:]).astype(o_ref.dtype)


def _neighbourhood_attention(q, k, v, kc, vc, cache_layer, rpb):
    b, seq_len, _ = q.shape
    lc = kc.shape[-2]
    ws, dr = _na_geometry(seq_len)
    b_left, b_right = _na_bias_blocks(rpb)
    n_tiles = len(ws)
    assert n_tiles % _NA_TILES_PER_STEP == 0
    tq = 2 * GRID_W
    tb = tq * _NA_TILES_PER_STEP
    qmap = lambda bi, ti, ws_r, dr_r: (bi, ti, 0)
    kmap = lambda bi, ti, ws_r, dr_r: (bi, 0, 0)
    bmap = lambda bi, ti, ws_r, dr_r: (0, 0, 0, 0)
    grid_spec = pltpu.PrefetchScalarGridSpec(
        num_scalar_prefetch=2,
        grid=(b, n_tiles // _NA_TILES_PER_STEP),
        in_specs=[
            pl.BlockSpec((None, tb, 512), qmap),
            pl.BlockSpec((None, seq_len, 512), kmap),
            pl.BlockSpec((None, seq_len, 512), kmap),
            pl.BlockSpec((None, None, lc, 512), lambda bi, ti, ws_r, dr_r: (bi, cache_layer, 0, 0)),
            pl.BlockSpec((None, None, lc, 512), lambda bi, ti, ws_r, dr_r: (bi, cache_layer, 0, 0)),
            pl.BlockSpec(b_left.shape, bmap),
            pl.BlockSpec(b_right.shape, bmap),
        ],
        out_specs=pl.BlockSpec((None, tb, 512), qmap),
        scratch_shapes=[pltpu.VMEM((-(-NA_KEY_ROWS * GRID_W // _KEY_BLOCK) + lc // _KEY_BLOCK, 2 * tq, _KEY_BLOCK),
                                   F32)],
    )
    assert lc % _KEY_BLOCK == 0
    return pl.pallas_call(
        _na_kernel,
        grid_spec=grid_spec,
        out_shape=jax.ShapeDtypeStruct((b, seq_len, 512), MXU_DTYPE),
        compiler_params=_params(("parallel", "arbitrary")),
        name="na_attn",
    )(jnp.asarray(ws), jnp.asarray(dr), q, k, v, kc, vc, b_left, b_right)


def _cmul(ar, ai, br, bi):
    return ar * br - ai * bi, ar * bi + ai * br


def _lam_bar(lr, li, ls):
    dt = jnp.exp(ls)
    mag = jnp.exp(lr * dt)
    return mag * jnp.cos(li * dt), mag * jnp.sin(li * dt)


def _zoh_coef(lr, li, zr, zi):
    nr, ni = zr - 1.0, zi
    den = 1.0 / (lr * lr + li * li)
    return (nr * lr + ni * li) * den, (ni * lr - nr * li) * den


def _squarings(zr, zi, n):
    out = [(zr, zi)]
    for _ in range(n - 1):
        zr, zi = _cmul(zr, zi, zr, zi)
        out.append((zr, zi))
    return out


def _cpow(squares, e):
    pr, pi = jnp.ones(e.shape, F32), jnp.zeros(e.shape, F32)
    for k, (zr, zi) in enumerate(squares):
        bit = jnp.bitwise_and(jnp.right_shift(e, k), 1) == 1
        nr, ni = _cmul(pr, pi, zr, zi)
        pr, pi = jnp.where(bit, nr, pr), jnp.where(bit, ni, pi)
    return pr, pi


def _ssm_ops_kernel(lrr_ref, lir_ref, lsr_ref, lrc_ref, lic_ref, lsc_ref, bt_ref, ct_ref, s0_ref, s1_ref,
                    wt_o, wb_o, wc_o, l16_o, tg_ref):
    tc, hg, p = SSM_CHUNK, SSM_GROUP_CH, SSM_STATE
    w = tc * hg
    lane_w = lax.broadcasted_iota(jnp.int32, (1, w), 1)
    lane_p = lax.broadcasted_iota(jnp.int32, (1, 2 * p), 1)
    row_w = lax.broadcasted_iota(jnp.int32, (2 * w, 1), 0)
    row_p = lax.broadcasted_iota(jnp.int32, (2 * p, 1), 0)
    tau_of_lane = jnp.right_shift(lane_w, 4)
    gl_of_lane = jnp.right_shift(lane_p, 6)
    step_of_row = jnp.right_shift(row_w, 5)
    same_group = jnp.bitwise_and(jnp.right_shift(row_w, 4), 1) == gl_of_lane
    first_rows = row_p < p
    tg_ref[...] = jnp.zeros_like(tg_ref)
    l16_rows = []
    for d in range(2):
        lr, li = lrr_ref[d], lir_ref[d]
        zr, zi = _lam_bar(lr, li, lsr_ref[d])
        cfr, cfi = _zoh_coef(lr, li, zr, zi)
        btr, bti = _cmul(cfr, cfi, bt_ref[d, 0], bt_ref[d, 1])
        sq = _squarings(zr, zi, 5)
        e_inj = (tc - 1 - step_of_row) if d == 0 else step_of_row
        pr, pi = _cpow(sq[:4], jnp.broadcast_to(e_inj, (2 * w, 2 * p)))
        ir, ii = _cmul(pr, pi, jnp.tile(btr, (2 * tc, 1)), jnp.tile(bti, (2 * tc, 1)))
        wb_o[:, (2 * d) * LANE:(2 * d + 1) * LANE] = jnp.where(same_group, ir, 0.0).astype(wb_o.dtype)
        wb_o[:, (2 * d + 1) * LANE:(2 * d + 2) * LANE] = jnp.where(same_group, ii, 0.0).astype(wb_o.dtype)
        l16_rows += [sq[4][0], sq[4][1]]
        lrc, lic = lrc_ref[d], lic_ref[d]
        zcr, zci = _lam_bar(lrc, lic, lsc_ref[d])
        tau = tau_of_lane if d == 0 else (tc - 1) - tau_of_lane
        pr, pi = _cpow(_squarings(zcr, zci, 4), jnp.broadcast_to(tau, (2 * p, w)))
        c0r, c0i = _cmul(ct_ref[d, 0], ct_ref[d, 1], pr, pi)
        c1r, c1i = _cmul(c0r, c0i, zcr, zci)
        for r, val in ((2 * d, c1r), (2 * d + 1, -c1i)):
            vb = val.astype(MXU_DTYPE)
            spread = jnp.where(first_rows, jnp.dot(vb, s0_ref[...], preferred_element_type=F32),
                               jnp.dot(vb, s1_ref[...], preferred_element_type=F32))
            wc_o[r * LANE:(r + 1) * LANE, :] = spread.astype(wc_o.dtype)
        for gl in range(2):
            in_group = gl_of_lane == gl
            kt = (jnp.dot(jnp.where(in_group, btr, 0.0), c0r, preferred_element_type=F32,
                          precision=lax.Precision.HIGHEST)
                  - jnp.dot(jnp.where(in_group, bti, 0.0), c0i, preferred_element_type=F32,
                            precision=lax.Precision.HIGHEST))
            for s in range(tc):
                if d == 0:
                    shift, keep = hg * s, lane_w >= hg * s
                else:
                    shift, keep = (w - hg * (tc - 1 - s)) % w, lane_w < hg * (s + 1)
                rolled = kt if shift == 0 else pltpu.roll(kt, shift, 1)
                tg_ref[gl, s * hg:(s + 1) * hg, :] += jnp.where(keep, rolled, 0.0)
    for gl, s_ref in enumerate((s0_ref, s1_ref)):
        spread = jnp.dot(tg_ref[gl].astype(MXU_DTYPE), s_ref[...], preferred_element_type=F32)
        for s in range(tc):
            wt_o[(2 * s + gl) * hg:(2 * s + gl + 1) * hg, :] = spread[s * hg:(s + 1) * hg].astype(wt_o.dtype)
    l16_o[...] = jnp.concatenate(l16_rows + [jnp.zeros((4, 2 * p), F32)], axis=0)


def _ssm_operators(lam_re, lam_im, log_step, b_re, b_im, c_re, c_im):
    depth = lam_re.shape[0]
    p, hg, tc, q = SSM_STATE, SSM_GROUP_CH, SSM_CHUNK, SSM_PAIRS
    assert (hg, p, tc) == (16, 64, 16), "lane/row index arithmetic in the kernel uses these as shifts"
    w = tc * hg

    def per_pair(a, tail):
        a = a.astype(F32).reshape((depth, 2, q, 2) + tail)
        return jnp.transpose(a, (0, 2, 1, 3) + tuple(range(4, 4 + len(tail))))

    lam_r, lam_i = per_pair(lam_re, (p,)), per_pair(lam_im, (p,))
    ls = jnp.broadcast_to(per_pair(log_step, ())[..., None], lam_r.shape)
    rows = [a.reshape(depth, q, 2, 1, 2 * p) for a in (lam_r, lam_i, ls)]
    cols = [a.reshape(depth, q, 2, 2 * p, 1) for a in (lam_r, lam_i, ls)]
    bt = jnp.stack([per_pair(b_re, (p, hg)), per_pair(b_im, (p, hg))], axis=3)
    bt = jnp.transpose(bt, (0, 1, 2, 3, 6, 4, 5)).reshape(depth, q, 2, 2, hg, 2 * p)
    ct = jnp.stack([per_pair(c_re, (hg, p)), per_pair(c_im, (hg, p))], axis=3)
    ct = jnp.transpose(ct, (0, 1, 2, 3, 4, 6, 5)).reshape(depth, q, 2, 2, 2 * p, hg)
    ct = jnp.tile(ct, (1, 1, 1, 1, 1, tc))
    r, c = np.arange(w)[:, None], np.arange(2 * w)[None, :]
    hit = (r // hg == c // (2 * hg)) & (r % hg == c % hg)
    spread = [jnp.asarray(hit & ((c // hg) % 2 == gl), MXU_DTYPE) for gl in range(2)]

    blk = lambda a: pl.BlockSpec((None, None) + a.shape[2:], lambda l, i: (l, i) + (0,) * (a.ndim - 2))
    const = pl.BlockSpec((w, 2 * w), lambda l, i: (0, 0))
    mat = pl.BlockSpec((None, None, 2 * w, 2 * w), lambda l, i: (l, i, 0, 0))
    args = rows + cols + [bt, ct]
    return pl.pallas_call(
        _ssm_ops_kernel,
        grid=(depth, q),
        in_specs=[blk(a) for a in args] + [const, const],
        out_specs=[mat, mat, mat, pl.BlockSpec((None, None, 8, 2 * p), lambda l, i: (l, i, 0, 0))],
        out_shape=[jax.ShapeDtypeStruct((depth, q, 2 * w, 2 * w), MXU_DTYPE)] * 3
        + [jax.ShapeDtypeStruct((depth, q, 8, 2 * p), F32)],
        scratch_shapes=[pltpu.VMEM((2, w, w), F32)],
        compiler_params=_params(("parallel", "parallel")),
        name="ssm_ops",
    )(*args, *spread)


_PAIRS_PER_TILE = 4
_SSM_ROW_BLOCK = 64
_SCAN_ROW_PAD = 8


def _ssm_kernel(zu_ref, wt_ref, wb_ref, wc_ref, l16_ref, h0_ref, y_o, fin_o, u_ref, yp_ref, *state_refs,
                batch, seq_len):
    tc, npair = SSM_CHUNK, _PAIRS_PER_TILE
    n_chunks = seq_len // tc
    nrows = batch * n_chunks
    rb = _SSM_ROW_BLOCK
    rs = n_chunks + _SCAN_ROW_PAD
    slot_w = LANE // npair
    slot = jnp.right_shift(lax.broadcasted_iota(jnp.int32, (1, LANE), 1), 5)
    dx_refs, xs_refs = state_refs[:4], state_refs[4:]

    def place(pieces, src_slot):
        offset = src_slot
        out = None
        for j, piece in enumerate(pieces):
            shift = (slot_w * (j - offset[j])) % LANE
            r = piece if shift == 0 else pltpu.roll(piece, shift, 1)
            out = r if out is None else jnp.where(slot == j, r, out)
        return out

    def gather_block(i, carry):
        r0 = pl.multiple_of(i * rb, rb)
        steps = [zu_ref[pl.ds(r0 * tc + s, rb, stride=tc), :] for s in range(tc)]
        for p in range(npair):
            tiles = [place(steps[4 * k:4 * k + 4], [p] * 4) for k in range(tc // 4)]
            u_ref[p, pl.ds(r0, rb), :] = jnp.concatenate(tiles, axis=1).astype(u_ref.dtype)
        return carry

    lax.fori_loop(0, nrows // rb, gather_block, 0)

    for p in range(npair):
        u = u_ref[p]
        y_intra = jnp.dot(u, wt_ref[p], preferred_element_type=F32)
        dx = jnp.dot(u, wb_ref[p], preferred_element_type=F32)
        for r in range(4):
            for b in range(batch):
                dx_refs[r][b * rs:b * rs + n_chunks, :] = dx[b * n_chunks:(b + 1) * n_chunks, r * LANE:(r + 1) * LANE]
        lfr, lfi, lbr, lbi = (l16_ref[p, r:r + 1, :] for r in range(4))

        def body(c, carry):
            fr, fi, br, bi = carry
            fwd = pl.ds(c, batch, stride=rs)
            bwd = pl.ds(n_chunks - 1 - c, batch, stride=rs)
            xs_refs[0][fwd, :] = fr
            xs_refs[1][fwd, :] = fi
            xs_refs[2][bwd, :] = br
            xs_refs[3][bwd, :] = bi
            nfr = lfr * fr - lfi * fi + dx_refs[0][fwd, :]
            nfi = lfr * fi + lfi * fr + dx_refs[1][fwd, :]
            nbr = lbr * br - lbi * bi + dx_refs[2][bwd, :]
            nbi = lbr * bi + lbi * br + dx_refs[3][bwd, :]
            return nfr, nfi, nbr, nbi

        fin = lax.fori_loop(0, n_chunks, body, tuple(h0_ref[p, :, r * LANE:(r + 1) * LANE] for r in range(4)),
                            unroll=4)
        for r in range(4):
            fin_o[p, :, r * LANE:(r + 1) * LANE] = fin[r]
        xs = jnp.concatenate(
            [jnp.concatenate([x[b * rs:b * rs + n_chunks, :] for b in range(batch)], axis=0) for x in xs_refs],
            axis=1).astype(MXU_DTYPE)
        yp_ref[p] = y_intra + jnp.dot(xs, wc_ref[p], preferred_element_type=F32)

    def scatter_block(i, carry):
        r0 = pl.multiple_of(i * rb, rb)
        for k in range(tc // 4):
            pieces = [yp_ref[p, pl.ds(r0, rb), k * LANE:(k + 1) * LANE] for p in range(npair)]
            for j in range(4):
                y_o[pl.ds(r0 * tc + 4 * k + j, rb, stride=tc), :] = place(pieces, [j] * npair)
        return carry

    lax.fori_loop(0, nrows // rb, scatter_block, 0)


def _ssm_scan(zu2d, w_t, w_b, w_c, l16, h0, batch, seq_len):
    t = zu2d.shape[0]
    nrows = t // SSM_CHUNK
    npair = _PAIRS_PER_TILE
    assert nrows % _SSM_ROW_BLOCK == 0 and seq_len % SSM_CHUNK == 0
    once = pl.Buffered(1)
    wspec = pl.BlockSpec((npair, 512, 512), lambda i: (i, 0, 0))
    return pl.pallas_call(
        functools.partial(_ssm_kernel, batch=batch, seq_len=seq_len),
        grid=(SSM_PAIRS // npair,),
        in_specs=[
            pl.BlockSpec((t, LANE), lambda i: (0, i), pipeline_mode=once),
            wspec, wspec, wspec,
            pl.BlockSpec((npair, 8, LANE), lambda i: (i, 0, 0)),
            pl.BlockSpec((npair, batch, 512), lambda i: (i, 0, 0)),
        ],
        out_specs=[
            pl.BlockSpec((t, LANE), lambda i: (0, i), pipeline_mode=once),
            pl.BlockSpec((npair, batch, 512), lambda i: (i, 0, 0)),
        ],
        out_shape=[jax.ShapeDtypeStruct((t, SSM_WIDTH), F32),
                   jax.ShapeDtypeStruct((SSM_PAIRS, batch, 512), F32)],
        scratch_shapes=[pltpu.VMEM((npair, nrows, 512), MXU_DTYPE), pltpu.VMEM((npair, nrows, 512), F32)]
        + [pltpu.VMEM((batch * (seq_len // SSM_CHUNK + _SCAN_ROW_PAD), LANE), F32)] * 8,
        compiler_params=_params(("arbitrary",)),
        name="ssm_scan",
    )(zu2d, w_t, w_b, w_c, l16, h0)


def _pack_state(s_re, s_im):
    b = s_re.shape[0]
    a = jnp.stack([s_re, s_im], axis=2).reshape(b, 2, 2, SSM_PAIRS, 2 * SSM_STATE)
    return jnp.transpose(a, (3, 0, 1, 2, 4)).reshape(SSM_PAIRS, b, 4 * 2 * SSM_STATE).astype(F32)


def _unpack_state(fin):
    b = fin.shape[1]
    a = jnp.transpose(fin.reshape(SSM_PAIRS, b, 2, 2, 2 * SSM_STATE), (1, 2, 3, 0, 4))
    a = a.reshape(b, 2, 2, SSM_GROUPS, SSM_STATE)
    return a[:, :, 0], a[:, :, 1]


def _merge_kernel(ya_ref, ys_ref, u_ref, yc_ref, g_ref, x_ref, mod_ref, ng_ref, d_ref,
                  wglu_ref, wa_ref, wb_ref, wc_ref, wo_ref, x_o, h_o):
    y = ys_ref[...].astype(F32) + d_ref[...] * u_ref[...]
    gl = _gelu_tanh(y)
    yb = gl * _sigmoid(_mm(gl, wglu_ref[...]))
    merged = (g_ref[:, 0:D_MODEL].astype(F32) * _mm(ya_ref[...], wa_ref[...])
              + g_ref[:, D_MODEL:2 * D_MODEL].astype(F32) * _mm(yb, wb_ref[...])
              + g_ref[:, 2 * D_MODEL:3 * D_MODEL].astype(F32) * _mm(yc_ref[...], wc_ref[...]))
    x1 = x_ref[...] + mod_ref[2:3, :] * _rms(_mm(merged, wo_ref[...]), ng_ref[1:2, :])
    x_o[...] = x1
    h_o[...] = (_rms(x1, ng_ref[2:3, :]) * (1.0 + mod_ref[4:5, :]) + mod_ref[3:4, :]).astype(h_o.dtype)


def _merge(ya, ys, zu, yc, gates, x2d, mod_l, mod_row, ng, ssm_d, w_glu, w_a, w_b, w_c, w_o):
    t = x2d.shape[0]
    tm = TOKEN_TILE
    row = lambda i: (i, 0)
    const = lambda i: (0, 0)
    r512 = pl.BlockSpec((tm, 512), row)
    wbr = pl.BlockSpec((512, D_MODEL), const)
    return pl.pallas_call(
        _merge_kernel,
        grid=(t // tm,),
        in_specs=[
            r512, r512, r512, r512,
            pl.BlockSpec((tm, 3 * D_MODEL), row),
            pl.BlockSpec((tm, D_MODEL), row),
            pl.BlockSpec((None, 6, D_MODEL), lambda i: (mod_row(i, tm), 0, 0)),
            pl.BlockSpec((4, D_MODEL), const),
            pl.BlockSpec((1, 512), const),
            pl.BlockSpec((512, 512), const),
            wbr, wbr, wbr,
            pl.BlockSpec((D_MODEL, D_MODEL), const),
        ],
        out_specs=[pl.BlockSpec((tm, D_MODEL), row), pl.BlockSpec((tm, D_MODEL), row)],
        out_shape=[jax.ShapeDtypeStruct((t, D_MODEL), F32), jax.ShapeDtypeStruct((t, D_MODEL), MXU_DTYPE)],
        compiler_params=_params(("parallel",)),
        name="merge",
    )(ya, ys, zu, yc, gates, x2d, mod_l, ng, ssm_d, w_glu, w_a, w_b, w_c, w_o)


def _ffn_kernel(*refs, seq_len, first, last):
    refs = list(refs)
    h_ref, hp_ref, hn_ref = refs[:3]
    del refs[:3]
    part_ref = None if first else refs.pop(0)
    if last:
        x_ref, mod_ref, ng_ref = refs[:3]
        del refs[:3]
    wa_ref, wg_ref, cwa_ref, cwg_ref, cba_ref, cbg_ref, wd_ref, out_ref, ua_ref, ug_ref, act_ref = refs
    i = pl.program_id(0)
    tm = h_ref.shape[0]
    ft = wd_ref.shape[0]
    n = tm + 16
    start = jnp.bitwise_and(i * tm, seq_len - 1)
    keep_prev = (start != 0).astype(F32)
    keep_next = (jnp.bitwise_and(start + tm, seq_len - 1) != 0).astype(F32)
    hh = jnp.concatenate([(hp_ref[...] * keep_prev).astype(h_ref.dtype), h_ref[...],
                          (hn_ref[...] * keep_next).astype(h_ref.dtype)], axis=0)
    ua_ref[...] = jnp.dot(hh, wa_ref[...], preferred_element_type=F32)
    ug_ref[...] = jnp.dot(hh, wg_ref[...], preferred_element_type=F32)
    interior = tm > seq_len
    if interior:
        pos = jnp.bitwise_and(lax.broadcasted_iota(jnp.int32, (tm, 1), 0), seq_len - 1)
        has_prev = (pos != 0).astype(F32)
        has_next = (pos != seq_len - 1).astype(F32)

    def conv(u_ref, cw_ref, cb_ref, lo):
        uc = u_ref[:, lo:lo + LANE]
        up = pltpu.roll(uc, 1, 0)[8:8 + tm]
        un = pltpu.roll(uc, n - 1, 0)[8:8 + tm]
        if interior:
            up, un = up * has_prev, un * has_next
        return (cw_ref[0:1, lo:lo + LANE] * up + cw_ref[1:2, lo:lo + LANE] * uc[8:8 + tm]
                + cw_ref[2:3, lo:lo + LANE] * un + cb_ref[0:1, lo:lo + LANE])

    for kc in range(ft // LANE):
        a = conv(ua_ref, cwa_ref, cba_ref, kc * LANE)
        g = conv(ug_ref, cwg_ref, cbg_ref, kc * LANE)
        act_ref[:, kc * LANE:(kc + 1) * LANE] = (g * _sigmoid(g) * a).astype(act_ref.dtype)
    total = jnp.dot(act_ref[...], wd_ref[...], preferred_element_type=F32)
    if not first:
        total = total + part_ref[...]
    if last:
        out_ref[...] = x_ref[...] + mod_ref[5:6, :] * _rms(total, ng_ref[3:4, :])
    else:
        out_ref[...] = total


def _conv_ffn(h2, x1, mod_l, mod_row, ng, w_up, conv_w, conv_b, w_down, seq_len):
    t = x1.shape[0]
    tm = FFN_TOKEN_TILE
    ft = FF_TILE
    nf = D_FF // ft
    nblk8 = t // 8
    assert seq_len & (seq_len - 1) == 0 and (seq_len % tm == 0 or tm % seq_len == 0) and t % tm == 0
    row = pl.BlockSpec((tm, D_MODEL), lambda i: (i, 0))
    once = pl.Buffered(1)
    part = None
    for j in range(nf):
        first, last = j == 0, j == nf - 1
        in_specs = [
            row,
            pl.BlockSpec((8, D_MODEL), lambda i: (jnp.maximum(i * (tm // 8) - 1, 0), 0)),
            pl.BlockSpec((8, D_MODEL), lambda i: (jnp.minimum((i + 1) * (tm // 8), nblk8 - 1), 0)),
        ]
        args = [h2, h2, h2]
        if not first:
            in_specs.append(row)
            args.append(part)
        if last:
            in_specs += [row, pl.BlockSpec((None, 6, D_MODEL), lambda i: (mod_row(i, tm), 0, 0)),
                         pl.BlockSpec((4, D_MODEL), lambda i: (0, 0))]
            args += [x1, mod_l, ng]
        in_specs += [
            pl.BlockSpec((D_MODEL, ft), lambda i, j=j: (0, j), pipeline_mode=once),
            pl.BlockSpec((D_MODEL, ft), lambda i, j=j: (0, nf + j), pipeline_mode=once),
            pl.BlockSpec((3, ft), lambda i, j=j: (0, j)),
            pl.BlockSpec((3, ft), lambda i, j=j: (0, nf + j)),
            pl.BlockSpec((1, ft), lambda i, j=j: (0, j)),
            pl.BlockSpec((1, ft), lambda i, j=j: (0, nf + j)),
            pl.BlockSpec((ft, D_MODEL), lambda i, j=j: (j, 0), pipeline_mode=once),
        ]
        args += [w_up, w_up, conv_w, conv_w, conv_b, conv_b, w_down]
        part = pl.pallas_call(
            functools.partial(_ffn_kernel, seq_len=seq_len, first=first, last=last),
            grid=(t // tm,),
            in_specs=in_specs,
            out_specs=row,
            out_shape=jax.ShapeDtypeStruct((t, D_MODEL), F32),
            scratch_shapes=[pltpu.VMEM((tm + 16, ft), F32), pltpu.VMEM((tm + 16, ft), F32),
                            pltpu.VMEM((tm, ft), MXU_DTYPE)],
            compiler_params=_params(("parallel",)),
            name="conv_ffn_last" if last else "conv_ffn_part",
        )(*args)
    return part


_Q_HEAD_ORDER = (0, 4, 1, 5, 2, 6, 3, 7)


def _rope_tables(seq_len):
    nf = HEAD_DIM // 4
    t = np.arange(seq_len)
    pos = np.stack([t // GRID_W, t % GRID_W]).astype(np.float32)
    inv = jnp.asarray(ROPE_THETA, F32) ** (-jnp.arange(nf, dtype=F32) / nf)
    ang = jnp.asarray(pos)[:, :, None] * inv
    d = np.arange(HEAD_DIM)
    ang = ang[d // (2 * nf), :, d % nf].T
    second = jnp.asarray(((d % (2 * nf)) // nf) == 1)[None, :]
    cos, sin = jnp.cos(ang), jnp.sin(ang)
    tabs = (cos, jnp.where(second, 0.0, -sin), jnp.where(second, sin, 0.0))
    return tuple(jnp.tile(x, (1, LANE // HEAD_DIM)).astype(F32) for x in tabs)


def _layer_weights(w_in, qk_g, w_br_a):
    hd = HEAD_DIM
    w_in_p = jnp.concatenate([w_in[:, h * hd:(h + 1) * hd] for h in _Q_HEAD_ORDER] + [w_in[:, 512:]],
                             axis=1).astype(MXU_DTYPE)
    w_a_p = jnp.concatenate([w_br_a[h * hd:(h + 1) * hd] for h in _Q_HEAD_ORDER], axis=0).astype(MXU_DTYPE)
    qg = jnp.tile(qk_g[0], N_HEADS).reshape(1, 512).astype(F32)
    kg = jnp.tile(qk_g[1], GA_KV_HEADS).reshape(1, LANE).astype(F32)
    return w_in_p, w_a_p, qg, kg


def kernel(x_prompt, x_sample, c, cache_ga_k, cache_ga_v, cache_na_k, cache_na_v, state_ssm_re, state_ssm_im,
           c_ctx, w_mod, b_mod, norm_g, w_in, qk_norm_g, na_rpb, ssm_lam_re, ssm_lam_im, ssm_log_step,
           ssm_b_re, ssm_b_im, ssm_c_re, ssm_c_im, ssm_d, w_glu, w_br_a, w_br_b, w_br_c, w_out,
           w_up, conv_w, conv_b, w_down):
    depth = w_in.shape[0]
    bp, lp, _ = x_prompt.shape
    bs, ls, _ = x_sample.shape
    lc = cache_ga_k.shape[2]
    assert lp % 256 == 0 and ls % FFN_TOKEN_TILE == 0 and (bp * lp) % FFN_TOKEN_TILE == 0
    assert FFN_TOKEN_TILE % TOKEN_TILE == 0
    assert bs % 8 == 0 and bp % 8 == 0, "the scan keeps one batch row per sublane"

    rows = 1 + bs
    rows_p = -(-rows // 8) * 8
    cvec = jnp.concatenate([c_ctx[None], c, jnp.zeros((rows_p - rows, D_MODEL), F32)], axis=0)
    mod = _modulation(cvec, w_mod, b_mod).reshape(depth, rows_p, 6, D_MODEL)

    w_t, w_b, w_c, l16 = _ssm_operators(ssm_lam_re, ssm_lam_im, ssm_log_step, ssm_b_re, ssm_b_im,
                                        ssm_c_re, ssm_c_im)
    seg = jnp.asarray(np.kron(np.eye(N_HEADS), np.full((HEAD_DIM, HEAD_DIM), 1.0 / HEAD_DIM)), MXU_DTYPE)
    rope_tabs = _rope_tables(ls)
    ctx_row = lambda i, tm: 0
    lat_row = lambda i, tm: 1 + (i * tm) // ls

    y_p = x_prompt.reshape(bp * lp, D_MODEL)
    y_s = x_sample.reshape(bs * ls, D_MODEL)
    zero_state = jnp.zeros((SSM_PAIRS, bp, 512), F32)
    cache_ga = [a.reshape(bs, depth, lc, LANE) for a in (cache_ga_k, cache_ga_v)]
    cache_na = [a.reshape(bs, depth, lc, 512) for a in (cache_na_k, cache_na_v)]
    states = ([], [])
    kv_stacks = None
    for l in range(depth):
        w_in_p, w_a_p, qg, kg = _layer_weights(w_in[l], qk_norm_g[l], w_br_a[l])
        w_glu_l, w_b_l, w_c_l, w_o_l = (a[l].astype(MXU_DTYPE) for a in (w_glu, w_br_b, w_br_c, w_out))
        ffn_w = (w_up[l].astype(MXU_DTYPE), conv_w[l].astype(F32), conv_b[l].reshape(1, 2 * D_FF).astype(F32),
                 w_down[l].astype(MXU_DTYPE))
        d_l = ssm_d[l].reshape(1, SSM_WIDTH).astype(F32)
        ng = norm_g[l].astype(F32)
        ssm_ops = (w_t[l], w_b[l], w_c[l], l16[l])

        q, k, v, zu, nq, nk, nv, gates = _in_projection(
            y_p, mod[l], ctx_row, ng, w_in_p, qg, kg, seg, None, lp, F32, stack=(l, depth, kv_stacks))
        kv_stacks = (k, v, nk, nv)
        r3 = lambda a: a.reshape(bp, lp, a.shape[-1])
        ya = _attention(r3(q), k, v, None, None, lp, "ga_ctx", kv_layer=l)
        yc = _attention(r3(nq), nk, nv, None, None, lp, "na_ctx", kv_layer=l)
        ys, fin = _ssm_scan(zu, *ssm_ops, zero_state, bp, lp)
        x1, h2 = _merge(ya.reshape(-1, 512), ys, zu, yc.reshape(-1, 512), gates, y_p,
                        mod[l], ctx_row, ng, d_l, w_glu_l, w_a_p, w_b_l, w_c_l, w_o_l)
        y_p = _conv_ffn(h2, x1, mod[l], ctx_row, ng, *ffn_w, lp)
        f_re, f_im = _unpack_state(fin)
        states[0].append(f_re)
        states[1].append(f_im)

        q, k, v, zu, nq, nk, nv, gates = _in_projection(
            y_s, mod[l], lat_row, ng, w_in_p, qg, kg, seg, rope_tabs, ls, MXU_DTYPE)
        r3 = lambda a: a.reshape(bs, ls, a.shape[-1])
        ya = _attention(r3(q), r3(k), r3(v), cache_ga[0], cache_ga[1], 4 * GRID_W, "ga_lat", cache_layer=l)
        yc = _neighbourhood_attention(r3(nq), r3(nk), r3(nv), cache_na[0], cache_na[1], l, na_rpb[l])
        h0 = _pack_state(state_ssm_re[:, l], state_ssm_im[:, l])
        ys, _ = _ssm_scan(zu, *ssm_ops, h0, bs, ls)
        x1, h2 = _merge(ya.reshape(-1, 512), ys, zu, yc.reshape(-1, 512), gates, y_s,
                        mod[l], lat_row, ng, d_l, w_glu_l, w_a_p, w_b_l, w_c_l, w_o_l)
        y_s = _conv_ffn(h2, x1, mod[l], lat_row, ng, *ffn_w, ls)

    k, v, nk, nv = kv_stacks
    return (y_p.reshape(bp, lp, D_MODEL), y_s.reshape(bs, ls, D_MODEL),
            k.reshape(bp, depth, lp, GA_KV_HEADS, HEAD_DIM), v.reshape(bp, depth, lp, GA_KV_HEADS, HEAD_DIM),
            nk.reshape(bp, depth, lp, N_HEADS, HEAD_DIM), nv.reshape(bp, depth, lp, N_HEADS, HEAD_DIM),
            jnp.stack(states[0], axis=1), jnp.stack(states[1], axis=1))
```

```python
import functools
import math

import numpy as np
import jax
import jax.numpy as jnp
from jax import lax
from jax.experimental import pallas as pl
from jax.experimental.pallas import tpu as pltpu

F32 = jnp.float32
MXU_DTYPE = jnp.bfloat16

D_MODEL = 1024
HEAD_DIM = 64
N_HEADS = 8
GA_KV_HEADS = 2
GRID_W = 64
NA_WIN_ROWS = 8
NA_WIN_COLS = 16
NA_KEY_ROWS = 10
SSM_WIDTH = 512
SSM_GROUPS = 32
SSM_GROUP_CH = 16
SSM_STATE = 64
SSM_CHUNK = 16
SSM_PAIRS = SSM_GROUPS // 2
D_FF = 2816
FF_TILE = 1408
ROPE_THETA = 10000.0
EPS = 1e-6
IN_WIDTH = 5888
NEG_BIG = -1e30

LANE = 128
TOKEN_TILE = 512
FFN_TOKEN_TILE = 1024
_FF_SUB = 256
VMEM_LIMIT = 56 * 1024 * 1024

_Q0, _K0, _V0, _U0, _NQ0, _NK0, _NV0, _G0 = 0, 512, 640, 768, 1280, 1792, 2304, 2816


def _sigmoid(x):
    return 1.0 / (1.0 + jnp.exp(-x))


def _gelu_tanh(x):
    return 0.5 * x * (1.0 + jnp.tanh(math.sqrt(2.0 / math.pi) * (x + 0.044715 * (x * x * x))))


def _rms(x, g):
    ms = jnp.mean(x * x, axis=-1, keepdims=True)
    return (x * lax.rsqrt(ms + EPS)) * g


def _mm(a, b):
    return jnp.dot(a.astype(MXU_DTYPE), b.astype(MXU_DTYPE), preferred_element_type=F32)


def _mm_nt(a, b):
    return lax.dot_general(a.astype(MXU_DTYPE), b.astype(MXU_DTYPE), (((1,), (1,)), ((), ())),
                           preferred_element_type=F32)


def _params(sem):
    return pltpu.CompilerParams(dimension_semantics=sem, vmem_limit_bytes=VMEM_LIMIT)


def _mod_kernel(c_ref, w_ref, b_ref, o_ref):
    c = c_ref[...]
    o_ref[...] = _mm(c * _sigmoid(c), w_ref[...]) + b_ref[...]


def _modulation(cvec, w_mod, b_mod):
    depth = w_mod.shape[0]
    rows = cvec.shape[0]
    tn = 1536
    return pl.pallas_call(
        _mod_kernel,
        grid=(depth, 6 * D_MODEL // tn),
        in_specs=[
            pl.BlockSpec((rows, D_MODEL), lambda l, j: (0, 0)),
            pl.BlockSpec((None, D_MODEL, tn), lambda l, j: (l, 0, j)),
            pl.BlockSpec((None, 1, tn), lambda l, j: (l, 0, j)),
        ],
        out_specs=pl.BlockSpec((None, rows, tn), lambda l, j: (l, 0, j)),
        out_shape=jax.ShapeDtypeStruct((depth, rows, 6 * D_MODEL), F32),
        compiler_params=_params(("parallel", "parallel")),
        name="adaln_mod",
    )(cvec, w_mod, b_mod.reshape(depth, 1, 6 * D_MODEL))


def _head_rms(z, seg, gain):
    ms = jnp.dot((z * z).astype(MXU_DTYPE), seg, preferred_element_type=F32)
    return (z * lax.rsqrt(ms + EPS)) * gain


def _rope_tile(t, c, s_up, s_dn):
    return t * c + pltpu.roll(t, LANE - 16, 1) * s_up + pltpu.roll(t, 16, 1) * s_dn


def _inproj_kernel(*refs, rope, n_carried):
    n_in = 10 if rope else 7
    refs = refs[:n_in] + refs[n_in + n_carried:]
    if rope:
        (x_ref, mod_ref, ng_ref, w_ref, qg_ref, kg_ref, seg_ref, cos_ref, sup_ref, sdn_ref,
         q_o, k_o, v_o, u_o, nq_o, nk_o, nv_o, g_o) = refs
    else:
        (x_ref, mod_ref, ng_ref, w_ref, qg_ref, kg_ref, seg_ref,
         q_o, k_o, v_o, u_o, nq_o, nk_o, nv_o, g_o) = refs
    x = x_ref[...]
    h = _rms(x, ng_ref[0:1, :]) * (1.0 + mod_ref[1:2, :]) + mod_ref[0:1, :]
    hb = h.astype(MXU_DTYPE)
    scale = HEAD_DIM ** -0.5

    def proj(lo, width):
        return jnp.dot(hb, w_ref[:, lo:lo + width], preferred_element_type=F32)

    def maybe_rope(z):
        if not rope:
            return z
        c, su, sd = cos_ref[...], sup_ref[...], sdn_ref[...]
        tiles = [_rope_tile(z[:, i * LANE:(i + 1) * LANE], c, su, sd) for i in range(z.shape[1] // LANE)]
        return tiles[0] if len(tiles) == 1 else jnp.concatenate(tiles, axis=1)

    q = maybe_rope(_head_rms(proj(_Q0, 512), seg_ref[...], qg_ref[...]))
    q_o[...] = (q * scale).astype(q_o.dtype)
    k = maybe_rope(_head_rms(proj(_K0, 128), seg_ref[0:LANE, 0:LANE], kg_ref[...]))
    k_o[...] = k.astype(k_o.dtype).reshape(k_o.shape)
    v_o[...] = proj(_V0, 128).astype(v_o.dtype).reshape(v_o.shape)
    u_o[...] = proj(_U0, 512).astype(u_o.dtype)
    nq_o[...] = (proj(_NQ0, 512) * scale).astype(nq_o.dtype)
    nk_o[...] = proj(_NK0, 512).astype(nk_o.dtype).reshape(nk_o.shape)
    nv_o[...] = proj(_NV0, 512).astype(nv_o.dtype).reshape(nv_o.shape)
    for i in range(3):
        g_o[:, i * D_MODEL:(i + 1) * D_MODEL] = _sigmoid(proj(_G0 + i * D_MODEL, D_MODEL)).astype(g_o.dtype)


def _in_projection(x2d, mod_l, mod_row, ng, w_in, qg, kg, seg, rope_tabs, seq_len, kv_dtype, stack=None):
    t = x2d.shape[0]
    tm = TOKEN_TILE
    tiles_per_seq = max(seq_len // tm, 1)
    rope = rope_tabs is not None
    row = lambda i: (i, 0)
    const = lambda i: (0, 0)
    in_specs = [
        pl.BlockSpec((tm, D_MODEL), row),
        pl.BlockSpec((None, 6, D_MODEL), lambda i: (mod_row(i, tm), 0, 0)),
        pl.BlockSpec((4, D_MODEL), const),
        pl.BlockSpec((D_MODEL, IN_WIDTH), const),
        pl.BlockSpec((1, 512), const),
        pl.BlockSpec((1, LANE), const),
        pl.BlockSpec((512, 512), const),
    ]
    args = [x2d, mod_l, ng, w_in, qg, kg, seg]
    if rope:
        in_specs += [pl.BlockSpec((tm, LANE), lambda i: (i % tiles_per_seq, 0))] * 3
        args += list(rope_tabs)
    widths = (512, 128, 128, 512, 512, 512, 512, 3 * D_MODEL)
    dtypes = (MXU_DTYPE, kv_dtype, kv_dtype, F32, MXU_DTYPE, kv_dtype, kv_dtype, MXU_DTYPE)
    out_specs = [pl.BlockSpec((tm, w), row) for w in widths]
    out_shape = [jax.ShapeDtypeStruct((t, w), dt) for w, dt in zip(widths, dtypes)]
    aliases, n_carried = {}, 0
    if stack is not None:
        layer, depth, carried = stack
        assert tm % seq_len == 0
        for o in (1, 2, 5, 6):
            out_specs[o] = pl.BlockSpec((tm // seq_len, None, seq_len, widths[o]), lambda i: (i, layer, 0, 0))
            out_shape[o] = jax.ShapeDtypeStruct((t // seq_len, depth, seq_len, widths[o]), dtypes[o])
        if carried is not None:
            n_carried = len(carried)
            aliases = {len(args) + n: o for n, o in enumerate((1, 2, 5, 6))}
            in_specs += [pl.BlockSpec(memory_space=pl.ANY)] * n_carried
            args += list(carried)
    return pl.pallas_call(
        functools.partial(_inproj_kernel, rope=rope, n_carried=n_carried),
        grid=(t // tm,),
        in_specs=in_specs,
        out_specs=out_specs,
        out_shape=out_shape,
        input_output_aliases=aliases,
        compiler_params=_params(("parallel",)),
        name="in_proj_rope" if rope else "in_proj",
    )(*args)


def _lane_masks(dtype):
    lane = lax.broadcasted_iota(jnp.int32, (1, LANE), 1)
    lo = lane < HEAD_DIM
    return lo, lo.astype(dtype), (~lo).astype(dtype)


_NA_TILES_PER_STEP = 2
_KEY_BLOCK = 256
_Q_SUB = 128


def _softmax_pv(qs, key_blocks, s_ref):
    macc = None
    for bi, (score_fn, _) in enumerate(key_blocks):
        sj = score_fn(qs)
        s_ref[bi] = sj
        macc = sj if macc is None else jnp.maximum(macc, sj)
    mb = jnp.broadcast_to(jnp.max(macc, axis=-1, keepdims=True), macc.shape)
    lacc = jnp.zeros(macc.shape, F32)
    o = jnp.zeros((qs.shape[0], LANE), F32)
    for bi, (_, v_fn) in enumerate(key_blocks):
        p = jnp.exp(s_ref[bi] - mb)
        lacc = lacc + p
        vb = v_fn()
        o = o + _mm(p[:, :vb.shape[0]], vb)
    return o * (1.0 / jnp.sum(lacc, axis=-1, keepdims=True))


def _attn_kernel(*refs, kv_tiles, cached):
    if cached:
        q_ref, k_ref, v_ref, kc_ref, vc_ref, o_ref, s_ref = refs
    else:
        q_ref, k_ref, v_ref, o_ref, s_ref = refs
    tq = q_ref.shape[0]
    kb = _KEY_BLOCK
    lo, m_lo, m_hi = _lane_masks(MXU_DTYPE)
    pairs_per_kv = (N_HEADS // 2) // kv_tiles
    sources = [(k_ref, v_ref)] + ([(kc_ref, vc_ref)] if cached else [])
    for q0 in range(0, tq, _Q_SUB):
        for hp in range(N_HEADS // 2):
            ksl = slice((hp // pairs_per_kv) * LANE, (hp // pairs_per_kv + 1) * LANE)
            q2 = q_ref[q0:q0 + _Q_SUB, hp * LANE:(hp + 1) * LANE]
            qs = jnp.concatenate([q2 * m_lo, q2 * m_hi], axis=0)
            blocks = [(functools.partial(lambda x, kr, off, ksl: _mm_nt(x, kr[off:off + kb, ksl]),
                                         kr=kr, off=off, ksl=ksl),
                       functools.partial(lambda vr, off, ksl: vr[off:off + kb, ksl], vr=vr, off=off, ksl=ksl))
                      for kr, vr in sources for off in range(0, kr.shape[0], kb)]
            o = _softmax_pv(qs, blocks, s_ref)
            o_ref[q0:q0 + _Q_SUB, hp * LANE:(hp + 1) * LANE] = jnp.where(
                lo, o[:_Q_SUB], o[_Q_SUB:]).astype(o_ref.dtype)


def _attention(q, k, v, kc, vc, tq, name, kv_layer=None, cache_layer=None):
    b, lq, _ = q.shape
    lk, kw = k.shape[-2], k.shape[-1]
    cached = kc is not None
    qmap = lambda bi, ti: (bi, ti, 0)

    def key_spec(n, layer):
        if layer is None:
            return pl.BlockSpec((None, n, kw), lambda bi, ti: (bi, 0, 0))
        return pl.BlockSpec((None, None, n, kw), lambda bi, ti: (bi, layer, 0, 0))

    in_specs = [pl.BlockSpec((None, tq, 512), qmap)] + [key_spec(lk, kv_layer)] * 2
    args = [q, k, v]
    lc = kc.shape[-2] if cached else 0
    if cached:
        in_specs += [key_spec(lc, cache_layer)] * 2
        args += [kc, vc]
    assert lk % _KEY_BLOCK == 0 and lc % _KEY_BLOCK == 0 and tq % _Q_SUB == 0
    n_blocks = (lk + lc) // _KEY_BLOCK
    return pl.pallas_call(
        functools.partial(_attn_kernel, kv_tiles=kw // LANE, cached=cached),
        grid=(b, lq // tq),
        in_specs=in_specs,
        out_specs=pl.BlockSpec((None, tq, 512), qmap),
        out_shape=jax.ShapeDtypeStruct((b, lq, 512), MXU_DTYPE),
        scratch_shapes=[pltpu.VMEM((n_blocks, 2 * _Q_SUB, _KEY_BLOCK), F32)],
        compiler_params=_params(("parallel", "parallel")),
        name=name,
    )(*args)


def _na_geometry(seq_len):
    rows = seq_len // GRID_W
    n_tiles = rows // 2
    assert rows >= NA_KEY_ROWS and NA_WIN_ROWS <= rows and NA_KEY_ROWS % 2 == 0
    ws = np.clip(2 * np.arange(n_tiles) - NA_WIN_ROWS // 2, 0, rows - NA_KEY_ROWS)
    r = 2 * np.arange(n_tiles)[:, None, None] + np.arange(2)[None, :, None]
    key_r = ws[:, None, None] + np.arange(NA_KEY_ROWS)[None, None, :]
    r0 = np.clip(r - NA_WIN_ROWS // 2, 0, rows - NA_WIN_ROWS)
    valid = (key_r >= r0) & (key_r < r0 + NA_WIN_ROWS)
    dr = np.where(valid, key_r - r + NA_WIN_ROWS - 1, 2 * NA_WIN_ROWS - 1)
    assert (valid.sum(-1) == NA_WIN_ROWS).all()
    return ws.astype(np.int32), dr.reshape(-1).astype(np.int32)


def _na_bias_blocks(rpb):
    h = rpb.shape[0]
    nrel = 2 * NA_WIN_ROWS - 1
    zeros = jnp.zeros((h, nrel, LANE - (2 * NA_WIN_COLS - 1)), F32)
    v = jnp.concatenate([rpb[..., NA_WIN_COLS - 1:], zeros, rpb[..., :NA_WIN_COLS - 1]], axis=-1).astype(F32)
    t = jnp.tile(v, (1, 1, GRID_W))[..., :GRID_W * (LANE - 1)].reshape(h, nrel, GRID_W, LANE - 1)[..., :GRID_W]
    c = np.arange(GRID_W)
    c0 = np.clip(c - NA_WIN_COLS // 2, 0, GRID_W - NA_WIN_COLS)
    colmask = (c[None, :] >= c0[:, None]) & (c[None, :] < c0[:, None] + NA_WIN_COLS)
    t = jnp.where(jnp.asarray(colmask)[None, None], t, NEG_BIG)
    t = jnp.concatenate([t, jnp.full((h, 1, GRID_W, GRID_W), NEG_BIG, F32)], axis=1)
    pad = jnp.zeros_like(t)
    return jnp.concatenate([t, pad], axis=-1), jnp.concatenate([pad, t], axis=-1)


def _na_kernel(ws_ref, dr_ref, q_ref, k_ref, v_ref, kc_ref, vc_ref, bl_ref, br_ref, o_ref, s_ref):
    for tt in range(_NA_TILES_PER_STEP):
        _na_tile(pl.program_id(1) * _NA_TILES_PER_STEP + tt, tt * 2 * GRID_W,
                 ws_ref, dr_ref, q_ref, k_ref, v_ref, kc_ref, vc_ref, bl_ref, br_ref, o_ref, s_ref)


def _na_tile(i, q0, ws_ref, dr_ref, q_ref, k_ref, v_ref, kc_ref, vc_ref, bl_ref, br_ref, o_ref, s_ref):
    start = pl.multiple_of(ws_ref[i] * GRID_W, GRID_W)
    nk = NA_KEY_ROWS * GRID_W
    kb = _KEY_BLOCK
    tq = 2 * GRID_W
    lc = kc_ref.shape[0]
    lo, m_lo, m_hi = _lane_masks(MXU_DTYPE)

    def bias_block(hp, off, width):
        rows = []
        for h in (2 * hp, 2 * hp + 1):
            for qr in range(2):
                base = (i * 2 + qr) * NA_KEY_ROWS + off // GRID_W
                tiles = [bl_ref[h, dr_ref[base + 2 * kp]] + br_ref[h, dr_ref[base + 2 * kp + 1]]
                         for kp in range(width // LANE)]
                rows.append(tiles[0] if len(tiles) == 1 else jnp.concatenate(tiles, axis=1))
        return jnp.concatenate(rows, axis=0)

    def local_scores(x, hp, sl, off, width):
        s = _mm_nt(x, k_ref[pl.ds(start + off, width), sl]) + bias_block(hp, off, width)
        if width < kb:
            s = jnp.concatenate([s, jnp.full((s.shape[0], kb - width), NEG_BIG, F32)], axis=1)
        return s

    for hp in range(N_HEADS // 2):
        sl = slice(hp * LANE, (hp + 1) * LANE)
        q2 = q_ref[q0:q0 + tq, sl]
        qs = jnp.concatenate([q2 * m_lo, q2 * m_hi], axis=0)
        blocks = []
        for off in range(0, nk, kb):
            width = min(kb, nk - off)
            blocks.append((functools.partial(local_scores, hp=hp, sl=sl, off=off, width=width),
                           functools.partial(lambda sl, off, width: v_ref[pl.ds(start + off, width), sl],
                                             sl=sl, off=off, width=width)))
        for off in range(0, lc, kb):
            blocks.append((functools.partial(lambda x, sl, off: _mm_nt(x, kc_ref[off:off + kb, sl]), sl=sl, off=off),
                           functools.partial(lambda sl, off: vc_ref[off:off + kb, sl], sl=sl, off=off)))
        o = _softmax_pv(qs, blocks, s_ref)
        o_ref[q0:q0 + tq, sl] = jnp.where(lo, o[:tq], o[tq:]).astype(o_ref.dtype)


def _neighbourhood_attention(q, k, v, kc, vc, cache_layer, rpb):
    b, seq_len, _ = q.shape
    lc = kc.shape[-2]
    ws, dr = _na_geometry(seq_len)
    b_left, b_right = _na_bias_blocks(rpb)
    n_tiles = len(ws)
    assert n_tiles % _NA_TILES_PER_STEP == 0
    tq = 2 * GRID_W
    tb = tq * _NA_TILES_PER_STEP
    qmap = lambda bi, ti, ws_r, dr_r: (bi, ti, 0)
    kmap = lambda bi, ti, ws_r, dr_r: (bi, 0, 0)
    bmap = lambda bi, ti, ws_r, dr_r: (0, 0, 0, 0)
    grid_spec = pltpu.PrefetchScalarGridSpec(
        num_scalar_prefetch=2,
        grid=(b, n_tiles // _NA_TILES_PER_STEP),
        in_specs=[
            pl.BlockSpec((None, tb, 512), qmap),
            pl.BlockSpec((None, seq_len, 512), kmap),
            pl.BlockSpec((None, seq_len, 512), kmap),
            pl.BlockSpec((None, None, lc, 512), lambda bi, ti, ws_r, dr_r: (bi, cache_layer, 0, 0)),
            pl.BlockSpec((None, None, lc, 512), lambda bi, ti, ws_r, dr_r: (bi, cache_layer, 0, 0)),
            pl.BlockSpec(b_left.shape, bmap),
            pl.BlockSpec(b_right.shape, bmap),
        ],
        out_specs=pl.BlockSpec((None, tb, 512), qmap),
        scratch_shapes=[pltpu.VMEM((-(-NA_KEY_ROWS * GRID_W // _KEY_BLOCK) + lc // _KEY_BLOCK, 2 * tq, _KEY_BLOCK),
                                   F32)],
    )
    assert lc % _KEY_BLOCK == 0
    return pl.pallas_call(
        _na_kernel,
        grid_spec=grid_spec,
        out_shape=jax.ShapeDtypeStruct((b, seq_len, 512), MXU_DTYPE),
        compiler_params=_params(("parallel", "arbitrary")),
        name="na_attn",
    )(jnp.asarray(ws), jnp.asarray(dr), q, k, v, kc, vc, b_left, b_right)


def _cmul(ar, ai, br, bi):
    return ar * br - ai * bi, ar * bi + ai * br


def _lam_bar(lr, li, ls):
    dt = jnp.exp(ls)
    mag = jnp.exp(lr * dt)
    return mag * jnp.cos(li * dt), mag * jnp.sin(li * dt)


def _zoh_coef(lr, li, zr, zi):
    nr, ni = zr - 1.0, zi
    den = 1.0 / (lr * lr + li * li)
    return (nr * lr + ni * li) * den, (ni * lr - nr * li) * den


def _squarings(zr, zi, n):
    out = [(zr, zi)]
    for _ in range(n - 1):
        zr, zi = _cmul(zr, zi, zr, zi)
        out.append((zr, zi))
    return out


def _cpow(squares, e):
    pr, pi = jnp.ones(e.shape, F32), jnp.zeros(e.shape, F32)
    for k, (zr, zi) in enumerate(squares):
        bit = jnp.bitwise_and(jnp.right_shift(e, k), 1) == 1
        nr, ni = _cmul(pr, pi, zr, zi)
        pr, pi = jnp.where(bit, nr, pr), jnp.where(bit, ni, pi)
    return pr, pi


def _ssm_ops_kernel(lrr_ref, lir_ref, lsr_ref, lrc_ref, lic_ref, lsc_ref, bt_ref, ct_ref, s0_ref, s1_ref,
                    wt_o, wb_o, wc_o, l16_o, tg_ref):
    tc, hg, p = SSM_CHUNK, SSM_GROUP_CH, SSM_STATE
    w = tc * hg
    lane_w = lax.broadcasted_iota(jnp.int32, (1, w), 1)
    lane_p = lax.broadcasted_iota(jnp.int32, (1, 2 * p), 1)
    row_w = lax.broadcasted_iota(jnp.int32, (2 * w, 1), 0)
    row_p = lax.broadcasted_iota(jnp.int32, (2 * p, 1), 0)
    tau_of_lane = jnp.right_shift(lane_w, 4)
    gl_of_lane = jnp.right_shift(lane_p, 6)
    step_of_row = jnp.right_shift(row_w, 5)
    same_group = jnp.bitwise_and(jnp.right_shift(row_w, 4), 1) == gl_of_lane
    first_rows = row_p < p
    tg_ref[...] = jnp.zeros_like(tg_ref)
    l16_rows = []
    for d in range(2):
        lr, li = lrr_ref[d], lir_ref[d]
        zr, zi = _lam_bar(lr, li, lsr_ref[d])
        cfr, cfi = _zoh_coef(lr, li, zr, zi)
        btr, bti = _cmul(cfr, cfi, bt_ref[d, 0], bt_ref[d, 1])
        sq = _squarings(zr, zi, 5)
        e_inj = (tc - 1 - step_of_row) if d == 0 else step_of_row
        pr, pi = _cpow(sq[:4], jnp.broadcast_to(e_inj, (2 * w, 2 * p)))
        ir, ii = _cmul(pr, pi, jnp.tile(btr, (2 * tc, 1)), jnp.tile(bti, (2 * tc, 1)))
        wb_o[:, (2 * d) * LANE:(2 * d + 1) * LANE] = jnp.where(same_group, ir, 0.0).astype(wb_o.dtype)
        wb_o[:, (2 * d + 1) * LANE:(2 * d + 2) * LANE] = jnp.where(same_group, ii, 0.0).astype(wb_o.dtype)
        l16_rows += [sq[4][0], sq[4][1]]
        lrc, lic = lrc_ref[d], lic_ref[d]
        zcr, zci = _lam_bar(lrc, lic, lsc_ref[d])
        tau = tau_of_lane if d == 0 else (tc - 1) - tau_of_lane
        pr, pi = _cpow(_squarings(zcr, zci, 4), jnp.broadcast_to(tau, (2 * p, w)))
        c0r, c0i = _cmul(ct_ref[d, 0], ct_ref[d, 1], pr, pi)
        c1r, c1i = _cmul(c0r, c0i, zcr, zci)
        for r, val in ((2 * d, c1r), (2 * d + 1, -c1i)):
            vb = val.astype(MXU_DTYPE)
            spread = jnp.where(first_rows, jnp.dot(vb, s0_ref[...], preferred_element_type=F32),
                               jnp.dot(vb, s1_ref[...], preferred_element_type=F32))
            wc_o[r * LANE:(r + 1) * LANE, :] = spread.astype(wc_o.dtype)
        for gl in range(2):
            in_group = gl_of_lane == gl
            kt = (jnp.dot(jnp.where(in_group, btr, 0.0), c0r, preferred_element_type=F32,
                          precision=lax.Precision.HIGHEST)
                  - jnp.dot(jnp.where(in_group, bti, 0.0), c0i, preferred_element_type=F32,
                            precision=lax.Precision.HIGHEST))
            for s in range(tc):
                if d == 0:
                    shift, keep = hg * s, lane_w >= hg * s
                else:
                    shift, keep = (w - hg * (tc - 1 - s)) % w, lane_w < hg * (s + 1)
                rolled = kt if shift == 0 else pltpu.roll(kt, shift, 1)
                tg_ref[gl, s * hg:(s + 1) * hg, :] += jnp.where(keep, rolled, 0.0)
    for gl, s_ref in enumerate((s0_ref, s1_ref)):
        spread = jnp.dot(tg_ref[gl].astype(MXU_DTYPE), s_ref[...], preferred_element_type=F32)
        for s in range(tc):
            wt_o[(2 * s + gl) * hg:(2 * s + gl + 1) * hg, :] = spread[s * hg:(s + 1) * hg].astype(wt_o.dtype)
    l16_o[...] = jnp.concatenate(l16_rows + [jnp.zeros((4, 2 * p), F32)], axis=0)


def _ssm_operators(lam_re, lam_im, log_step, b_re, b_im, c_re, c_im):
    depth = lam_re.shape[0]
    p, hg, tc, q = SSM_STATE, SSM_GROUP_CH, SSM_CHUNK, SSM_PAIRS
    assert (hg, p, tc) == (16, 64, 16), "lane/row index arithmetic in the kernel uses these as shifts"
    w = tc * hg

    def per_pair(a, tail):
        a = a.astype(F32).reshape((depth, 2, q, 2) + tail)
        return jnp.transpose(a, (0, 2, 1, 3) + tuple(range(4, 4 + len(tail))))

    lam_r, lam_i = per_pair(lam_re, (p,)), per_pair(lam_im, (p,))
    ls = jnp.broadcast_to(per_pair(log_step, ())[..., None], lam_r.shape)
    rows = [a.reshape(depth, q, 2, 1, 2 * p) for a in (lam_r, lam_i, ls)]
    cols = [a.reshape(depth, q, 2, 2 * p, 1) for a in (lam_r, lam_i, ls)]
    bt = jnp.stack([per_pair(b_re, (p, hg)), per_pair(b_im, (p, hg))], axis=3)
    bt = jnp.transpose(bt, (0, 1, 2, 3, 6, 4, 5)).reshape(depth, q, 2, 2, hg, 2 * p)
    ct = jnp.stack([per_pair(c_re, (hg, p)), per_pair(c_im, (hg, p))], axis=3)
    ct = jnp.transpose(ct, (0, 1, 2, 3, 4, 6, 5)).reshape(depth, q, 2, 2, 2 * p, hg)
    ct = jnp.tile(ct, (1, 1, 1, 1, 1, tc))
    r, c = np.arange(w)[:, None], np.arange(2 * w)[None, :]
    hit = (r // hg == c // (2 * hg)) & (r % hg == c % hg)
    spread = [jnp.asarray(hit & ((c // hg) % 2 == gl), MXU_DTYPE) for gl in range(2)]

    blk = lambda a: pl.BlockSpec((None, None) + a.shape[2:], lambda l, i: (l, i) + (0,) * (a.ndim - 2))
    const = pl.BlockSpec((w, 2 * w), lambda l, i: (0, 0))
    mat = pl.BlockSpec((None, None, 2 * w, 2 * w), lambda l, i: (l, i, 0, 0))
    args = rows + cols + [bt, ct]
    return pl.pallas_call(
        _ssm_ops_kernel,
        grid=(depth, q),
        in_specs=[blk(a) for a in args] + [const, const],
        out_specs=[mat, mat, mat, pl.BlockSpec((None, None, 8, 2 * p), lambda l, i: (l, i, 0, 0))],
        out_shape=[jax.ShapeDtypeStruct((depth, q, 2 * w, 2 * w), MXU_DTYPE)] * 3
        + [jax.ShapeDtypeStruct((depth, q, 8, 2 * p), F32)],
        scratch_shapes=[pltpu.VMEM((2, w, w), F32)],
        compiler_params=_params(("parallel", "parallel")),
        name="ssm_ops",
    )(*args, *spread)


_PAIRS_PER_TILE = 4
_SSM_ROW_BLOCK = 64
_SCAN_ROW_PAD = 8


def _ssm_kernel(zu_ref, wt_ref, wb_ref, wc_ref, l16_ref, h0_ref, y_o, fin_o, u_ref, yp_ref, *state_refs,
                batch, seq_len):
    tc, npair = SSM_CHUNK, _PAIRS_PER_TILE
    n_chunks = seq_len // tc
    nrows = batch * n_chunks
    rb = _SSM_ROW_BLOCK
    rs = n_chunks + _SCAN_ROW_PAD
    slot_w = LANE // npair
    slot = jnp.right_shift(lax.broadcasted_iota(jnp.int32, (1, LANE), 1), 5)
    dx_refs, xs_refs = state_refs[:4], state_refs[4:]

    def place(pieces, src_slot):
        offset = src_slot
        out = None
        for j, piece in enumerate(pieces):
            shift = (slot_w * (j - offset[j])) % LANE
            r = piece if shift == 0 else pltpu.roll(piece, shift, 1)
            out = r if out is None else jnp.where(slot == j, r, out)
        return out

    def gather_block(i, carry):
        r0 = pl.multiple_of(i * rb, rb)
        steps = [zu_ref[pl.ds(r0 * tc + s, rb, stride=tc), :] for s in range(tc)]
        for p in range(npair):
            tiles = [place(steps[4 * k:4 * k + 4], [p] * 4) for k in range(tc // 4)]
            u_ref[p, pl.ds(r0, rb), :] = jnp.concatenate(tiles, axis=1).astype(u_ref.dtype)
        return carry

    lax.fori_loop(0, nrows // rb, gather_block, 0)

    for p in range(npair):
        u = u_ref[p]
        y_intra = jnp.dot(u, wt_ref[p], preferred_element_type=F32)
        dx = jnp.dot(u, wb_ref[p], preferred_element_type=F32)
        for r in range(4):
            for b in range(batch):
                dx_refs[r][b * rs:b * rs + n_chunks, :] = dx[b * n_chunks:(b + 1) * n_chunks, r * LANE:(r + 1) * LANE]
        lfr, lfi, lbr, lbi = (l16_ref[p, r:r + 1, :] for r in range(4))

        def body(c, carry):
            fr, fi, br, bi = carry
            fwd = pl.ds(c, batch, stride=rs)
            bwd = pl.ds(n_chunks - 1 - c, batch, stride=rs)
            xs_refs[0][fwd, :] = fr
            xs_refs[1][fwd, :] = fi
            xs_refs[2][bwd, :] = br
            xs_refs[3][bwd, :] = bi
            nfr = lfr * fr - lfi * fi + dx_refs[0][fwd, :]
            nfi = lfr * fi + lfi * fr + dx_refs[1][fwd, :]
            nbr = lbr * br - lbi * bi + dx_refs[2][bwd, :]
            nbi = lbr * bi + lbi * br + dx_refs[3][bwd, :]
            return nfr, nfi, nbr, nbi

        fin = lax.fori_loop(0, n_chunks, body, tuple(h0_ref[p, :, r * LANE:(r + 1) * LANE] for r in range(4)),
                            unroll=4)
        for r in range(4):
            fin_o[p, :, r * LANE:(r + 1) * LANE] = fin[r]
        xs = jnp.concatenate(
            [jnp.concatenate([x[b * rs:b * rs + n_chunks, :] for b in range(batch)], axis=0) for x in xs_refs],
            axis=1).astype(MXU_DTYPE)
        yp_ref[p] = y_intra + jnp.dot(xs, wc_ref[p], preferred_element_type=F32)

    def scatter_block(i, carry):
        r0 = pl.multiple_of(i * rb, rb)
        for k in range(tc // 4):
            pieces = [yp_ref[p, pl.ds(r0, rb), k * LANE:(k + 1) * LANE] for p in range(npair)]
            for j in range(4):
                y_o[pl.ds(r0 * tc + 4 * k + j, rb, stride=tc), :] = place(pieces, [j] * npair)
        return carry

    lax.fori_loop(0, nrows // rb, scatter_block, 0)


def _ssm_scan(zu2d, w_t, w_b, w_c, l16, h0, batch, seq_len):
    t = zu2d.shape[0]
    nrows = t // SSM_CHUNK
    npair = _PAIRS_PER_TILE
    assert nrows % _SSM_ROW_BLOCK == 0 and seq_len % SSM_CHUNK == 0
    once = pl.Buffered(1)
    wspec = pl.BlockSpec((npair, 512, 512), lambda i: (i, 0, 0))
    return pl.pallas_call(
        functools.partial(_ssm_kernel, batch=batch, seq_len=seq_len),
        grid=(SSM_PAIRS // npair,),
        in_specs=[
            pl.BlockSpec((t, LANE), lambda i: (0, i), pipeline_mode=once),
            wspec, wspec, wspec,
            pl.BlockSpec((npair, 8, LANE), lambda i: (i, 0, 0)),
            pl.BlockSpec((npair, batch, 512), lambda i: (i, 0, 0)),
        ],
        out_specs=[
            pl.BlockSpec((t, LANE), lambda i: (0, i), pipeline_mode=once),
            pl.BlockSpec((npair, batch, 512), lambda i: (i, 0, 0)),
        ],
        out_shape=[jax.ShapeDtypeStruct((t, SSM_WIDTH), F32),
                   jax.ShapeDtypeStruct((SSM_PAIRS, batch, 512), F32)],
        scratch_shapes=[pltpu.VMEM((npair, nrows, 512), MXU_DTYPE), pltpu.VMEM((npair, nrows, 512), F32)]
        + [pltpu.VMEM((batch * (seq_len // SSM_CHUNK + _SCAN_ROW_PAD), LANE), F32)] * 8,
        compiler_params=_params(("arbitrary",)),
        name="ssm_scan",
    )(zu2d, w_t, w_b, w_c, l16, h0)


def _pack_state(s_re, s_im):
    b = s_re.shape[0]
    a = jnp.stack([s_re, s_im], axis=2).reshape(b, 2, 2, SSM_PAIRS, 2 * SSM_STATE)
    return jnp.transpose(a, (3, 0, 1, 2, 4)).reshape(SSM_PAIRS, b, 4 * 2 * SSM_STATE).astype(F32)


def _unpack_state(fin):
    b = fin.shape[1]
    a = jnp.transpose(fin.reshape(SSM_PAIRS, b, 2, 2, 2 * SSM_STATE), (1, 2, 3, 0, 4))
    a = a.reshape(b, 2, 2, SSM_GROUPS, SSM_STATE)
    return a[:, :, 0], a[:, :, 1]


def _merge_kernel(ya_ref, ys_ref, u_ref, yc_ref, g_ref, x_ref, mod_ref, ng_ref, d_ref,
                  wglu_ref, wa_ref, wb_ref, wc_ref, wo_ref, x_o, h_o):
    y = ys_ref[...].astype(F32) + d_ref[...] * u_ref[...]
    gl = _gelu_tanh(y)
    yb = gl * _sigmoid(_mm(gl, wglu_ref[...]))
    merged = (g_ref[:, 0:D_MODEL].astype(F32) * _mm(ya_ref[...], wa_ref[...])
              + g_ref[:, D_MODEL:2 * D_MODEL].astype(F32) * _mm(yb, wb_ref[...])
              + g_ref[:, 2 * D_MODEL:3 * D_MODEL].astype(F32) * _mm(yc_ref[...], wc_ref[...]))
    x1 = x_ref[...] + mod_ref[2:3, :] * _rms(_mm(merged, wo_ref[...]), ng_ref[1:2, :])
    x_o[...] = x1
    h_o[...] = (_rms(x1, ng_ref[2:3, :]) * (1.0 + mod_ref[4:5, :]) + mod_ref[3:4, :]).astype(h_o.dtype)


def _merge(ya, ys, zu, yc, gates, x2d, mod_l, mod_row, ng, ssm_d, w_glu, w_a, w_b, w_c, w_o):
    t = x2d.shape[0]
    tm = TOKEN_TILE
    row = lambda i: (i, 0)
    const = lambda i: (0, 0)
    r512 = pl.BlockSpec((tm, 512), row)
    wbr = pl.BlockSpec((512, D_MODEL), const)
    return pl.pallas_call(
        _merge_kernel,
        grid=(t // tm,),
        in_specs=[
            r512, r512, r512, r512,
            pl.BlockSpec((tm, 3 * D_MODEL), row),
            pl.BlockSpec((tm, D_MODEL), row),
            pl.BlockSpec((None, 6, D_MODEL), lambda i: (mod_row(i, tm), 0, 0)),
            pl.BlockSpec((4, D_MODEL), const),
            pl.BlockSpec((1, 512), const),
            pl.BlockSpec((512, 512), const),
            wbr, wbr, wbr,
            pl.BlockSpec((D_MODEL, D_MODEL), const),
        ],
        out_specs=[pl.BlockSpec((tm, D_MODEL), row), pl.BlockSpec((tm, D_MODEL), row)],
        out_shape=[jax.ShapeDtypeStruct((t, D_MODEL), F32), jax.ShapeDtypeStruct((t, D_MODEL), MXU_DTYPE)],
        compiler_params=_params(("parallel",)),
        name="merge",
    )(ya, ys, zu, yc, gates, x2d, mod_l, ng, ssm_d, w_glu, w_a, w_b, w_c, w_o)


def _ffn_kernel(*refs, seq_len, first, last):
    refs = list(refs)
    h_ref, hp_ref, hn_ref = refs[:3]
    del refs[:3]
    part_ref = None if first else refs.pop(0)
    if last:
        x_ref, mod_ref, ng_ref = refs[:3]
        del refs[:3]
    wa_ref, wg_ref, cwa_ref, cwg_ref, cba_ref, cbg_ref, wd_ref, out_ref, ua_ref, ug_ref, act_ref = refs
    i = pl.program_id(0)
    tm = h_ref.shape[0]
    ft = wd_ref.shape[0]
    n = tm + 16
    start = jnp.bitwise_and(i * tm, seq_len - 1)
    keep_prev = (start != 0).astype(F32)
    keep_next = (jnp.bitwise_and(start + tm, seq_len - 1) != 0).astype(F32)
    hh = jnp.concatenate([(hp_ref[...] * keep_prev).astype(h_ref.dtype), h_ref[...],
                          (hn_ref[...] * keep_next).astype(h_ref.dtype)], axis=0)
    interior = tm > seq_len
    if interior:
        pos = jnp.bitwise_and(lax.broadcasted_iota(jnp.int32, (tm, 1), 0), seq_len - 1)
        has_prev = (pos != 0).astype(F32)
        has_next = (pos != seq_len - 1).astype(F32)

    def conv(u_ref, cw_ref, cb_ref, lo, col):
        uc = u_ref[:, lo:lo + LANE]
        up = pltpu.roll(uc, 1, 0)[8:8 + tm]
        un = pltpu.roll(uc, n - 1, 0)[8:8 + tm]
        if interior:
            up, un = up * has_prev, un * has_next
        return (cw_ref[0:1, col:col + LANE] * up + cw_ref[1:2, col:col + LANE] * uc[8:8 + tm]
                + cw_ref[2:3, col:col + LANE] * un + cb_ref[0:1, col:col + LANE])

    for c0 in range(0, ft, _FF_SUB):
        wsub = min(_FF_SUB, ft - c0)
        ua_ref[:, :wsub] = jnp.dot(hh, wa_ref[:, c0:c0 + wsub], preferred_element_type=F32)
        ug_ref[:, :wsub] = jnp.dot(hh, wg_ref[:, c0:c0 + wsub], preferred_element_type=F32)
        for lo in range(0, wsub, LANE):
            a = conv(ua_ref, cwa_ref, cba_ref, lo, c0 + lo)
            g = conv(ug_ref, cwg_ref, cbg_ref, lo, c0 + lo)
            act_ref[:, c0 + lo:c0 + lo + LANE] = (g * _sigmoid(g) * a).astype(act_ref.dtype)
    total = jnp.dot(act_ref[...], wd_ref[...], preferred_element_type=F32)
    if not first:
        total = total + part_ref[...]
    if last:
        out_ref[...] = x_ref[...] + mod_ref[5:6, :] * _rms(total, ng_ref[3:4, :])
    else:
        out_ref[...] = total


def _conv_ffn(h2, x1, mod_l, mod_row, ng, w_up, conv_w, conv_b, w_down, seq_len):
    t = x1.shape[0]
    tm = FFN_TOKEN_TILE
    ft = FF_TILE
    nf = D_FF // ft
    nblk8 = t // 8
    assert seq_len & (seq_len - 1) == 0 and (seq_len % tm == 0 or tm % seq_len == 0) and t % tm == 0
    row = pl.BlockSpec((tm, D_MODEL), lambda i: (i, 0))
    once = pl.Buffered(1)
    part = None
    for j in range(nf):
        first, last = j == 0, j == nf - 1
        in_specs = [
            row,
            pl.BlockSpec((8, D_MODEL), lambda i: (jnp.maximum(i * (tm // 8) - 1, 0), 0)),
            pl.BlockSpec((8, D_MODEL), lambda i: (jnp.minimum((i + 1) * (tm // 8), nblk8 - 1), 0)),
        ]
        args = [h2, h2, h2]
        if not first:
            in_specs.append(row)
            args.append(part)
        if last:
            in_specs += [row, pl.BlockSpec((None, 6, D_MODEL), lambda i: (mod_row(i, tm), 0, 0)),
                         pl.BlockSpec((4, D_MODEL), lambda i: (0, 0))]
            args += [x1, mod_l, ng]
        in_specs += [
            pl.BlockSpec((D_MODEL, ft), lambda i, j=j: (0, j), pipeline_mode=once),
            pl.BlockSpec((D_MODEL, ft), lambda i, j=j: (0, nf + j), pipeline_mode=once),
            pl.BlockSpec((3, ft), lambda i, j=j: (0, j)),
            pl.BlockSpec((3, ft), lambda i, j=j: (0, nf + j)),
            pl.BlockSpec((1, ft), lambda i, j=j: (0, j)),
            pl.BlockSpec((1, ft), lambda i, j=j: (0, nf + j)),
            pl.BlockSpec((ft, D_MODEL), lambda i, j=j: (j, 0), pipeline_mode=once),
        ]
        args += [w_up, w_up, conv_w, conv_w, conv_b, conv_b, w_down]
        part = pl.pallas_call(
            functools.partial(_ffn_kernel, seq_len=seq_len, first=first, last=last),
            grid=(t // tm,),
            in_specs=in_specs,
            out_specs=row,
            out_shape=jax.ShapeDtypeStruct((t, D_MODEL), F32),
            scratch_shapes=[pltpu.VMEM((tm + 16, _FF_SUB), F32), pltpu.VMEM((tm + 16, _FF_SUB), F32),
                            pltpu.VMEM((tm, ft), MXU_DTYPE)],
            compiler_params=_params(("parallel",)),
            name="conv_ffn_last" if last else "conv_ffn_part",
        )(*args)
    return part


_Q_HEAD_ORDER = (0, 4, 1, 5, 2, 6, 3, 7)


def _rope_tables(seq_len):
    nf = HEAD_DIM // 4
    t = np.arange(seq_len)
    pos = np.stack([t // GRID_W, t % GRID_W]).astype(np.float32)
    inv = jnp.asarray(ROPE_THETA, F32) ** (-jnp.arange(nf, dtype=F32) / nf)
    ang = jnp.asarray(pos)[:, :, None] * inv
    d = np.arange(HEAD_DIM)
    ang = ang[d // (2 * nf), :, d % nf].T
    second = jnp.asarray(((d % (2 * nf)) // nf) == 1)[None, :]
    cos, sin = jnp.cos(ang), jnp.sin(ang)
    tabs = (cos, jnp.where(second, 0.0, -sin), jnp.where(second, sin, 0.0))
    return tuple(jnp.tile(x, (1, LANE // HEAD_DIM)).astype(F32) for x in tabs)


def _layer_weights(w_in, qk_g, w_br_a):
    hd = HEAD_DIM
    w_in_p = jnp.concatenate([w_in[:, h * hd:(h + 1) * hd] for h in _Q_HEAD_ORDER] + [w_in[:, 512:]],
                             axis=1).astype(MXU_DTYPE)
    w_a_p = jnp.concatenate([w_br_a[h * hd:(h + 1) * hd] for h in _Q_HEAD_ORDER], axis=0).astype(MXU_DTYPE)
    qg = jnp.tile(qk_g[0], N_HEADS).reshape(1, 512).astype(F32)
    kg = jnp.tile(qk_g[1], GA_KV_HEADS).reshape(1, LANE).astype(F32)
    return w_in_p, w_a_p, qg, kg


def kernel(x_prompt, x_sample, c, cache_ga_k, cache_ga_v, cache_na_k, cache_na_v, state_ssm_re, state_ssm_im,
           c_ctx, w_mod, b_mod, norm_g, w_in, qk_norm_g, na_rpb, ssm_lam_re, ssm_lam_im, ssm_log_step,
           ssm_b_re, ssm_b_im, ssm_c_re, ssm_c_im, ssm_d, w_glu, w_br_a, w_br_b, w_br_c, w_out,
           w_up, conv_w, conv_b, w_down):
    depth = w_in.shape[0]
    bp, lp, _ = x_prompt.shape
    bs, ls, _ = x_sample.shape
    lc = cache_ga_k.shape[2]
    assert lp % 256 == 0 and ls % FFN_TOKEN_TILE == 0 and (bp * lp) % FFN_TOKEN_TILE == 0
    assert FFN_TOKEN_TILE % TOKEN_TILE == 0
    assert bs % 8 == 0 and bp % 8 == 0, "the scan keeps one batch row per sublane"

    rows = 1 + bs
    rows_p = -(-rows // 8) * 8
    cvec = jnp.concatenate([c_ctx[None], c, jnp.zeros((rows_p - rows, D_MODEL), F32)], axis=0)
    mod = _modulation(cvec, w_mod, b_mod).reshape(depth, rows_p, 6, D_MODEL)

    w_t, w_b, w_c, l16 = _ssm_operators(ssm_lam_re, ssm_lam_im, ssm_log_step, ssm_b_re, ssm_b_im,
                                        ssm_c_re, ssm_c_im)
    seg = jnp.asarray(np.kron(np.eye(N_HEADS), np.full((HEAD_DIM, HEAD_DIM), 1.0 / HEAD_DIM)), MXU_DTYPE)
    rope_tabs = _rope_tables(ls)
    ctx_row = lambda i, tm: 0
    lat_row = lambda i, tm: 1 + (i * tm) // ls

    y_p = x_prompt.reshape(bp * lp, D_MODEL)
    y_s = x_sample.reshape(bs * ls, D_MODEL)
    zero_state = jnp.zeros((SSM_PAIRS, bp, 512), F32)
    cache_ga = [a.reshape(bs, depth, lc, LANE) for a in (cache_ga_k, cache_ga_v)]
    cache_na = [a.reshape(bs, depth, lc, 512) for a in (cache_na_k, cache_na_v)]
    states = ([], [])
    kv_stacks = tuple(jnp.zeros((bp, depth, lp, w), F32) for w in (LANE, LANE, 512, 512))
    for l in range(depth):
        w_in_p, w_a_p, qg, kg = _layer_weights(w_in[l], qk_norm_g[l], w_br_a[l])
        w_glu_l, w_b_l, w_c_l, w_o_l = (a[l].astype(MXU_DTYPE) for a in (w_glu, w_br_b, w_br_c, w_out))
        ffn_w = (w_up[l].astype(MXU_DTYPE), conv_w[l].astype(F32), conv_b[l].reshape(1, 2 * D_FF).astype(F32),
                 w_down[l].astype(MXU_DTYPE))
        d_l = ssm_d[l].reshape(1, SSM_WIDTH).astype(F32)
        ng = norm_g[l].astype(F32)
        ssm_ops = (w_t[l], w_b[l], w_c[l], l16[l])

        q, k, v, zu, nq, nk, nv, gates = _in_projection(
            y_p, mod[l], ctx_row, ng, w_in_p, qg, kg, seg, None, lp, F32, stack=(l, depth, kv_stacks))
        kv_stacks = (k, v, nk, nv)
        r3 = lambda a: a.reshape(bp, lp, a.shape[-1])
        ya = _attention(r3(q), k, v, None, None, lp, "ga_ctx", kv_layer=l)
        yc = _attention(r3(nq), nk, nv, None, None, lp, "na_ctx", kv_layer=l)
        ys, fin = _ssm_scan(zu, *ssm_ops, zero_state, bp, lp)
        x1, h2 = _merge(ya.reshape(-1, 512), ys, zu, yc.reshape(-1, 512), gates, y_p,
                        mod[l], ctx_row, ng, d_l, w_glu_l, w_a_p, w_b_l, w_c_l, w_o_l)
        y_p = _conv_ffn(h2, x1, mod[l], ctx_row, ng, *ffn_w, lp)
        f_re, f_im = _unpack_state(fin)
        states[0].append(f_re)
        states[1].append(f_im)

        q, k, v, zu, nq, nk, nv, gates = _in_projection(
            y_s, mod[l], lat_row, ng, w_in_p, qg, kg, seg, rope_tabs, ls, MXU_DTYPE)
        r3 = lambda a: a.reshape(bs, ls, a.shape[-1])
        ya = _attention(r3(q), r3(k), r3(v), cache_ga[0], cache_ga[1], 4 * GRID_W, "ga_lat", cache_layer=l)
        yc = _neighbourhood_attention(r3(nq), r3(nk), r3(nv), cache_na[0], cache_na[1], l, na_rpb[l])
        h0 = _pack_state(state_ssm_re[:, l], state_ssm_im[:, l])
        ys, _ = _ssm_scan(zu, *ssm_ops, h0, bs, ls)
        x1, h2 = _merge(ya.reshape(-1, 512), ys, zu, yc.reshape(-1, 512), gates, y_s,
                        mod[l], lat_row, ng, d_l, w_glu_l, w_a_p, w_b_l, w_c_l, w_o_l)
        y_s = _conv_ffn(h2, x1, mod[l], lat_row, ng, *ffn_w, ls)

    k, v, nk, nv = kv_stacks
    return (y_p.reshape(bp, lp, D_MODEL), y_s.reshape(bs, ls, D_MODEL),
            k.reshape(bp, depth, lp, GA_KV_HEADS, HEAD_DIM), v.reshape(bp, depth, lp, GA_KV_HEADS, HEAD_DIM),
            nk.reshape(bp, depth, lp, N_HEADS, HEAD_DIM), nv.reshape(bp, depth, lp, N_HEADS, HEAD_DIM),
            jnp.stack(states[0], axis=1), jnp.stack(states[1], axis=1))
```

```python
import functools
import math

import numpy as np
import jax
import jax.numpy as jnp
from jax import lax
from jax.experimental import pallas as pl
from jax.experimental.pallas import tpu as pltpu

F32 = jnp.float32
MXU_DTYPE = jnp.bfloat16

D_MODEL = 1024
HEAD_DIM = 64
N_HEADS = 8
GA_KV_HEADS = 2
GRID_W = 64
NA_WIN_ROWS = 8
NA_WIN_COLS = 16
NA_KEY_ROWS = 10
SSM_WIDTH = 512
SSM_GROUPS = 32
SSM_GROUP_CH = 16
SSM_STATE = 64
SSM_CHUNK = 16
SSM_PAIRS = SSM_GROUPS // 2
D_FF = 2816
FF_TILE = 1408
ROPE_THETA = 10000.0
EPS = 1e-6
IN_WIDTH = 5888
NEG_BIG = -1e30

LANE = 128
TOKEN_TILE = 512
FFN_TOKEN_TILE = 1024
VMEM_LIMIT = 56 * 1024 * 1024

_Q0, _K0, _V0, _U0, _NQ0, _NK0, _NV0, _G0 = 0, 512, 640, 768, 1280, 1792, 2304, 2816


def _sigmoid(x):
    return 1.0 / (1.0 + jnp.exp(-x))


def _gelu_tanh(x):
    return 0.5 * x * (1.0 + jnp.tanh(math.sqrt(2.0 / math.pi) * (x + 0.044715 * (x * x * x))))


def _rms(x, g):
    ms = jnp.mean(x * x, axis=-1, keepdims=True)
    return (x * lax.rsqrt(ms + EPS)) * g


def _mm(a, b):
    return jnp.dot(a.astype(MXU_DTYPE), b.astype(MXU_DTYPE), preferred_element_type=F32)


def _mm_nt(a, b):
    return lax.dot_general(a.astype(MXU_DTYPE), b.astype(MXU_DTYPE), (((1,), (1,)), ((), ())),
                           preferred_element_type=F32)


def _params(sem):
    return pltpu.CompilerParams(dimension_semantics=sem, vmem_limit_bytes=VMEM_LIMIT)


def _mod_kernel(c_ref, w_ref, b_ref, o_ref):
    c = c_ref[...]
    o_ref[...] = _mm(c * _sigmoid(c), w_ref[...]) + b_ref[...]


def _modulation(cvec, w_mod, b_mod):
    depth = w_mod.shape[0]
    rows = cvec.shape[0]
    tn = 1536
    return pl.pallas_call(
        _mod_kernel,
        grid=(depth, 6 * D_MODEL // tn),
        in_specs=[
            pl.BlockSpec((rows, D_MODEL), lambda l, j: (0, 0)),
            pl.BlockSpec((None, D_MODEL, tn), lambda l, j: (l, 0, j)),
            pl.BlockSpec((None, 1, tn), lambda l, j: (l, 0, j)),
        ],
        out_specs=pl.BlockSpec((None, rows, tn), lambda l, j: (l, 0, j)),
        out_shape=jax.ShapeDtypeStruct((depth, rows, 6 * D_MODEL), F32),
        compiler_params=_params(("parallel", "parallel")),
        name="adaln_mod",
    )(cvec, w_mod, b_mod.reshape(depth, 1, 6 * D_MODEL))


def _head_rms(z, seg, gain):
    ms = jnp.dot((z * z).astype(MXU_DTYPE), seg, preferred_element_type=F32)
    return (z * lax.rsqrt(ms + EPS)) * gain


def _rope_tile(t, c, s_up, s_dn):
    return t * c + pltpu.roll(t, LANE - 16, 1) * s_up + pltpu.roll(t, 16, 1) * s_dn


def _inproj_kernel(*refs, rope, n_carried):
    n_in = 10 if rope else 7
    refs = refs[:n_in] + refs[n_in + n_carried:]
    if rope:
        (x_ref, mod_ref, ng_ref, w_ref, qg_ref, kg_ref, seg_ref, cos_ref, sup_ref, sdn_ref,
         q_o, k_o, v_o, u_o, nq_o, nk_o, nv_o, g_o) = refs
    else:
        (x_ref, mod_ref, ng_ref, w_ref, qg_ref, kg_ref, seg_ref,
         q_o, k_o, v_o, u_o, nq_o, nk_o, nv_o, g_o) = refs
    x = x_ref[...]
    h = _rms(x, ng_ref[0:1, :]) * (1.0 + mod_ref[1:2, :]) + mod_ref[0:1, :]
    hb = h.astype(MXU_DTYPE)
    scale = HEAD_DIM ** -0.5

    def proj(lo, width):
        return jnp.dot(hb, w_ref[:, lo:lo + width], preferred_element_type=F32)

    def maybe_rope(z):
        if not rope:
            return z
        c, su, sd = cos_ref[...], sup_ref[...], sdn_ref[...]
        tiles = [_rope_tile(z[:, i * LANE:(i + 1) * LANE], c, su, sd) for i in range(z.shape[1] // LANE)]
        return tiles[0] if len(tiles) == 1 else jnp.concatenate(tiles, axis=1)

    q = maybe_rope(_head_rms(proj(_Q0, 512), seg_ref[...], qg_ref[...]))
    q_o[...] = (q * scale).astype(q_o.dtype)
    k = maybe_rope(_head_rms(proj(_K0, 128), seg_ref[0:LANE, 0:LANE], kg_ref[...]))
    k_o[...] = k.astype(k_o.dtype).reshape(k_o.shape)
    v_o[...] = proj(_V0, 128).astype(v_o.dtype).reshape(v_o.shape)
    u_o[...] = proj(_U0, 512).astype(u_o.dtype)
    nq_o[...] = (proj(_NQ0, 512) * scale).astype(nq_o.dtype)
    nk_o[...] = proj(_NK0, 512).astype(nk_o.dtype).reshape(nk_o.shape)
    nv_o[...] = proj(_NV0, 512).astype(nv_o.dtype).reshape(nv_o.shape)
    for i in range(3):
        g_o[:, i * D_MODEL:(i + 1) * D_MODEL] = _sigmoid(proj(_G0 + i * D_MODEL, D_MODEL)).astype(g_o.dtype)


def _in_projection(x2d, mod_l, mod_row, ng, w_in, qg, kg, seg, rope_tabs, seq_len, kv_dtype, stack=None):
    t = x2d.shape[0]
    tm = TOKEN_TILE
    tiles_per_seq = max(seq_len // tm, 1)
    rope = rope_tabs is not None
    row = lambda i: (i, 0)
    const = lambda i: (0, 0)
    in_specs = [
        pl.BlockSpec((tm, D_MODEL), row),
        pl.BlockSpec((None, 6, D_MODEL), lambda i: (mod_row(i, tm), 0, 0)),
        pl.BlockSpec((4, D_MODEL), const),
        pl.BlockSpec((D_MODEL, IN_WIDTH), const),
        pl.BlockSpec((1, 512), const),
        pl.BlockSpec((1, LANE), const),
        pl.BlockSpec((512, 512), const),
    ]
    args = [x2d, mod_l, ng, w_in, qg, kg, seg]
    if rope:
        in_specs += [pl.BlockSpec((tm, LANE), lambda i: (i % tiles_per_seq, 0))] * 3
        args += list(rope_tabs)
    widths = (512, 128, 128, 512, 512, 512, 512, 3 * D_MODEL)
    dtypes = (MXU_DTYPE, kv_dtype, kv_dtype, F32, MXU_DTYPE, kv_dtype, kv_dtype, MXU_DTYPE)
    out_specs = [pl.BlockSpec((tm, w), row) for w in widths]
    out_shape = [jax.ShapeDtypeStruct((t, w), dt) for w, dt in zip(widths, dtypes)]
    aliases, n_carried = {}, 0
    if stack is not None:
        layer, depth, carried = stack
        assert tm % seq_len == 0
        for o in (1, 2, 5, 6):
            out_specs[o] = pl.BlockSpec((tm // seq_len, None, seq_len, widths[o]), lambda i: (i, layer, 0, 0))
            out_shape[o] = jax.ShapeDtypeStruct((t // seq_len, depth, seq_len, widths[o]), dtypes[o])
        if carried is not None:
            n_carried = len(carried)
            aliases = {len(args) + n: o for n, o in enumerate((1, 2, 5, 6))}
            in_specs += [pl.BlockSpec(memory_space=pl.ANY)] * n_carried
            args += list(carried)
    return pl.pallas_call(
        functools.partial(_inproj_kernel, rope=rope, n_carried=n_carried),
        grid=(t // tm,),
        in_specs=in_specs,
        out_specs=out_specs,
        out_shape=out_shape,
        input_output_aliases=aliases,
        compiler_params=_params(("parallel",)),
        name="in_proj_rope" if rope else "in_proj",
    )(*args)


def _lane_masks(dtype):
    lane = lax.broadcasted_iota(jnp.int32, (1, LANE), 1)
    lo = lane < HEAD_DIM
    return lo, lo.astype(dtype), (~lo).astype(dtype)


_NA_TILES_PER_STEP = 2
_KEY_BLOCK = 256
_Q_SUB = 128


def _softmax_pv(qs, key_blocks, s_ref):
    macc = None
    for bi, (score_fn, _) in enumerate(key_blocks):
        sj = score_fn(qs)
        s_ref[bi] = sj
        macc = sj if macc is None else jnp.maximum(macc, sj)
    mb = jnp.broadcast_to(jnp.max(macc, axis=-1, keepdims=True), macc.shape)
    lacc = jnp.zeros(macc.shape, F32)
    o = jnp.zeros((qs.shape[0], LANE), F32)
    for bi, (_, v_fn) in enumerate(key_blocks):
        p = jnp.exp(s_ref[bi] - mb)
        lacc = lacc + p
        vb = v_fn()
        o = o + _mm(p[:, :vb.shape[0]], vb)
    return o * (1.0 / jnp.sum(lacc, axis=-1, keepdims=True))


def _attn_kernel(*refs, kv_tiles, cached):
    if cached:
        q_ref, k_ref, v_ref, kc_ref, vc_ref, o_ref, s_ref = refs
    else:
        q_ref, k_ref, v_ref, o_ref, s_ref = refs
    tq = q_ref.shape[0]
    kb = _KEY_BLOCK
    lo, m_lo, m_hi = _lane_masks(MXU_DTYPE)
    pairs_per_kv = (N_HEADS // 2) // kv_tiles
    sources = [(k_ref, v_ref)] + ([(kc_ref, vc_ref)] if cached else [])
    for q0 in range(0, tq, _Q_SUB):
        for hp in range(N_HEADS // 2):
            ksl = slice((hp // pairs_per_kv) * LANE, (hp // pairs_per_kv + 1) * LANE)
            q2 = q_ref[q0:q0 + _Q_SUB, hp * LANE:(hp + 1) * LANE]
            qs = jnp.concatenate([q2 * m_lo, q2 * m_hi], axis=0)
            blocks = [(functools.partial(lambda x, kr, off, ksl: _mm_nt(x, kr[off:off + kb, ksl]),
                                         kr=kr, off=off, ksl=ksl),
                       functools.partial(lambda vr, off, ksl: vr[off:off + kb, ksl], vr=vr, off=off, ksl=ksl))
                      for kr, vr in sources for off in range(0, kr.shape[0], kb)]
            o = _softmax_pv(qs, blocks, s_ref)
            o_ref[q0:q0 + _Q_SUB, hp * LANE:(hp + 1) * LANE] = jnp.where(
                lo, o[:_Q_SUB], o[_Q_SUB:]).astype(o_ref.dtype)


def _attention(q, k, v, kc, vc, tq, name, kv_layer=None, cache_layer=None):
    b, lq, _ = q.shape
    lk, kw = k.shape[-2], k.shape[-1]
    cached = kc is not None
    qmap = lambda bi, ti: (bi, ti, 0)

    def key_spec(n, layer):
        if layer is None:
            return pl.BlockSpec((None, n, kw), lambda bi, ti: (bi, 0, 0))
        return pl.BlockSpec((None, None, n, kw), lambda bi, ti: (bi, layer, 0, 0))

    in_specs = [pl.BlockSpec((None, tq, 512), qmap)] + [key_spec(lk, kv_layer)] * 2
    args = [q, k, v]
    lc = kc.shape[-2] if cached else 0
    if cached:
        in_specs += [key_spec(lc, cache_layer)] * 2
        args += [kc, vc]
    assert lk % _KEY_BLOCK == 0 and lc % _KEY_BLOCK == 0 and tq % _Q_SUB == 0
    n_blocks = (lk + lc) // _KEY_BLOCK
    return pl.pallas_call(
        functools.partial(_attn_kernel, kv_tiles=kw // LANE, cached=cached),
        grid=(b, lq // tq),
        in_specs=in_specs,
        out_specs=pl.BlockSpec((None, tq, 512), qmap),
        out_shape=jax.ShapeDtypeStruct((b, lq, 512), MXU_DTYPE),
        scratch_shapes=[pltpu.VMEM((n_blocks, 2 * _Q_SUB, _KEY_BLOCK), F32)],
        compiler_params=_params(("parallel", "parallel")),
        name=name,
    )(*args)


def _na_geometry(seq_len):
    rows = seq_len // GRID_W
    n_tiles = rows // 2
    assert rows >= NA_KEY_ROWS and NA_WIN_ROWS <= rows and NA_KEY_ROWS % 2 == 0
    ws = np.clip(2 * np.arange(n_tiles) - NA_WIN_ROWS // 2, 0, rows - NA_KEY_ROWS)
    r = 2 * np.arange(n_tiles)[:, None, None] + np.arange(2)[None, :, None]
    key_r = ws[:, None, None] + np.arange(NA_KEY_ROWS)[None, None, :]
    r0 = np.clip(r - NA_WIN_ROWS // 2, 0, rows - NA_WIN_ROWS)
    valid = (key_r >= r0) & (key_r < r0 + NA_WIN_ROWS)
    dr = np.where(valid, key_r - r + NA_WIN_ROWS - 1, 2 * NA_WIN_ROWS - 1)
    assert (valid.sum(-1) == NA_WIN_ROWS).all()
    return ws.astype(np.int32), dr.reshape(-1).astype(np.int32)


def _na_bias_blocks(rpb):
    h = rpb.shape[0]
    nrel = 2 * NA_WIN_ROWS - 1
    zeros = jnp.zeros((h, nrel, LANE - (2 * NA_WIN_COLS - 1)), F32)
    v = jnp.concatenate([rpb[..., NA_WIN_COLS - 1:], zeros, rpb[..., :NA_WIN_COLS - 1]], axis=-1).astype(F32)
    t = jnp.tile(v, (1, 1, GRID_W))[..., :GRID_W * (LANE - 1)].reshape(h, nrel, GRID_W, LANE - 1)[..., :GRID_W]
    c = np.arange(GRID_W)
    c0 = np.clip(c - NA_WIN_COLS // 2, 0, GRID_W - NA_WIN_COLS)
    colmask = (c[None, :] >= c0[:, None]) & (c[None, :] < c0[:, None] + NA_WIN_COLS)
    t = jnp.where(jnp.asarray(colmask)[None, None], t, NEG_BIG)
    t = jnp.concatenate([t, jnp.full((h, 1, GRID_W, GRID_W), NEG_BIG, F32)], axis=1)
    pad = jnp.zeros_like(t)
    return jnp.concatenate([t, pad], axis=-1), jnp.concatenate([pad, t], axis=-1)


def _na_kernel(ws_ref, dr_ref, q_ref, k_ref, v_ref, kc_ref, vc_ref, bl_ref, br_ref, o_ref, s_ref):
    for tt in range(_NA_TILES_PER_STEP):
        _na_tile(pl.program_id(1) * _NA_TILES_PER_STEP + tt, tt * 2 * GRID_W,
                 ws_ref, dr_ref, q_ref, k_ref, v_ref, kc_ref, vc_ref, bl_ref, br_ref, o_ref, s_ref)


def _na_tile(i, q0, ws_ref, dr_ref, q_ref, k_ref, v_ref, kc_ref, vc_ref, bl_ref, br_ref, o_ref, s_ref):
    start = pl.multiple_of(ws_ref[i] * GRID_W, GRID_W)
    nk = NA_KEY_ROWS * GRID_W
    kb = _KEY_BLOCK
    tq = 2 * GRID_W
    lc = kc_ref.shape[0]
    lo, m_lo, m_hi = _lane_masks(MXU_DTYPE)

    def bias_block(hp, off, width):
        rows = []
        for h in (2 * hp, 2 * hp + 1):
            for qr in range(2):
                base = (i * 2 + qr) * NA_KEY_ROWS + off // GRID_W
                tiles = [bl_ref[h, dr_ref[base + 2 * kp]] + br_ref[h, dr_ref[base + 2 * kp + 1]]
                         for kp in range(width // LANE)]
                rows.append(tiles[0] if len(tiles) == 1 else jnp.concatenate(tiles, axis=1))
        return jnp.concatenate(rows, axis=0)

    def local_scores(x, hp, sl, off, width):
        s = _mm_nt(x, k_ref[pl.ds(start + off, width), sl]) + bias_block(hp, off, width)
        if width < kb:
            s = jnp.concatenate([s, jnp.full((s.shape[0], kb - width), NEG_BIG, F32)], axis=1)
        return s

    for hp in range(N_HEADS // 2):
        sl = slice(hp * LANE, (hp + 1) * LANE)
        q2 = q_ref[q0:q0 + tq, sl]
        qs = jnp.concatenate([q2 * m_lo, q2 * m_hi], axis=0)
        blocks = []
        for off in range(0, nk, kb):
            width = min(kb, nk - off)
            blocks.append((functools.partial(local_scores, hp=hp, sl=sl, off=off, width=width),
                           functools.partial(lambda sl, off, width: v_ref[pl.ds(start + off, width), sl],
                                             sl=sl, off=off, width=width)))
        for off in range(0, lc, kb):
            blocks.append((functools.partial(lambda x, sl, off: _mm_nt(x, kc_ref[off:off + kb, sl]), sl=sl, off=off),
                           functools.partial(lambda sl, off: vc_ref[off:off + kb, sl], sl=sl, off=off)))
        o = _softmax_pv(qs, blocks, s_ref)
        o_ref[q0:q0 + tq, sl] = jnp.where(lo, o[:tq], o[tq:]).astype(o_ref.dtype)


def _neighbourhood_attention(q, k, v, kc, vc, cache_layer, rpb):
    b, seq_len, _ = q.shape
    lc = kc.shape[-2]
    ws, dr = _na_geometry(seq_len)
    b_left, b_right = _na_bias_blocks(rpb)
    n_tiles = len(ws)
    assert n_tiles % _NA_TILES_PER_STEP == 0
    tq = 2 * GRID_W
    tb = tq * _NA_TILES_PER_STEP
    qmap = lambda bi, ti, ws_r, dr_r: (bi, ti, 0)
    kmap = lambda bi, ti, ws_r, dr_r: (bi, 0, 0)
    bmap = lambda bi, ti, ws_r, dr_r: (0, 0, 0, 0)
    grid_spec = pltpu.PrefetchScalarGridSpec(
        num_scalar_prefetch=2,
        grid=(b, n_tiles // _NA_TILES_PER_STEP),
        in_specs=[
            pl.BlockSpec((None, tb, 512), qmap),
            pl.BlockSpec((None, seq_len, 512), kmap),
            pl.BlockSpec((None, seq_len, 512), kmap),
            pl.BlockSpec((None, None, lc, 512), lambda bi, ti, ws_r, dr_r: (bi, cache_layer, 0, 0)),
            pl.BlockSpec((None, None, lc, 512), lambda bi, ti, ws_r, dr_r: (bi, cache_layer, 0, 0)),
            pl.BlockSpec(b_left.shape, bmap),
            pl.BlockSpec(b_right.shape, bmap),
        ],
        out_specs=pl.BlockSpec((None, tb, 512), qmap),
        scratch_shapes=[pltpu.VMEM((-(-NA_KEY_ROWS * GRID_W // _KEY_BLOCK) + lc // _KEY_BLOCK, 2 * tq, _KEY_BLOCK),
                                   F32)],
    )
    assert lc % _KEY_BLOCK == 0
    return pl.pallas_call(
        _na_kernel,
        grid_spec=grid_spec,
        out_shape=jax.ShapeDtypeStruct((b, seq_len, 512), MXU_DTYPE),
        compiler_params=_params(("parallel", "arbitrary")),
        name="na_attn",
    )(jnp.asarray(ws), jnp.asarray(dr), q, k, v, kc, vc, b_left, b_right)


def _cmul(ar, ai, br, bi):
    return ar * br - ai * bi, ar * bi + ai * br


def _lam_bar(lr, li, ls):
    dt = jnp.exp(ls)
    mag = jnp.exp(lr * dt)
    return mag * jnp.cos(li * dt), mag * jnp.sin(li * dt)


def _zoh_coef(lr, li, zr, zi):
    nr, ni = zr - 1.0, zi
    den = 1.0 / (lr * lr + li * li)
    return (nr * lr + ni * li) * den, (ni * lr - nr * li) * den


def _squarings(zr, zi, n):
    out = [(zr, zi)]
    for _ in range(n - 1):
        zr, zi = _cmul(zr, zi, zr, zi)
        out.append((zr, zi))
    return out


def _cpow(squares, e):
    pr, pi = jnp.ones(e.shape, F32), jnp.zeros(e.shape, F32)
    for k, (zr, zi) in enumerate(squares):
        bit = jnp.bitwise_and(jnp.right_shift(e, k), 1) == 1
        nr, ni = _cmul(pr, pi, zr, zi)
        pr, pi = jnp.where(bit, nr, pr), jnp.where(bit, ni, pi)
    return pr, pi


def _ssm_ops_kernel(lrr_ref, lir_ref, lsr_ref, lrc_ref, lic_ref, lsc_ref, bt_ref, ct_ref, s0_ref, s1_ref,
                    wt_o, wb_o, wc_o, l16_o, tg_ref):
    tc, hg, p = SSM_CHUNK, SSM_GROUP_CH, SSM_STATE
    w = tc * hg
    lane_w = lax.broadcasted_iota(jnp.int32, (1, w), 1)
    lane_p = lax.broadcasted_iota(jnp.int32, (1, 2 * p), 1)
    row_w = lax.broadcasted_iota(jnp.int32, (2 * w, 1), 0)
    row_p = lax.broadcasted_iota(jnp.int32, (2 * p, 1), 0)
    tau_of_lane = jnp.right_shift(lane_w, 4)
    gl_of_lane = jnp.right_shift(lane_p, 6)
    step_of_row = jnp.right_shift(row_w, 5)
    same_group = jnp.bitwise_and(jnp.right_shift(row_w, 4), 1) == gl_of_lane
    first_rows = row_p < p
    tg_ref[...] = jnp.zeros_like(tg_ref)
    l16_rows = []
    for d in range(2):
        lr, li = lrr_ref[d], lir_ref[d]
        zr, zi = _lam_bar(lr, li, lsr_ref[d])
        cfr, cfi = _zoh_coef(lr, li, zr, zi)
        btr, bti = _cmul(cfr, cfi, bt_ref[d, 0], bt_ref[d, 1])
        sq = _squarings(zr, zi, 5)
        e_inj = (tc - 1 - step_of_row) if d == 0 else step_of_row
        pr, pi = _cpow(sq[:4], jnp.broadcast_to(e_inj, (2 * w, 2 * p)))
        ir, ii = _cmul(pr, pi, jnp.tile(btr, (2 * tc, 1)), jnp.tile(bti, (2 * tc, 1)))
        wb_o[:, (2 * d) * LANE:(2 * d + 1) * LANE] = jnp.where(same_group, ir, 0.0).astype(wb_o.dtype)
        wb_o[:, (2 * d + 1) * LANE:(2 * d + 2) * LANE] = jnp.where(same_group, ii, 0.0).astype(wb_o.dtype)
        l16_rows += [sq[4][0], sq[4][1]]
        lrc, lic = lrc_ref[d], lic_ref[d]
        zcr, zci = _lam_bar(lrc, lic, lsc_ref[d])
        tau = tau_of_lane if d == 0 else (tc - 1) - tau_of_lane
        pr, pi = _cpow(_squarings(zcr, zci, 4), jnp.broadcast_to(tau, (2 * p, w)))
        c0r, c0i = _cmul(ct_ref[d, 0], ct_ref[d, 1], pr, pi)
        c1r, c1i = _cmul(c0r, c0i, zcr, zci)
        for r, val in ((2 * d, c1r), (2 * d + 1, -c1i)):
            vb = val.astype(MXU_DTYPE)
            spread = jnp.where(first_rows, jnp.dot(vb, s0_ref[...], preferred_element_type=F32),
                               jnp.dot(vb, s1_ref[...], preferred_element_type=F32))
            wc_o[r * LANE:(r + 1) * LANE, :] = spread.astype(wc_o.dtype)
        for gl in range(2):
            in_group = gl_of_lane == gl
            kt = (jnp.dot(jnp.where(in_group, btr, 0.0), c0r, preferred_element_type=F32,
                          precision=lax.Precision.HIGHEST)
                  - jnp.dot(jnp.where(in_group, bti, 0.0), c0i, preferred_element_type=F32,
                            precision=lax.Precision.HIGHEST))
            for s in range(tc):
                if d == 0:
                    shift, keep = hg * s, lane_w >= hg * s
                else:
                    shift, keep = (w - hg * (tc - 1 - s)) % w, lane_w < hg * (s + 1)
                rolled = kt if shift == 0 else pltpu.roll(kt, shift, 1)
                tg_ref[gl, s * hg:(s + 1) * hg, :] += jnp.where(keep, rolled, 0.0)
    for gl, s_ref in enumerate((s0_ref, s1_ref)):
        spread = jnp.dot(tg_ref[gl].astype(MXU_DTYPE), s_ref[...], preferred_element_type=F32)
        for s in range(tc):
            wt_o[(2 * s + gl) * hg:(2 * s + gl + 1) * hg, :] = spread[s * hg:(s + 1) * hg].astype(wt_o.dtype)
    l16_o[...] = jnp.concatenate(l16_rows + [jnp.zeros((4, 2 * p), F32)], axis=0)


def _ssm_operators(lam_re, lam_im, log_step, b_re, b_im, c_re, c_im):
    depth = lam_re.shape[0]
    p, hg, tc, q = SSM_STATE, SSM_GROUP_CH, SSM_CHUNK, SSM_PAIRS
    assert (hg, p, tc) == (16, 64, 16), "lane/row index arithmetic in the kernel uses these as shifts"
    w = tc * hg

    def per_pair(a, tail):
        a = a.astype(F32).reshape((depth, 2, q, 2) + tail)
        return jnp.transpose(a, (0, 2, 1, 3) + tuple(range(4, 4 + len(tail))))

    lam_r, lam_i = per_pair(lam_re, (p,)), per_pair(lam_im, (p,))
    ls = jnp.broadcast_to(per_pair(log_step, ())[..., None], lam_r.shape)
    rows = [a.reshape(depth, q, 2, 1, 2 * p) for a in (lam_r, lam_i, ls)]
    cols = [a.reshape(depth, q, 2, 2 * p, 1) for a in (lam_r, lam_i, ls)]
    bt = jnp.stack([per_pair(b_re, (p, hg)), per_pair(b_im, (p, hg))], axis=3)
    bt = jnp.transpose(bt, (0, 1, 2, 3, 6, 4, 5)).reshape(depth, q, 2, 2, hg, 2 * p)
    ct = jnp.stack([per_pair(c_re, (hg, p)), per_pair(c_im, (hg, p))], axis=3)
    ct = jnp.transpose(ct, (0, 1, 2, 3, 4, 6, 5)).reshape(depth, q, 2, 2, 2 * p, hg)
    ct = jnp.tile(ct, (1, 1, 1, 1, 1, tc))
    r, c = np.arange(w)[:, None], np.arange(2 * w)[None, :]
    hit = (r // hg == c // (2 * hg)) & (r % hg == c % hg)
    spread = [jnp.asarray(hit & ((c // hg) % 2 == gl), MXU_DTYPE) for gl in range(2)]

    blk = lambda a: pl.BlockSpec((None, None) + a.shape[2:], lambda l, i: (l, i) + (0,) * (a.ndim - 2))
    const = pl.BlockSpec((w, 2 * w), lambda l, i: (0, 0))
    mat = pl.BlockSpec((None, None, 2 * w, 2 * w), lambda l, i: (l, i, 0, 0))
    args = rows + cols + [bt, ct]
    return pl.pallas_call(
        _ssm_ops_kernel,
        grid=(depth, q),
        in_specs=[blk(a) for a in args] + [const, const],
        out_specs=[mat, mat, mat, pl.BlockSpec((None, None, 8, 2 * p), lambda l, i: (l, i, 0, 0))],
        out_shape=[jax.ShapeDtypeStruct((depth, q, 2 * w, 2 * w), MXU_DTYPE)] * 3
        + [jax.ShapeDtypeStruct((depth, q, 8, 2 * p), F32)],
        scratch_shapes=[pltpu.VMEM((2, w, w), F32)],
        compiler_params=_params(("parallel", "parallel")),
        name="ssm_ops",
    )(*args, *spread)


_PAIRS_PER_TILE = 4
_SSM_ROW_BLOCK = 64
_SCAN_ROW_PAD = 8


def _ssm_kernel(zu_ref, wt_ref, wb_ref, wc_ref, l16_ref, h0_ref, y_o, fin_o, u_ref, yp_ref, *state_refs,
                batch, seq_len):
    tc, npair = SSM_CHUNK, _PAIRS_PER_TILE
    n_chunks = seq_len // tc
    nrows = batch * n_chunks
    rb = _SSM_ROW_BLOCK
    rs = n_chunks + _SCAN_ROW_PAD
    slot_w = LANE // npair
    slot = jnp.right_shift(lax.broadcasted_iota(jnp.int32, (1, LANE), 1), 5)
    dx_refs, xs_refs = state_refs[:4], state_refs[4:]

    def place(pieces, src_slot):
        offset = src_slot
        out = None
        for j, piece in enumerate(pieces):
            shift = (slot_w * (j - offset[j])) % LANE
            r = piece if shift == 0 else pltpu.roll(piece, shift, 1)
            out = r if out is None else jnp.where(slot == j, r, out)
        return out

    def gather_block(i, carry):
        r0 = pl.multiple_of(i * rb, rb)
        steps = [zu_ref[pl.ds(r0 * tc + s, rb, stride=tc), :] for s in range(tc)]
        for p in range(npair):
            tiles = [place(steps[4 * k:4 * k + 4], [p] * 4) for k in range(tc // 4)]
            u_ref[p, pl.ds(r0, rb), :] = jnp.concatenate(tiles, axis=1).astype(u_ref.dtype)
        return carry

    lax.fori_loop(0, nrows // rb, gather_block, 0)

    for p in range(npair):
        u = u_ref[p]
        y_intra = jnp.dot(u, wt_ref[p], preferred_element_type=F32)
        dx = jnp.dot(u, wb_ref[p], preferred_element_type=F32)
        for r in range(4):
            for b in range(batch):
                dx_refs[r][b * rs:b * rs + n_chunks, :] = dx[b * n_chunks:(b + 1) * n_chunks, r * LANE:(r + 1) * LANE]
        lfr, lfi, lbr, lbi = (l16_ref[p, r:r + 1, :] for r in range(4))

        def body(c, carry):
            fr, fi, br, bi = carry
            fwd = pl.ds(c, batch, stride=rs)
            bwd = pl.ds(n_chunks - 1 - c, batch, stride=rs)
            xs_refs[0][fwd, :] = fr
            xs_refs[1][fwd, :] = fi
            xs_refs[2][bwd, :] = br
            xs_refs[3][bwd, :] = bi
            nfr = lfr * fr - lfi * fi + dx_refs[0][fwd, :]
            nfi = lfr * fi + lfi * fr + dx_refs[1][fwd, :]
            nbr = lbr * br - lbi * bi + dx_refs[2][bwd, :]
            nbi = lbr * bi + lbi * br + dx_refs[3][bwd, :]
            return nfr, nfi, nbr, nbi

        fin = lax.fori_loop(0, n_chunks, body, tuple(h0_ref[p, :, r * LANE:(r + 1) * LANE] for r in range(4)),
                            unroll=4)
        for r in range(4):
            fin_o[p, :, r * LANE:(r + 1) * LANE] = fin[r]
        xs = jnp.concatenate(
            [jnp.concatenate([x[b * rs:b * rs + n_chunks, :] for b in range(batch)], axis=0) for x in xs_refs],
            axis=1).astype(MXU_DTYPE)
        yp_ref[p] = y_intra + jnp.dot(xs, wc_ref[p], preferred_element_type=F32)

    def scatter_block(i, carry):
        r0 = pl.multiple_of(i * rb, rb)
        for k in range(tc // 4):
            pieces = [yp_ref[p, pl.ds(r0, rb), k * LANE:(k + 1) * LANE] for p in range(npair)]
            for j in range(4):
                y_o[pl.ds(r0 * tc + 4 * k + j, rb, stride=tc), :] = place(pieces, [j] * npair)
        return carry

    lax.fori_loop(0, nrows // rb, scatter_block, 0)


def _ssm_scan(zu2d, w_t, w_b, w_c, l16, h0, batch, seq_len):
    t = zu2d.shape[0]
    nrows = t // SSM_CHUNK
    npair = _PAIRS_PER_TILE
    assert nrows % _SSM_ROW_BLOCK == 0 and seq_len % SSM_CHUNK == 0
    once = pl.Buffered(1)
    wspec = pl.BlockSpec((npair, 512, 512), lambda i: (i, 0, 0))
    return pl.pallas_call(
        functools.partial(_ssm_kernel, batch=batch, seq_len=seq_len),
        grid=(SSM_PAIRS // npair,),
        in_specs=[
            pl.BlockSpec((t, LANE), lambda i: (0, i), pipeline_mode=once),
            wspec, wspec, wspec,
            pl.BlockSpec((npair, 8, LANE), lambda i: (i, 0, 0)),
            pl.BlockSpec((npair, batch, 512), lambda i: (i, 0, 0)),
        ],
        out_specs=[
            pl.BlockSpec((t, LANE), lambda i: (0, i), pipeline_mode=once),
            pl.BlockSpec((npair, batch, 512), lambda i: (i, 0, 0)),
        ],
        out_shape=[jax.ShapeDtypeStruct((t, SSM_WIDTH), F32),
                   jax.ShapeDtypeStruct((SSM_PAIRS, batch, 512), F32)],
        scratch_shapes=[pltpu.VMEM((npair, nrows, 512), MXU_DTYPE), pltpu.VMEM((npair, nrows, 512), F32)]
        + [pltpu.VMEM((batch * (seq_len // SSM_CHUNK + _SCAN_ROW_PAD), LANE), F32)] * 8,
        compiler_params=_params(("arbitrary",)),
        name="ssm_scan",
    )(zu2d, w_t, w_b, w_c, l16, h0)


def _pack_state(s_re, s_im):
    b = s_re.shape[0]
    a = jnp.stack([s_re, s_im], axis=2).reshape(b, 2, 2, SSM_PAIRS, 2 * SSM_STATE)
    return jnp.transpose(a, (3, 0, 1, 2, 4)).reshape(SSM_PAIRS, b, 4 * 2 * SSM_STATE).astype(F32)


def _unpack_state(fin):
    b = fin.shape[1]
    a = jnp.transpose(fin.reshape(SSM_PAIRS, b, 2, 2, 2 * SSM_STATE), (1, 2, 3, 0, 4))
    a = a.reshape(b, 2, 2, SSM_GROUPS, SSM_STATE)
    return a[:, :, 0], a[:, :, 1]


def _merge_kernel(ya_ref, ys_ref, u_ref, yc_ref, g_ref, x_ref, mod_ref, ng_ref, d_ref,
                  wglu_ref, wa_ref, wb_ref, wc_ref, wo_ref, x_o, h_o):
    y = ys_ref[...].astype(F32) + d_ref[...] * u_ref[...]
    gl = _gelu_tanh(y)
    yb = gl * _sigmoid(_mm(gl, wglu_ref[...]))
    merged = (g_ref[:, 0:D_MODEL].astype(F32) * _mm(ya_ref[...], wa_ref[...])
              + g_ref[:, D_MODEL:2 * D_MODEL].astype(F32) * _mm(yb, wb_ref[...])
              + g_ref[:, 2 * D_MODEL:3 * D_MODEL].astype(F32) * _mm(yc_ref[...], wc_ref[...]))
    x1 = x_ref[...] + mod_ref[2:3, :] * _rms(_mm(merged, wo_ref[...]), ng_ref[1:2, :])
    x_o[...] = x1
    h_o[...] = (_rms(x1, ng_ref[2:3, :]) * (1.0 + mod_ref[4:5, :]) + mod_ref[3:4, :]).astype(h_o.dtype)


def _merge(ya, ys, zu, yc, gates, x2d, mod_l, mod_row, ng, ssm_d, w_glu, w_a, w_b, w_c, w_o):
    t = x2d.shape[0]
    tm = TOKEN_TILE
    row = lambda i: (i, 0)
    const = lambda i: (0, 0)
    r512 = pl.BlockSpec((tm, 512), row)
    wbr = pl.BlockSpec((512, D_MODEL), const)
    return pl.pallas_call(
        _merge_kernel,
        grid=(t // tm,),
        in_specs=[
            r512, r512, r512, r512,
            pl.BlockSpec((tm, 3 * D_MODEL), row),
            pl.BlockSpec((tm, D_MODEL), row),
            pl.BlockSpec((None, 6, D_MODEL), lambda i: (mod_row(i, tm), 0, 0)),
            pl.BlockSpec((4, D_MODEL), const),
            pl.BlockSpec((1, 512), const),
            pl.BlockSpec((512, 512), const),
            wbr, wbr, wbr,
            pl.BlockSpec((D_MODEL, D_MODEL), const),
        ],
        out_specs=[pl.BlockSpec((tm, D_MODEL), row), pl.BlockSpec((tm, D_MODEL), row)],
        out_shape=[jax.ShapeDtypeStruct((t, D_MODEL), F32), jax.ShapeDtypeStruct((t, D_MODEL), MXU_DTYPE)],
        compiler_params=_params(("parallel",)),
        name="merge",
    )(ya, ys, zu, yc, gates, x2d, mod_l, ng, ssm_d, w_glu, w_a, w_b, w_c, w_o)


def _ffn_kernel(*refs, seq_len, first, last):
    refs = list(refs)
    h_ref, hp_ref, hn_ref = refs[:3]
    del refs[:3]
    part_ref = None if first else refs.pop(0)
    if last:
        x_ref, mod_ref, ng_ref = refs[:3]
        del refs[:3]
    wa_ref, wg_ref, cwa_ref, cwg_ref, cba_ref, cbg_ref, wd_ref, out_ref, ua_ref, ug_ref, act_ref = refs
    i = pl.program_id(0)
    tm = h_ref.shape[0]
    ft = wd_ref.shape[0]
    n = tm + 16
    start = jnp.bitwise_and(i * tm, seq_len - 1)
    keep_prev = (start != 0).astype(F32)
    keep_next = (jnp.bitwise_and(start + tm, seq_len - 1) != 0).astype(F32)
    hh = jnp.concatenate([(hp_ref[...] * keep_prev).astype(h_ref.dtype), h_ref[...],
                          (hn_ref[...] * keep_next).astype(h_ref.dtype)], axis=0)
    ua_ref[...] = jnp.dot(hh, wa_ref[...], preferred_element_type=F32)
    ug_ref[...] = jnp.dot(hh, wg_ref[...], preferred_element_type=F32)
    interior = tm > seq_len
    if interior:
        pos = jnp.bitwise_and(lax.broadcasted_iota(jnp.int32, (tm, 1), 0), seq_len - 1)
        has_prev = (pos != 0).astype(F32)
        has_next = (pos != seq_len - 1).astype(F32)

    def conv(u_ref, cw_ref, cb_ref, lo):
        uc = u_ref[:, lo:lo + LANE]
        up = pltpu.roll(uc, 1, 0)[8:8 + tm]
        un = pltpu.roll(uc, n - 1, 0)[8:8 + tm]
        if interior:
            up, un = up * has_prev, un * has_next
        return (cw_ref[0:1, lo:lo + LANE] * up + cw_ref[1:2, lo:lo + LANE] * uc[8:8 + tm]
                + cw_ref[2:3, lo:lo + LANE] * un + cb_ref[0:1, lo:lo + LANE])

    for kc in range(ft // LANE):
        a = conv(ua_ref, cwa_ref, cba_ref, kc * LANE)
        g = conv(ug_ref, cwg_ref, cbg_ref, kc * LANE)
        act_ref[:, kc * LANE:(kc + 1) * LANE] = (g * _sigmoid(g) * a).astype(act_ref.dtype)
    total = jnp.dot(act_ref[...], wd_ref[...], preferred_element_type=F32)
    if not first:
        total = total + part_ref[...]
    if last:
        out_ref[...] = x_ref[...] + mod_ref[5:6, :] * _rms(total, ng_ref[3:4, :])
    else:
        out_ref[...] = total


def _conv_ffn(h2, x1, mod_l, mod_row, ng, w_up, conv_w, conv_b, w_down, seq_len):
    t = x1.shape[0]
    tm = FFN_TOKEN_TILE
    ft = FF_TILE
    nf = D_FF // ft
    nblk8 = t // 8
    assert seq_len & (seq_len - 1) == 0 and (seq_len % tm == 0 or tm % seq_len == 0) and t % tm == 0
    row = pl.BlockSpec((tm, D_MODEL), lambda i: (i, 0))
    once = pl.Buffered(1)
    part = None
    for j in range(nf):
        first, last = j == 0, j == nf - 1
        in_specs = [
            row,
            pl.BlockSpec((8, D_MODEL), lambda i: (jnp.maximum(i * (tm // 8) - 1, 0), 0)),
            pl.BlockSpec((8, D_MODEL), lambda i: (jnp.minimum((i + 1) * (tm // 8), nblk8 - 1), 0)),
        ]
        args = [h2, h2, h2]
        if not first:
            in_specs.append(row)
            args.append(part)
        if last:
            in_specs += [row, pl.BlockSpec((None, 6, D_MODEL), lambda i: (mod_row(i, tm), 0, 0)),
                         pl.BlockSpec((4, D_MODEL), lambda i: (0, 0))]
            args += [x1, mod_l, ng]
        in_specs += [
            pl.BlockSpec((D_MODEL, ft), lambda i, j=j: (0, j), pipeline_mode=once),
            pl.BlockSpec((D_MODEL, ft), lambda i, j=j: (0, nf + j), pipeline_mode=once),
            pl.BlockSpec((3, ft), lambda i, j=j: (0, j)),
            pl.BlockSpec((3, ft), lambda i, j=j: (0, nf + j)),
            pl.BlockSpec((1, ft), lambda i, j=j: (0, j)),
            pl.BlockSpec((1, ft), lambda i, j=j: (0, nf + j)),
            pl.BlockSpec((ft, D_MODEL), lambda i, j=j: (j, 0), pipeline_mode=once),
        ]
        args += [w_up, w_up, conv_w, conv_w, conv_b, conv_b, w_down]
        part = pl.pallas_call(
            functools.partial(_ffn_kernel, seq_len=seq_len, first=first, last=last),
            grid=(t // tm,),
            in_specs=in_specs,
            out_specs=row,
            out_shape=jax.ShapeDtypeStruct((t, D_MODEL), F32),
            scratch_shapes=[pltpu.VMEM((tm + 16, ft), F32), pltpu.VMEM((tm + 16, ft), F32),
                            pltpu.VMEM((tm, ft), MXU_DTYPE)],
            compiler_params=_params(("parallel",)),
            name="conv_ffn_last" if last else "conv_ffn_part",
        )(*args)
    return part


_Q_HEAD_ORDER = (0, 4, 1, 5, 2, 6, 3, 7)


def _rope_tables(seq_len):
    nf = HEAD_DIM // 4
    t = np.arange(seq_len)
    pos = np.stack([t // GRID_W, t % GRID_W]).astype(np.float32)
    inv = jnp.asarray(ROPE_THETA, F32) ** (-jnp.arange(nf, dtype=F32) / nf)
    ang = jnp.asarray(pos)[:, :, None] * inv
    d = np.arange(HEAD_DIM)
    ang = ang[d // (2 * nf), :, d % nf].T
    second = jnp.asarray(((d % (2 * nf)) // nf) == 1)[None, :]
    cos, sin = jnp.cos(ang), jnp.sin(ang)
    tabs = (cos, jnp.where(second, 0.0, -sin), jnp.where(second, sin, 0.0))
    return tuple(jnp.tile(x, (1, LANE // HEAD_DIM)).astype(F32) for x in tabs)


def _layer_weights(w_in, qk_g, w_br_a):
    hd = HEAD_DIM
    w_in_p = jnp.concatenate([w_in[:, h * hd:(h + 1) * hd] for h in _Q_HEAD_ORDER] + [w_in[:, 512:]],
                             axis=1).astype(MXU_DTYPE)
    w_a_p = jnp.concatenate([w_br_a[h * hd:(h + 1) * hd] for h in _Q_HEAD_ORDER], axis=0).astype(MXU_DTYPE)
    qg = jnp.tile(qk_g[0], N_HEADS).reshape(1, 512).astype(F32)
    kg = jnp.tile(qk_g[1], GA_KV_HEADS).reshape(1, LANE).astype(F32)
    return w_in_p, w_a_p, qg, kg


def kernel(x_prompt, x_sample, c, cache_ga_k, cache_ga_v, cache_na_k, cache_na_v, state_ssm_re, state_ssm_im,
           c_ctx, w_mod, b_mod, norm_g, w_in, qk_norm_g, na_rpb, ssm_lam_re, ssm_lam_im, ssm_log_step,
           ssm_b_re, ssm_b_im, ssm_c_re, ssm_c_im, ssm_d, w_glu, w_br_a, w_br_b, w_br_c, w_out,
           w_up, conv_w, conv_b, w_down):
    depth = w_in.shape[0]
    bp, lp, _ = x_prompt.shape
    bs, ls, _ = x_sample.shape
    lc = cache_ga_k.shape[2]
    assert lp % 256 == 0 and ls % FFN_TOKEN_TILE == 0 and (bp * lp) % FFN_TOKEN_TILE == 0
    assert FFN_TOKEN_TILE % TOKEN_TILE == 0
    assert bs % 8 == 0 and bp % 8 == 0, "the scan keeps one batch row per sublane"

    rows = 1 + bs
    rows_p = -(-rows // 8) * 8
    cvec = jnp.concatenate([c_ctx[None], c, jnp.zeros((rows_p - rows, D_MODEL), F32)], axis=0)
    mod = _modulation(cvec, w_mod, b_mod).reshape(depth, rows_p, 6, D_MODEL)

    w_t, w_b, w_c, l16 = _ssm_operators(ssm_lam_re, ssm_lam_im, ssm_log_step, ssm_b_re, ssm_b_im,
                                        ssm_c_re, ssm_c_im)
    seg = jnp.asarray(np.kron(np.eye(N_HEADS), np.full((HEAD_DIM, HEAD_DIM), 1.0 / HEAD_DIM)), MXU_DTYPE)
    rope_tabs = _rope_tables(ls)
    ctx_row = lambda i, tm: 0
    lat_row = lambda i, tm: 1 + (i * tm) // ls

    y_p = x_prompt.reshape(bp * lp, D_MODEL)
    y_s = x_sample.reshape(bs * ls, D_MODEL)
    zero_state = jnp.zeros((SSM_PAIRS, bp, 512), F32)
    cache_ga = [a.reshape(bs, depth, lc, LANE) for a in (cache_ga_k, cache_ga_v)]
    cache_na = [a.reshape(bs, depth, lc, 512) for a in (cache_na_k, cache_na_v)]
    states = ([], [])
    kv_stacks = tuple(jnp.zeros((bp, depth, lp, w), F32) for w in (LANE, LANE, 512, 512))
    for l in range(depth):
        w_in_p, w_a_p, qg, kg = _layer_weights(w_in[l], qk_norm_g[l], w_br_a[l])
        w_glu_l, w_b_l, w_c_l, w_o_l = (a[l].astype(MXU_DTYPE) for a in (w_glu, w_br_b, w_br_c, w_out))
        ffn_w = (w_up[l].astype(MXU_DTYPE), conv_w[l].astype(F32), conv_b[l].reshape(1, 2 * D_FF).astype(F32),
                 w_down[l].astype(MXU_DTYPE))
        d_l = ssm_d[l].reshape(1, SSM_WIDTH).astype(F32)
        ng = norm_g[l].astype(F32)
        ssm_ops = (w_t[l], w_b[l], w_c[l], l16[l])

        q, k, v, zu, nq, nk, nv, gates = _in_projection(
            y_p, mod[l], ctx_row, ng, w_in_p, qg, kg, seg, None, lp, F32, stack=(l, depth, kv_stacks))
        kv_stacks = (k, v, nk, nv)
        r3 = lambda a: a.reshape(bp, lp, a.shape[-1])
        ya = _attention(r3(q), k, v, None, None, lp, "ga_ctx", kv_layer=l)
        yc = _attention(r3(nq), nk, nv, None, None, lp, "na_ctx", kv_layer=l)
        ys, fin = _ssm_scan(zu, *ssm_ops, zero_state, bp, lp)
        x1, h2 = _merge(ya.reshape(-1, 512), ys, zu, yc.reshape(-1, 512), gates, y_p,
                        mod[l], ctx_row, ng, d_l, w_glu_l, w_a_p, w_b_l, w_c_l, w_o_l)
        y_p = _conv_ffn(h2, x1, mod[l], ctx_row, ng, *ffn_w, lp)
        f_re, f_im = _unpack_state(fin)
        states[0].append(f_re)
        states[1].append(f_im)

        q, k, v, zu, nq, nk, nv, gates = _in_projection(
            y_s, mod[l], lat_row, ng, w_in_p, qg, kg, seg, rope_tabs, ls, MXU_DTYPE)
        r3 = lambda a: a.reshape(bs, ls, a.shape[-1])
        ya = _attention(r3(q), r3(k), r3(v), cache_ga[0], cache_ga[1], 4 * GRID_W, "ga_lat", cache_layer=l)
        yc = _neighbourhood_attention(r3(nq), r3(nk), r3(nv), cache_na[0], cache_na[1], l, na_rpb[l])
        h0 = _pack_state(state_ssm_re[:, l], state_ssm_im[:, l])
        ys, _ = _ssm_scan(zu, *ssm_ops, h0, bs, ls)
        x1, h2 = _merge(ya.reshape(-1, 512), ys, zu, yc.reshape(-1, 512), gates, y_s,
                        mod[l], lat_row, ng, d_l, w_glu_l, w_a_p, w_b_l, w_c_l, w_o_l)
        y_s = _conv_ffn(h2, x1, mod[l], lat_row, ng, *ffn_w, ls)

    k, v, nk, nv = kv_stacks
    return (y_p.reshape(bp, lp, D_MODEL), y_s.reshape(bs, ls, D_MODEL),
            k.reshape(bp, depth, lp, GA_KV_HEADS, HEAD_DIM), v.reshape(bp, depth, lp, GA_KV_HEADS, HEAD_DIM),
            nk.reshape(bp, depth, lp, N_HEADS, HEAD_DIM), nv.reshape(bp, depth, lp, N_HEADS, HEAD_DIM),
            jnp.stack(states[0], axis=1), jnp.stack(states[1], axis=1))
```

```python
import functools
import math

import numpy as np
import jax
import jax.numpy as jnp
from jax import lax
from jax.experimental import pallas as pl
from jax.experimental.pallas import tpu as pltpu

F32 = jnp.float32
MXU_DTYPE = jnp.bfloat16

D_MODEL = 1024
HEAD_DIM = 64
N_HEADS = 8
GA_KV_HEADS = 2
GRID_W = 64
NA_WIN_ROWS = 8
NA_WIN_COLS = 16
NA_KEY_ROWS = 10
SSM_WIDTH = 512
SSM_GROUPS = 32
SSM_GROUP_CH = 16
SSM_STATE = 64
SSM_CHUNK = 16
SSM_PAIRS = SSM_GROUPS // 2
D_FF = 2816
FF_TILE = 1408
ROPE_THETA = 10000.0
EPS = 1e-6
IN_WIDTH = 5888
NEG_BIG = -1e30

LANE = 128
TOKEN_TILE = 512
FFN_TOKEN_TILE = 1024
VMEM_LIMIT = 56 * 1024 * 1024

_Q0, _K0, _V0, _U0, _NQ0, _NK0, _NV0, _G0 = 0, 512, 640, 768, 1280, 1792, 2304, 2816


def _sigmoid(x):
    return 1.0 / (1.0 + jnp.exp(-x))


def _gelu_tanh(x):
    return 0.5 * x * (1.0 + jnp.tanh(math.sqrt(2.0 / math.pi) * (x + 0.044715 * (x * x * x))))


def _rms(x, g):
    ms = jnp.mean(x * x, axis=-1, keepdims=True)
    return (x * lax.rsqrt(ms + EPS)) * g


def _mm(a, b):
    return jnp.dot(a.astype(MXU_DTYPE), b.astype(MXU_DTYPE), preferred_element_type=F32)


def _mm_nt(a, b):
    return lax.dot_general(a.astype(MXU_DTYPE), b.astype(MXU_DTYPE), (((1,), (1,)), ((), ())),
                           preferred_element_type=F32)


def _params(sem):
    return pltpu.CompilerParams(dimension_semantics=sem, vmem_limit_bytes=VMEM_LIMIT)


def _mod_kernel(c_ref, w_ref, b_ref, o_ref):
    c = c_ref[...]
    o_ref[...] = _mm(c * _sigmoid(c), w_ref[...]) + b_ref[...]


def _modulation(cvec, w_mod, b_mod):
    depth = w_mod.shape[0]
    rows = cvec.shape[0]
    tn = 1536
    return pl.pallas_call(
        _mod_kernel,
        grid=(depth, 6 * D_MODEL // tn),
        in_specs=[
            pl.BlockSpec((rows, D_MODEL), lambda l, j: (0, 0)),
            pl.BlockSpec((None, D_MODEL, tn), lambda l, j: (l, 0, j)),
            pl.BlockSpec((None, 1, tn), lambda l, j: (l, 0, j)),
        ],
        out_specs=pl.BlockSpec((None, rows, tn), lambda l, j: (l, 0, j)),
        out_shape=jax.ShapeDtypeStruct((depth, rows, 6 * D_MODEL), F32),
        compiler_params=_params(("parallel", "parallel")),
        name="adaln_mod",
    )(cvec, w_mod, b_mod.reshape(depth, 1, 6 * D_MODEL))


def _head_rms(z, seg, gain):
    ms = jnp.dot((z * z).astype(MXU_DTYPE), seg, preferred_element_type=F32)
    return (z * lax.rsqrt(ms + EPS)) * gain


def _rope_tile(t, c, s_up, s_dn):
    return t * c + pltpu.roll(t, LANE - 16, 1) * s_up + pltpu.roll(t, 16, 1) * s_dn


def _inproj_kernel(*refs, rope, n_carried, new_stack_layer=None):
    n_in = 10 if rope else 7
    refs = refs[:n_in] + refs[n_in + n_carried:]
    if rope:
        (x_ref, mod_ref, ng_ref, w_ref, qg_ref, kg_ref, seg_ref, cos_ref, sup_ref, sdn_ref,
         q_o, k_o, v_o, u_o, nq_o, nk_o, nv_o, g_o) = refs
    else:
        (x_ref, mod_ref, ng_ref, w_ref, qg_ref, kg_ref, seg_ref,
         q_o, k_o, v_o, u_o, nq_o, nk_o, nv_o, g_o) = refs
    x = x_ref[...]
    h = _rms(x, ng_ref[0:1, :]) * (1.0 + mod_ref[1:2, :]) + mod_ref[0:1, :]
    hb = h.astype(MXU_DTYPE)
    scale = HEAD_DIM ** -0.5

    def proj(lo, width):
        return jnp.dot(hb, w_ref[:, lo:lo + width], preferred_element_type=F32)

    def maybe_rope(z):
        if not rope:
            return z
        c, su, sd = cos_ref[...], sup_ref[...], sdn_ref[...]
        tiles = [_rope_tile(z[:, i * LANE:(i + 1) * LANE], c, su, sd) for i in range(z.shape[1] // LANE)]
        return tiles[0] if len(tiles) == 1 else jnp.concatenate(tiles, axis=1)

    def put(o_ref, val):
        val = val.astype(o_ref.dtype)
        if new_stack_layer is None:
            o_ref[...] = val.reshape(o_ref.shape)
            return
        nseq, depth, seq, width = o_ref.shape
        for d in range(depth):
            o_ref[:, d] = val.reshape(nseq, seq, width) if d == new_stack_layer else jnp.zeros(
                (nseq, seq, width), o_ref.dtype)

    q = maybe_rope(_head_rms(proj(_Q0, 512), seg_ref[...], qg_ref[...]))
    q_o[...] = (q * scale).astype(q_o.dtype)
    put(k_o, maybe_rope(_head_rms(proj(_K0, 128), seg_ref[0:LANE, 0:LANE], kg_ref[...])))
    put(v_o, proj(_V0, 128))
    u_o[...] = proj(_U0, 512).astype(u_o.dtype)
    nq_o[...] = (proj(_NQ0, 512) * scale).astype(nq_o.dtype)
    put(nk_o, proj(_NK0, 512))
    put(nv_o, proj(_NV0, 512))
    for i in range(3):
        g_o[:, i * D_MODEL:(i + 1) * D_MODEL] = _sigmoid(proj(_G0 + i * D_MODEL, D_MODEL)).astype(g_o.dtype)


def _in_projection(x2d, mod_l, mod_row, ng, w_in, qg, kg, seg, rope_tabs, seq_len, kv_dtype, stack=None):
    t = x2d.shape[0]
    tm = TOKEN_TILE
    tiles_per_seq = max(seq_len // tm, 1)
    rope = rope_tabs is not None
    row = lambda i: (i, 0)
    const = lambda i: (0, 0)
    in_specs = [
        pl.BlockSpec((tm, D_MODEL), row),
        pl.BlockSpec((None, 6, D_MODEL), lambda i: (mod_row(i, tm), 0, 0)),
        pl.BlockSpec((4, D_MODEL), const),
        pl.BlockSpec((D_MODEL, IN_WIDTH), const),
        pl.BlockSpec((1, 512), const),
        pl.BlockSpec((1, LANE), const),
        pl.BlockSpec((512, 512), const),
    ]
    args = [x2d, mod_l, ng, w_in, qg, kg, seg]
    if rope:
        in_specs += [pl.BlockSpec((tm, LANE), lambda i: (i % tiles_per_seq, 0))] * 3
        args += list(rope_tabs)
    widths = (512, 128, 128, 512, 512, 512, 512, 3 * D_MODEL)
    dtypes = (MXU_DTYPE, kv_dtype, kv_dtype, F32, MXU_DTYPE, kv_dtype, kv_dtype, MXU_DTYPE)
    out_specs = [pl.BlockSpec((tm, w), row) for w in widths]
    out_shape = [jax.ShapeDtypeStruct((t, w), dt) for w, dt in zip(widths, dtypes)]
    aliases, n_carried, new_stack_layer = {}, 0, None
    if stack is not None:
        layer, depth, carried = stack
        assert tm % seq_len == 0
        nseq = tm // seq_len
        for o in (1, 2, 5, 6):
            if carried is None:
                out_specs[o] = pl.BlockSpec((nseq, depth, seq_len, widths[o]), lambda i: (i, 0, 0, 0))
            else:
                out_specs[o] = pl.BlockSpec((nseq, None, seq_len, widths[o]), lambda i: (i, layer, 0, 0))
            out_shape[o] = jax.ShapeDtypeStruct((t // seq_len, depth, seq_len, widths[o]), dtypes[o])
        if carried is None:
            new_stack_layer = layer
        else:
            n_carried = len(carried)
            aliases = {len(args) + n: o for n, o in enumerate((1, 2, 5, 6))}
            in_specs += [pl.BlockSpec(memory_space=pl.ANY)] * n_carried
            args += list(carried)
    return pl.pallas_call(
        functools.partial(_inproj_kernel, rope=rope, n_carried=n_carried, new_stack_layer=new_stack_layer),
        grid=(t // tm,),
        in_specs=in_specs,
        out_specs=out_specs,
        out_shape=out_shape,
        input_output_aliases=aliases,
        compiler_params=_params(("parallel",)),
        name="in_proj_rope" if rope else "in_proj",
    )(*args)


def _lane_masks(dtype):
    lane = lax.broadcasted_iota(jnp.int32, (1, LANE), 1)
    lo = lane < HEAD_DIM
    return lo, lo.astype(dtype), (~lo).astype(dtype)


_NA_TILES_PER_STEP = 2
_KEY_BLOCK = 256
_Q_SUB = 128


def _softmax_pv(qs, key_blocks, s_ref):
    macc = None
    for bi, (score_fn, _) in enumerate(key_blocks):
        sj = score_fn(qs)
        s_ref[bi] = sj
        macc = sj if macc is None else jnp.maximum(macc, sj)
    mb = jnp.broadcast_to(jnp.max(macc, axis=-1, keepdims=True), macc.shape)
    lacc = jnp.zeros(macc.shape, F32)
    o = jnp.zeros((qs.shape[0], LANE), F32)
    for bi, (_, v_fn) in enumerate(key_blocks):
        p = jnp.exp(s_ref[bi] - mb)
        lacc = lacc + p
        vb = v_fn()
        o = o + _mm(p[:, :vb.shape[0]], vb)
    return o * (1.0 / jnp.sum(lacc, axis=-1, keepdims=True))


def _attn_kernel(*refs, kv_tiles, cached):
    if cached:
        q_ref, k_ref, v_ref, kc_ref, vc_ref, o_ref, s_ref = refs
    else:
        q_ref, k_ref, v_ref, o_ref, s_ref = refs
    tq = q_ref.shape[0]
    kb = _KEY_BLOCK
    lo, m_lo, m_hi = _lane_masks(MXU_DTYPE)
    pairs_per_kv = (N_HEADS // 2) // kv_tiles
    sources = [(k_ref, v_ref)] + ([(kc_ref, vc_ref)] if cached else [])
    for q0 in range(0, tq, _Q_SUB):
        for hp in range(N_HEADS // 2):
            ksl = slice((hp // pairs_per_kv) * LANE, (hp // pairs_per_kv + 1) * LANE)
            q2 = q_ref[q0:q0 + _Q_SUB, hp * LANE:(hp + 1) * LANE]
            qs = jnp.concatenate([q2 * m_lo, q2 * m_hi], axis=0)
            blocks = [(functools.partial(lambda x, kr, off, ksl: _mm_nt(x, kr[off:off + kb, ksl]),
                                         kr=kr, off=off, ksl=ksl),
                       functools.partial(lambda vr, off, ksl: vr[off:off + kb, ksl], vr=vr, off=off, ksl=ksl))
                      for kr, vr in sources for off in range(0, kr.shape[0], kb)]
            o = _softmax_pv(qs, blocks, s_ref)
            o_ref[q0:q0 + _Q_SUB, hp * LANE:(hp + 1) * LANE] = jnp.where(
                lo, o[:_Q_SUB], o[_Q_SUB:]).astype(o_ref.dtype)


def _attention(q, k, v, kc, vc, tq, name, kv_layer=None, cache_layer=None):
    b, lq, _ = q.shape
    lk, kw = k.shape[-2], k.shape[-1]
    cached = kc is not None
    qmap = lambda bi, ti: (bi, ti, 0)

    def key_spec(n, layer):
        if layer is None:
            return pl.BlockSpec((None, n, kw), lambda bi, ti: (bi, 0, 0))
        return pl.BlockSpec((None, None, n, kw), lambda bi, ti: (bi, layer, 0, 0))

    in_specs = [pl.BlockSpec((None, tq, 512), qmap)] + [key_spec(lk, kv_layer)] * 2
    args = [q, k, v]
    lc = kc.shape[-2] if cached else 0
    if cached:
        in_specs += [key_spec(lc, cache_layer)] * 2
        args += [kc, vc]
    assert lk % _KEY_BLOCK == 0 and lc % _KEY_BLOCK == 0 and tq % _Q_SUB == 0
    n_blocks = (lk + lc) // _KEY_BLOCK
    return pl.pallas_call(
        functools.partial(_attn_kernel, kv_tiles=kw // LANE, cached=cached),
        grid=(b, lq // tq),
        in_specs=in_specs,
        out_specs=pl.BlockSpec((None, tq, 512), qmap),
        out_shape=jax.ShapeDtypeStruct((b, lq, 512), MXU_DTYPE),
        scratch_shapes=[pltpu.VMEM((n_blocks, 2 * _Q_SUB, _KEY_BLOCK), F32)],
        compiler_params=_params(("parallel", "parallel")),
        name=name,
    )(*args)


def _na_geometry(seq_len):
    rows = seq_len // GRID_W
    n_tiles = rows // 2
    assert rows >= NA_KEY_ROWS and NA_WIN_ROWS <= rows and NA_KEY_ROWS % 2 == 0
    ws = np.clip(2 * np.arange(n_tiles) - NA_WIN_ROWS // 2, 0, rows - NA_KEY_ROWS)
    r = 2 * np.arange(n_tiles)[:, None, None] + np.arange(2)[None, :, None]
    key_r = ws[:, None, None] + np.arange(NA_KEY_ROWS)[None, None, :]
    r0 = np.clip(r - NA_WIN_ROWS // 2, 0, rows - NA_WIN_ROWS)
    valid = (key_r >= r0) & (key_r < r0 + NA_WIN_ROWS)
    dr = np.where(valid, key_r - r + NA_WIN_ROWS - 1, 2 * NA_WIN_ROWS - 1)
    assert (valid.sum(-1) == NA_WIN_ROWS).all()
    return ws.astype(np.int32), dr.reshape(-1).astype(np.int32)


def _na_bias_blocks(rpb):
    h = rpb.shape[0]
    nrel = 2 * NA_WIN_ROWS - 1
    zeros = jnp.zeros((h, nrel, LANE - (2 * NA_WIN_COLS - 1)), F32)
    v = jnp.concatenate([rpb[..., NA_WIN_COLS - 1:], zeros, rpb[..., :NA_WIN_COLS - 1]], axis=-1).astype(F32)
    t = jnp.tile(v, (1, 1, GRID_W))[..., :GRID_W * (LANE - 1)].reshape(h, nrel, GRID_W, LANE - 1)[..., :GRID_W]
    c = np.arange(GRID_W)
    c0 = np.clip(c - NA_WIN_COLS // 2, 0, GRID_W - NA_WIN_COLS)
    colmask = (c[None, :] >= c0[:, None]) & (c[None, :] < c0[:, None] + NA_WIN_COLS)
    t = jnp.where(jnp.asarray(colmask)[None, None], t, NEG_BIG)
    t = jnp.concatenate([t, jnp.full((h, 1, GRID_W, GRID_W), NEG_BIG, F32)], axis=1)
    pad = jnp.zeros_like(t)
    return jnp.concatenate([t, pad], axis=-1), jnp.concatenate([pad, t], axis=-1)


def _na_kernel(ws_ref, dr_ref, q_ref, k_ref, v_ref, kc_ref, vc_ref, bl_ref, br_ref, o_ref, s_ref):
    for tt in range(_NA_TILES_PER_STEP):
        _na_tile(pl.program_id(1) * _NA_TILES_PER_STEP + tt, tt * 2 * GRID_W,
                 ws_ref, dr_ref, q_ref, k_ref, v_ref, kc_ref, vc_ref, bl_ref, br_ref, o_ref, s_ref)


def _na_tile(i, q0, ws_ref, dr_ref, q_ref, k_ref, v_ref, kc_ref, vc_ref, bl_ref, br_ref, o_ref, s_ref):
    start = pl.multiple_of(ws_ref[i] * GRID_W, GRID_W)
    nk = NA_KEY_ROWS * GRID_W
    kb = _KEY_BLOCK
    tq = 2 * GRID_W
    lc = kc_ref.shape[0]
    lo, m_lo, m_hi = _lane_masks(MXU_DTYPE)

    def bias_block(hp, off, width):
        rows = []
        for h in (2 * hp, 2 * hp + 1):
            for qr in range(2):
                base = (i * 2 + qr) * NA_KEY_ROWS + off // GRID_W
                tiles = [bl_ref[h, dr_ref[base + 2 * kp]] + br_ref[h, dr_ref[base + 2 * kp + 1]]
                         for kp in range(width // LANE)]
                rows.append(tiles[0] if len(tiles) == 1 else jnp.concatenate(tiles, axis=1))
        return jnp.concatenate(rows, axis=0)

    def local_scores(x, hp, sl, off, width):
        s = _mm_nt(x, k_ref[pl.ds(start + off, width), sl]) + bias_block(hp, off, width)
        if width < kb:
            s = jnp.concatenate([s, jnp.full((s.shape[0], kb - width), NEG_BIG, F32)], axis=1)
        return s

    for hp in range(N_HEADS // 2):
        sl = slice(hp * LANE, (hp + 1) * LANE)
        q2 = q_ref[q0:q0 + tq, sl]
        qs = jnp.concatenate([q2 * m_lo, q2 * m_hi], axis=0)
        blocks = []
        for off in range(0, nk, kb):
            width = min(kb, nk - off)
            blocks.append((functools.partial(local_scores, hp=hp, sl=sl, off=off, width=width),
                           functools.partial(lambda sl, off, width: v_ref[pl.ds(start + off, width), sl],
                                             sl=sl, off=off, width=width)))
        for off in range(0, lc, kb):
            blocks.append((functools.partial(lambda x, sl, off: _mm_nt(x, kc_ref[off:off + kb, sl]), sl=sl, off=off),
                           functools.partial(lambda sl, off: vc_ref[off:off + kb, sl], sl=sl, off=off)))
        o = _softmax_pv(qs, blocks, s_ref)
        o_ref[q0:q0 + tq, sl] = jnp.where(lo, o[:tq], o[tq:]).astype(o_ref.dtype)


def _neighbourhood_attention(q, k, v, kc, vc, cache_layer, rpb):
    b, seq_len, _ = q.shape
    lc = kc.shape[-2]
    ws, dr = _na_geometry(seq_len)
    b_left, b_right = _na_bias_blocks(rpb)
    n_tiles = len(ws)
    assert n_tiles % _NA_TILES_PER_STEP == 0
    tq = 2 * GRID_W
    tb = tq * _NA_TILES_PER_STEP
    qmap = lambda bi, ti, ws_r, dr_r: (bi, ti, 0)
    kmap = lambda bi, ti, ws_r, dr_r: (bi, 0, 0)
    bmap = lambda bi, ti, ws_r, dr_r: (0, 0, 0, 0)
    grid_spec = pltpu.PrefetchScalarGridSpec(
        num_scalar_prefetch=2,
        grid=(b, n_tiles // _NA_TILES_PER_STEP),
        in_specs=[
            pl.BlockSpec((None, tb, 512), qmap),
            pl.BlockSpec((None, seq_len, 512), kmap),
            pl.BlockSpec((None, seq_len, 512), kmap),
            pl.BlockSpec((None, None, lc, 512), lambda bi, ti, ws_r, dr_r: (bi, cache_layer, 0, 0)),
            pl.BlockSpec((None, None, lc, 512), lambda bi, ti, ws_r, dr_r: (bi, cache_layer, 0, 0)),
            pl.BlockSpec(b_left.shape, bmap),
            pl.BlockSpec(b_right.shape, bmap),
        ],
        out_specs=pl.BlockSpec((None, tb, 512), qmap),
        scratch_shapes=[pltpu.VMEM((-(-NA_KEY_ROWS * GRID_W // _KEY_BLOCK) + lc // _KEY_BLOCK, 2 * tq, _KEY_BLOCK),
                                   F32)],
    )
    assert lc % _KEY_BLOCK == 0
    return pl.pallas_call(
        _na_kernel,
        grid_spec=grid_spec,
        out_shape=jax.ShapeDtypeStruct((b, seq_len, 512), MXU_DTYPE),
        compiler_params=_params(("parallel", "arbitrary")),
        name="na_attn",
    )(jnp.asarray(ws), jnp.asarray(dr), q, k, v, kc, vc, b_left, b_right)


def _cmul(ar, ai, br, bi):
    return ar * br - ai * bi, ar * bi + ai * br


def _lam_bar(lr, li, ls):
    dt = jnp.exp(ls)
    mag = jnp.exp(lr * dt)
    return mag * jnp.cos(li * dt), mag * jnp.sin(li * dt)


def _zoh_coef(lr, li, zr, zi):
    nr, ni = zr - 1.0, zi
    den = 1.0 / (lr * lr + li * li)
    return (nr * lr + ni * li) * den, (ni * lr - nr * li) * den


def _squarings(zr, zi, n):
    out = [(zr, zi)]
    for _ in range(n - 1):
        zr, zi = _cmul(zr, zi, zr, zi)
        out.append((zr, zi))
    return out


def _cpow(squares, e):
    pr, pi = jnp.ones(e.shape, F32), jnp.zeros(e.shape, F32)
    for k, (zr, zi) in enumerate(squares):
        bit = jnp.bitwise_and(jnp.right_shift(e, k), 1) == 1
        nr, ni = _cmul(pr, pi, zr, zi)
        pr, pi = jnp.where(bit, nr, pr), jnp.where(bit, ni, pi)
    return pr, pi


def _ssm_ops_kernel(lrr_ref, lir_ref, lsr_ref, lrc_ref, lic_ref, lsc_ref, bt_ref, ct_ref, s0_ref, s1_ref,
                    wt_o, wb_o, wc_o, l16_o, tg_ref):
    tc, hg, p = SSM_CHUNK, SSM_GROUP_CH, SSM_STATE
    w = tc * hg
    lane_w = lax.broadcasted_iota(jnp.int32, (1, w), 1)
    lane_p = lax.broadcasted_iota(jnp.int32, (1, 2 * p), 1)
    row_w = lax.broadcasted_iota(jnp.int32, (2 * w, 1), 0)
    row_p = lax.broadcasted_iota(jnp.int32, (2 * p, 1), 0)
    tau_of_lane = jnp.right_shift(lane_w, 4)
    gl_of_lane = jnp.right_shift(lane_p, 6)
    same_group = jnp.bitwise_and(jnp.right_shift(row_w, 4), 1) == gl_of_lane
    first_rows = row_p < p
    tg_ref[...] = jnp.zeros_like(tg_ref)
    l16_rows = []
    for d in range(2):
        lr, li = lrr_ref[d], lir_ref[d]
        zr, zi = _lam_bar(lr, li, lsr_ref[d])
        cfr, cfi = _zoh_coef(lr, li, zr, zi)
        btr, bti = _cmul(cfr, cfi, bt_ref[d, 0], bt_ref[d, 1])
        powers = [(jnp.ones_like(zr), jnp.zeros_like(zi))]
        for _ in range(tc):
            powers.append(_cmul(powers[-1][0], powers[-1][1], zr, zi))
        order = range(tc - 1, -1, -1) if d == 0 else range(tc)
        rows_per_step = 2 * hg
        pr = jnp.concatenate([jnp.broadcast_to(powers[e][0], (rows_per_step, 2 * p)) for e in order], axis=0)
        pi = jnp.concatenate([jnp.broadcast_to(powers[e][1], (rows_per_step, 2 * p)) for e in order], axis=0)
        ir, ii = _cmul(pr, pi, jnp.tile(btr, (2 * tc, 1)), jnp.tile(bti, (2 * tc, 1)))
        wb_o[:, (2 * d) * LANE:(2 * d + 1) * LANE] = jnp.where(same_group, ir, 0.0).astype(wb_o.dtype)
        wb_o[:, (2 * d + 1) * LANE:(2 * d + 2) * LANE] = jnp.where(same_group, ii, 0.0).astype(wb_o.dtype)
        l16_rows += [powers[tc][0], powers[tc][1]]
        lrc, lic = lrc_ref[d], lic_ref[d]
        zcr, zci = _lam_bar(lrc, lic, lsc_ref[d])
        tau = tau_of_lane if d == 0 else (tc - 1) - tau_of_lane
        pr, pi = _cpow(_squarings(zcr, zci, 4), jnp.broadcast_to(tau, (2 * p, w)))
        c0r, c0i = _cmul(ct_ref[d, 0], ct_ref[d, 1], pr, pi)
        c1r, c1i = _cmul(c0r, c0i, zcr, zci)
        for r, val in ((2 * d, c1r), (2 * d + 1, -c1i)):
            vb = val.astype(MXU_DTYPE)
            spread = jnp.where(first_rows, jnp.dot(vb, s0_ref[...], preferred_element_type=F32),
                               jnp.dot(vb, s1_ref[...], preferred_element_type=F32))
            wc_o[r * LANE:(r + 1) * LANE, :] = spread.astype(wc_o.dtype)
        for gl in range(2):
            in_group = gl_of_lane == gl
            kt = (jnp.dot(jnp.where(in_group, btr, 0.0), c0r, preferred_element_type=F32,
                          precision=lax.Precision.HIGHEST)
                  - jnp.dot(jnp.where(in_group, bti, 0.0), c0i, preferred_element_type=F32,
                            precision=lax.Precision.HIGHEST))
            for s in range(tc):
                if d == 0:
                    shift, keep = hg * s, lane_w >= hg * s
                else:
                    shift, keep = (w - hg * (tc - 1 - s)) % w, lane_w < hg * (s + 1)
                rolled = kt if shift == 0 else pltpu.roll(kt, shift, 1)
                tg_ref[gl, s * hg:(s + 1) * hg, :] += jnp.where(keep, rolled, 0.0)
    for gl, s_ref in enumerate((s0_ref, s1_ref)):
        spread = jnp.dot(tg_ref[gl].astype(MXU_DTYPE), s_ref[...], preferred_element_type=F32)
        for s in range(tc):
            wt_o[(2 * s + gl) * hg:(2 * s + gl + 1) * hg, :] = spread[s * hg:(s + 1) * hg].astype(wt_o.dtype)
    l16_o[...] = jnp.concatenate(l16_rows + [jnp.zeros((4, 2 * p), F32)], axis=0)


def _ssm_operators(lam_re, lam_im, log_step, b_re, b_im, c_re, c_im):
    depth = lam_re.shape[0]
    p, hg, tc, q = SSM_STATE, SSM_GROUP_CH, SSM_CHUNK, SSM_PAIRS
    assert (hg, p, tc) == (16, 64, 16), "lane/row index arithmetic in the kernel uses these as shifts"
    w = tc * hg

    def per_pair(a, tail):
        a = a.astype(F32).reshape((depth, 2, q, 2) + tail)
        return jnp.transpose(a, (0, 2, 1, 3) + tuple(range(4, 4 + len(tail))))

    lam_r, lam_i = per_pair(lam_re, (p,)), per_pair(lam_im, (p,))
    ls = jnp.broadcast_to(per_pair(log_step, ())[..., None], lam_r.shape)
    rows = [a.reshape(depth, q, 2, 1, 2 * p) for a in (lam_r, lam_i, ls)]
    cols = [a.reshape(depth, q, 2, 2 * p, 1) for a in (lam_r, lam_i, ls)]
    bt = jnp.stack([per_pair(b_re, (p, hg)), per_pair(b_im, (p, hg))], axis=3)
    bt = jnp.transpose(bt, (0, 1, 2, 3, 6, 4, 5)).reshape(depth, q, 2, 2, hg, 2 * p)
    ct = jnp.stack([per_pair(c_re, (hg, p)), per_pair(c_im, (hg, p))], axis=3)
    ct = jnp.transpose(ct, (0, 1, 2, 3, 4, 6, 5)).reshape(depth, q, 2, 2, 2 * p, hg)
    ct = jnp.tile(ct, (1, 1, 1, 1, 1, tc))
    r, c = np.arange(w)[:, None], np.arange(2 * w)[None, :]
    hit = (r // hg == c // (2 * hg)) & (r % hg == c % hg)
    spread = [jnp.asarray(hit & ((c // hg) % 2 == gl), MXU_DTYPE) for gl in range(2)]

    blk = lambda a: pl.BlockSpec((None, None) + a.shape[2:], lambda l, i: (l, i) + (0,) * (a.ndim - 2))
    const = pl.BlockSpec((w, 2 * w), lambda l, i: (0, 0))
    mat = pl.BlockSpec((None, None, 2 * w, 2 * w), lambda l, i: (l, i, 0, 0))
    args = rows + cols + [bt, ct]
    return pl.pallas_call(
        _ssm_ops_kernel,
        grid=(depth, q),
        in_specs=[blk(a) for a in args] + [const, const],
        out_specs=[mat, mat, mat, pl.BlockSpec((None, None, 8, 2 * p), lambda l, i: (l, i, 0, 0))],
        out_shape=[jax.ShapeDtypeStruct((depth, q, 2 * w, 2 * w), MXU_DTYPE)] * 3
        + [jax.ShapeDtypeStruct((depth, q, 8, 2 * p), F32)],
        scratch_shapes=[pltpu.VMEM((2, w, w), F32)],
        compiler_params=_params(("parallel", "parallel")),
        name="ssm_ops",
    )(*args, *spread)


_PAIRS_PER_TILE = 4
_SSM_ROW_BLOCK = 64
_SCAN_ROW_PAD = 8


def _ssm_kernel(zu_ref, wt_ref, wb_ref, wc_ref, l16_ref, h0_ref, y_o, fin_o, u_ref, yp_ref, *state_refs,
                batch, seq_len):
    tc, npair = SSM_CHUNK, _PAIRS_PER_TILE
    n_chunks = seq_len // tc
    nrows = batch * n_chunks
    rb = _SSM_ROW_BLOCK
    rs = n_chunks + _SCAN_ROW_PAD
    slot_w = LANE // npair
    slot = jnp.right_shift(lax.broadcasted_iota(jnp.int32, (1, LANE), 1), 5)
    dx_refs, xs_refs = state_refs[:4], state_refs[4:]

    def place(pieces, src_slot):
        offset = src_slot
        out = None
        for j, piece in enumerate(pieces):
            shift = (slot_w * (j - offset[j])) % LANE
            r = piece if shift == 0 else pltpu.roll(piece, shift, 1)
            out = r if out is None else jnp.where(slot == j, r, out)
        return out

    def gather_block(i, carry):
        r0 = pl.multiple_of(i * rb, rb)
        steps = [zu_ref[pl.ds(r0 * tc + s, rb, stride=tc), :] for s in range(tc)]
        for p in range(npair):
            tiles = [place(steps[4 * k:4 * k + 4], [p] * 4) for k in range(tc // 4)]
            u_ref[p, pl.ds(r0, rb), :] = jnp.concatenate(tiles, axis=1).astype(u_ref.dtype)
        return carry

    lax.fori_loop(0, nrows // rb, gather_block, 0)

    for p in range(npair):
        u = u_ref[p]
        y_intra = jnp.dot(u, wt_ref[p], preferred_element_type=F32)
        dx = jnp.dot(u, wb_ref[p], preferred_element_type=F32)
        for r in range(4):
            for b in range(batch):
                dx_refs[r][b * rs:b * rs + n_chunks, :] = dx[b * n_chunks:(b + 1) * n_chunks, r * LANE:(r + 1) * LANE]
        lfr, lfi, lbr, lbi = (l16_ref[p, r:r + 1, :] for r in range(4))

        def body(c, carry):
            fr, fi, br, bi = carry
            fwd = pl.ds(c, batch, stride=rs)
            bwd = pl.ds(n_chunks - 1 - c, batch, stride=rs)
            xs_refs[0][fwd, :] = fr
            xs_refs[1][fwd, :] = fi
            xs_refs[2][bwd, :] = br
            xs_refs[3][bwd, :] = bi
            nfr = lfr * fr - lfi * fi + dx_refs[0][fwd, :]
            nfi = lfr * fi + lfi * fr + dx_refs[1][fwd, :]
            nbr = lbr * br - lbi * bi + dx_refs[2][bwd, :]
            nbi = lbr * bi + lbi * br + dx_refs[3][bwd, :]
            return nfr, nfi, nbr, nbi

        fin = lax.fori_loop(0, n_chunks, body, tuple(h0_ref[p, :, r * LANE:(r + 1) * LANE] for r in range(4)),
                            unroll=4)
        for r in range(4):
            fin_o[p, :, r * LANE:(r + 1) * LANE] = fin[r]
        xs = jnp.concatenate(
            [jnp.concatenate([x[b * rs:b * rs + n_chunks, :] for b in range(batch)], axis=0) for x in xs_refs],
            axis=1).astype(MXU_DTYPE)
        yp_ref[p] = y_intra + jnp.dot(xs, wc_ref[p], preferred_element_type=F32)

    def scatter_block(i, carry):
        r0 = pl.multiple_of(i * rb, rb)
        for k in range(tc // 4):
            pieces = [yp_ref[p, pl.ds(r0, rb), k * LANE:(k + 1) * LANE] for p in range(npair)]
            for j in range(4):
                y_o[pl.ds(r0 * tc + 4 * k + j, rb, stride=tc), :] = place(pieces, [j] * npair)
        return carry

    lax.fori_loop(0, nrows // rb, scatter_block, 0)


def _ssm_scan(zu2d, w_t, w_b, w_c, l16, h0, batch, seq_len):
    t = zu2d.shape[0]
    nrows = t // SSM_CHUNK
    npair = _PAIRS_PER_TILE
    assert nrows % _SSM_ROW_BLOCK == 0 and seq_len % SSM_CHUNK == 0
    once = pl.Buffered(1)
    wspec = pl.BlockSpec((npair, 512, 512), lambda i: (i, 0, 0))
    return pl.pallas_call(
        functools.partial(_ssm_kernel, batch=batch, seq_len=seq_len),
        grid=(SSM_PAIRS // npair,),
        in_specs=[
            pl.BlockSpec((t, LANE), lambda i: (0, i), pipeline_mode=once),
            wspec, wspec, wspec,
            pl.BlockSpec((npair, 8, LANE), lambda i: (i, 0, 0)),
            pl.BlockSpec((npair, batch, 512), lambda i: (i, 0, 0)),
        ],
        out_specs=[
            pl.BlockSpec((t, LANE), lambda i: (0, i), pipeline_mode=once),
            pl.BlockSpec((npair, batch, 512), lambda i: (i, 0, 0)),
        ],
        out_shape=[jax.ShapeDtypeStruct((t, SSM_WIDTH), F32),
                   jax.ShapeDtypeStruct((SSM_PAIRS, batch, 512), F32)],
        scratch_shapes=[pltpu.VMEM((npair, nrows, 512), MXU_DTYPE), pltpu.VMEM((npair, nrows, 512), F32)]
        + [pltpu.VMEM((batch * (seq_len // SSM_CHUNK + _SCAN_ROW_PAD), LANE), F32)] * 8,
        compiler_params=_params(("arbitrary",)),
        name="ssm_scan",
    )(zu2d, w_t, w_b, w_c, l16, h0)


def _pack_state(s_re, s_im):
    b = s_re.shape[0]
    a = jnp.stack([s_re, s_im], axis=2).reshape(b, 2, 2, SSM_PAIRS, 2 * SSM_STATE)
    return jnp.transpose(a, (3, 0, 1, 2, 4)).reshape(SSM_PAIRS, b, 4 * 2 * SSM_STATE).astype(F32)


def _unpack_state(fin):
    b = fin.shape[1]
    a = jnp.transpose(fin.reshape(SSM_PAIRS, b, 2, 2, 2 * SSM_STATE), (1, 2, 3, 0, 4))
    a = a.reshape(b, 2, 2, SSM_GROUPS, SSM_STATE)
    return a[:, :, 0], a[:, :, 1]


def _merge_kernel(ya_ref, ys_ref, u_ref, yc_ref, g_ref, x_ref, mod_ref, ng_ref, d_ref,
                  wglu_ref, wa_ref, wb_ref, wc_ref, wo_ref, x_o, h_o):
    y = ys_ref[...].astype(F32) + d_ref[...] * u_ref[...]
    gl = _gelu_tanh(y)
    yb = gl * _sigmoid(_mm(gl, wglu_ref[...]))
    merged = (g_ref[:, 0:D_MODEL].astype(F32) * _mm(ya_ref[...], wa_ref[...])
              + g_ref[:, D_MODEL:2 * D_MODEL].astype(F32) * _mm(yb, wb_ref[...])
              + g_ref[:, 2 * D_MODEL:3 * D_MODEL].astype(F32) * _mm(yc_ref[...], wc_ref[...]))
    x1 = x_ref[...] + mod_ref[2:3, :] * _rms(_mm(merged, wo_ref[...]), ng_ref[1:2, :])
    x_o[...] = x1
    h_o[...] = (_rms(x1, ng_ref[2:3, :]) * (1.0 + mod_ref[4:5, :]) + mod_ref[3:4, :]).astype(h_o.dtype)


def _merge(ya, ys, zu, yc, gates, x2d, mod_l, mod_row, ng, ssm_d, w_glu, w_a, w_b, w_c, w_o):
    t = x2d.shape[0]
    tm = TOKEN_TILE
    row = lambda i: (i, 0)
    const = lambda i: (0, 0)
    r512 = pl.BlockSpec((tm, 512), row)
    wbr = pl.BlockSpec((512, D_MODEL), const)
    return pl.pallas_call(
        _merge_kernel,
        grid=(t // tm,),
        in_specs=[
            r512, r512, r512, r512,
            pl.BlockSpec((tm, 3 * D_MODEL), row),
            pl.BlockSpec((tm, D_MODEL), row),
            pl.BlockSpec((None, 6, D_MODEL), lambda i: (mod_row(i, tm), 0, 0)),
            pl.BlockSpec((4, D_MODEL), const),
            pl.BlockSpec((1, 512), const),
            pl.BlockSpec((512, 512), const),
            wbr, wbr, wbr,
            pl.BlockSpec((D_MODEL, D_MODEL), const),
        ],
        out_specs=[pl.BlockSpec((tm, D_MODEL), row), pl.BlockSpec((tm, D_MODEL), row)],
        out_shape=[jax.ShapeDtypeStruct((t, D_MODEL), F32), jax.ShapeDtypeStruct((t, D_MODEL), MXU_DTYPE)],
        compiler_params=_params(("parallel",)),
        name="merge",
    )(ya, ys, zu, yc, gates, x2d, mod_l, ng, ssm_d, w_glu, w_a, w_b, w_c, w_o)


def _ffn_kernel(*refs, seq_len, first, last):
    refs = list(refs)
    h_ref, hp_ref, hn_ref = refs[:3]
    del refs[:3]
    part_ref = None if first else refs.pop(0)
    if last:
        x_ref, mod_ref, ng_ref = refs[:3]
        del refs[:3]
    wa_ref, wg_ref, cwa_ref, cwg_ref, cba_ref, cbg_ref, wd_ref, out_ref, ua_ref, ug_ref, act_ref = refs
    i = pl.program_id(0)
    tm = h_ref.shape[0]
    ft = wd_ref.shape[0]
    n = tm + 16
    start = jnp.bitwise_and(i * tm, seq_len - 1)
    keep_prev = (start != 0).astype(F32)
    keep_next = (jnp.bitwise_and(start + tm, seq_len - 1) != 0).astype(F32)
    hh = jnp.concatenate([(hp_ref[...] * keep_prev).astype(h_ref.dtype), h_ref[...],
                          (hn_ref[...] * keep_next).astype(h_ref.dtype)], axis=0)
    ua_ref[...] = jnp.dot(hh, wa_ref[...], preferred_element_type=F32)
    ug_ref[...] = jnp.dot(hh, wg_ref[...], preferred_element_type=F32)
    interior = tm > seq_len
    if interior:
        pos = jnp.bitwise_and(lax.broadcasted_iota(jnp.int32, (tm, 1), 0), seq_len - 1)
        has_prev = (pos != 0).astype(F32)
        has_next = (pos != seq_len - 1).astype(F32)

    def conv(u_ref, cw_ref, cb_ref, lo):
        uc = u_ref[:, lo:lo + LANE]
        up = pltpu.roll(uc, 1, 0)[8:8 + tm]
        un = pltpu.roll(uc, n - 1, 0)[8:8 + tm]
        if interior:
            up, un = up * has_prev, un * has_next
        return (cw_ref[0:1, lo:lo + LANE] * up + cw_ref[1:2, lo:lo + LANE] * uc[8:8 + tm]
                + cw_ref[2:3, lo:lo + LANE] * un + cb_ref[0:1, lo:lo + LANE])

    for kc in range(ft // LANE):
        a = conv(ua_ref, cwa_ref, cba_ref, kc * LANE)
        g = conv(ug_ref, cwg_ref, cbg_ref, kc * LANE)
        act_ref[:, kc * LANE:(kc + 1) * LANE] = (g * _sigmoid(g) * a).astype(act_ref.dtype)
    total = jnp.dot(act_ref[...], wd_ref[...], preferred_element_type=F32)
    if not first:
        total = total + part_ref[...]
    if last:
        out_ref[...] = x_ref[...] + mod_ref[5:6, :] * _rms(total, ng_ref[3:4, :])
    else:
        out_ref[...] = total


def _conv_ffn(h2, x1, mod_l, mod_row, ng, w_up, conv_w, conv_b, w_down, seq_len):
    t = x1.shape[0]
    tm = FFN_TOKEN_TILE
    ft = FF_TILE
    nf = D_FF // ft
    nblk8 = t // 8
    assert seq_len & (seq_len - 1) == 0 and (seq_len % tm == 0 or tm % seq_len == 0) and t % tm == 0
    row = pl.BlockSpec((tm, D_MODEL), lambda i: (i, 0))
    once = pl.Buffered(1)
    part = None
    for j in range(nf):
        first, last = j == 0, j == nf - 1
        in_specs = [
            row,
            pl.BlockSpec((8, D_MODEL), lambda i: (jnp.maximum(i * (tm // 8) - 1, 0), 0)),
            pl.BlockSpec((8, D_MODEL), lambda i: (jnp.minimum((i + 1) * (tm // 8), nblk8 - 1), 0)),
        ]
        args = [h2, h2, h2]
        if not first:
            in_specs.append(row)
            args.append(part)
        if last:
            in_specs += [row, pl.BlockSpec((None, 6, D_MODEL), lambda i: (mod_row(i, tm), 0, 0)),
                         pl.BlockSpec((4, D_MODEL), lambda i: (0, 0))]
            args += [x1, mod_l, ng]
        in_specs += [
            pl.BlockSpec((D_MODEL, ft), lambda i, j=j: (0, j), pipeline_mode=once),
            pl.BlockSpec((D_MODEL, ft), lambda i, j=j: (0, nf + j), pipeline_mode=once),
            pl.BlockSpec((3, ft), lambda i, j=j: (0, j)),
            pl.BlockSpec((3, ft), lambda i, j=j: (0, nf + j)),
            pl.BlockSpec((1, ft), lambda i, j=j: (0, j)),
            pl.BlockSpec((1, ft), lambda i, j=j: (0, nf + j)),
            pl.BlockSpec((ft, D_MODEL), lambda i, j=j: (j, 0), pipeline_mode=once),
        ]
        args += [w_up, w_up, conv_w, conv_w, conv_b, conv_b, w_down]
        part = pl.pallas_call(
            functools.partial(_ffn_kernel, seq_len=seq_len, first=first, last=last),
            grid=(t // tm,),
            in_specs=in_specs,
            out_specs=row,
            out_shape=jax.ShapeDtypeStruct((t, D_MODEL), F32),
            scratch_shapes=[pltpu.VMEM((tm + 16, ft), F32), pltpu.VMEM((tm + 16, ft), F32),
                            pltpu.VMEM((tm, ft), MXU_DTYPE)],
            compiler_params=_params(("parallel",)),
            name="conv_ffn_last" if last else "conv_ffn_part",
        )(*args)
    return part


_Q_HEAD_ORDER = (0, 4, 1, 5, 2, 6, 3, 7)


def _rope_tables(seq_len):
    nf = HEAD_DIM // 4
    t = np.arange(seq_len)
    pos = np.stack([t // GRID_W, t % GRID_W]).astype(np.float32)
    inv = jnp.asarray(ROPE_THETA, F32) ** (-jnp.arange(nf, dtype=F32) / nf)
    ang = jnp.asarray(pos)[:, :, None] * inv
    d = np.arange(HEAD_DIM)
    ang = ang[d // (2 * nf), :, d % nf].T
    second = jnp.asarray(((d % (2 * nf)) // nf) == 1)[None, :]
    cos, sin = jnp.cos(ang), jnp.sin(ang)
    tabs = (cos, jnp.where(second, 0.0, -sin), jnp.where(second, sin, 0.0))
    return tuple(jnp.tile(x, (1, LANE // HEAD_DIM)).astype(F32) for x in tabs)


def _layer_weights(w_in, qk_g, w_br_a):
    hd = HEAD_DIM
    w_in_p = jnp.concatenate([w_in[:, h * hd:(h + 1) * hd] for h in _Q_HEAD_ORDER] + [w_in[:, 512:]],
                             axis=1).astype(MXU_DTYPE)
    w_a_p = jnp.concatenate([w_br_a[h * hd:(h + 1) * hd] for h in _Q_HEAD_ORDER], axis=0).astype(MXU_DTYPE)
    qg = jnp.tile(qk_g[0], N_HEADS).reshape(1, 512).astype(F32)
    kg = jnp.tile(qk_g[1], GA_KV_HEADS).reshape(1, LANE).astype(F32)
    return w_in_p, w_a_p, qg, kg


def kernel(x_prompt, x_sample, c, cache_ga_k, cache_ga_v, cache_na_k, cache_na_v, state_ssm_re, state_ssm_im,
           c_ctx, w_mod, b_mod, norm_g, w_in, qk_norm_g, na_rpb, ssm_lam_re, ssm_lam_im, ssm_log_step,
           ssm_b_re, ssm_b_im, ssm_c_re, ssm_c_im, ssm_d, w_glu, w_br_a, w_br_b, w_br_c, w_out,
           w_up, conv_w, conv_b, w_down):
    depth = w_in.shape[0]
    bp, lp, _ = x_prompt.shape
    bs, ls, _ = x_sample.shape
    lc = cache_ga_k.shape[2]
    assert lp % 256 == 0 and ls % FFN_TOKEN_TILE == 0 and (bp * lp) % FFN_TOKEN_TILE == 0
    assert FFN_TOKEN_TILE % TOKEN_TILE == 0
    assert bs % 8 == 0 and bp % 8 == 0, "the scan keeps one batch row per sublane"

    rows = 1 + bs
    rows_p = -(-rows // 8) * 8
    cvec = jnp.concatenate([c_ctx[None], c, jnp.zeros((rows_p - rows, D_MODEL), F32)], axis=0)
    mod = _modulation(cvec, w_mod, b_mod).reshape(depth, rows_p, 6, D_MODEL)

    w_t, w_b, w_c, l16 = _ssm_operators(ssm_lam_re, ssm_lam_im, ssm_log_step, ssm_b_re, ssm_b_im,
                                        ssm_c_re, ssm_c_im)
    seg = jnp.asarray(np.kron(np.eye(N_HEADS), np.full((HEAD_DIM, HEAD_DIM), 1.0 / HEAD_DIM)), MXU_DTYPE)
    rope_tabs = _rope_tables(ls)
    ctx_row = lambda i, tm: 0
    lat_row = lambda i, tm: 1 + (i * tm) // ls

    y_p = x_prompt.reshape(bp * lp, D_MODEL)
    y_s = x_sample.reshape(bs * ls, D_MODEL)
    zero_state = jnp.zeros((SSM_PAIRS, bp, 512), F32)
    cache_ga = [a.reshape(bs, depth, lc, LANE) for a in (cache_ga_k, cache_ga_v)]
    cache_na = [a.reshape(bs, depth, lc, 512) for a in (cache_na_k, cache_na_v)]
    states = ([], [])
    kv_stacks = None
    for l in range(depth):
        w_in_p, w_a_p, qg, kg = _layer_weights(w_in[l], qk_norm_g[l], w_br_a[l])
        w_glu_l, w_b_l, w_c_l, w_o_l = (a[l].astype(MXU_DTYPE) for a in (w_glu, w_br_b, w_br_c, w_out))
        ffn_w = (w_up[l].astype(MXU_DTYPE), conv_w[l].astype(F32), conv_b[l].reshape(1, 2 * D_FF).astype(F32),
                 w_down[l].astype(MXU_DTYPE))
        d_l = ssm_d[l].reshape(1, SSM_WIDTH).astype(F32)
        ng = norm_g[l].astype(F32)
        ssm_ops = (w_t[l], w_b[l], w_c[l], l16[l])

        q, k, v, zu, nq, nk, nv, gates = _in_projection(
            y_p, mod[l], ctx_row, ng, w_in_p, qg, kg, seg, None, lp, F32, stack=(l, depth, kv_stacks))
        kv_stacks = (k, v, nk, nv)
        r3 = lambda a: a.reshape(bp, lp, a.shape[-1])
        ya = _attention(r3(q), k, v, None, None, lp, "ga_ctx", kv_layer=l)
        yc = _attention(r3(nq), nk, nv, None, None, lp, "na_ctx", kv_layer=l)
        ys, fin = _ssm_scan(zu, *ssm_ops, zero_state, bp, lp)
        x1, h2 = _merge(ya.reshape(-1, 512), ys, zu, yc.reshape(-1, 512), gates, y_p,
                        mod[l], ctx_row, ng, d_l, w_glu_l, w_a_p, w_b_l, w_c_l, w_o_l)
        y_p = _conv_ffn(h2, x1, mod[l], ctx_row, ng, *ffn_w, lp)
        f_re, f_im = _unpack_state(fin)
        states[0].append(f_re)
        states[1].append(f_im)

        q, k, v, zu, nq, nk, nv, gates = _in_projection(
            y_s, mod[l], lat_row, ng, w_in_p, qg, kg, seg, rope_tabs, ls, MXU_DTYPE)
        r3 = lambda a: a.reshape(bs, ls, a.shape[-1])
        ya = _attention(r3(q), r3(k), r3(v), cache_ga[0], cache_ga[1], 4 * GRID_W, "ga_lat", cache_layer=l)
        yc = _neighbourhood_attention(r3(nq), r3(nk), r3(nv), cache_na[0], cache_na[1], l, na_rpb[l])
        h0 = _pack_state(state_ssm_re[:, l], state_ssm_im[:, l])
        ys, _ = _ssm_scan(zu, *ssm_ops, h0, bs, ls)
        x1, h2 = _merge(ya.reshape(-1, 512), ys, zu, yc.reshape(-1, 512), gates, y_s,
                        mod[l], lat_row, ng, d_l, w_glu_l, w_a_p, w_b_l, w_c_l, w_o_l)
        y_s = _conv_ffn(h2, x1, mod[l], lat_row, ng, *ffn_w, ls)

    k, v, nk, nv = kv_stacks
    return (y_p.reshape(bp, lp, D_MODEL), y_s.reshape(bs, ls, D_MODEL),
            k.reshape(bp, depth, lp, GA_KV_HEADS, HEAD_DIM), v.reshape(bp, depth, lp, GA_KV_HEADS, HEAD_DIM),
            nk.reshape(bp, depth, lp, N_HEADS, HEAD_DIM), nv.reshape(bp, depth, lp, N_HEADS, HEAD_DIM),
            jnp.stack(states[0], axis=1), jnp.stack(states[1], axis=1))
```

```python
import functools
import math

import numpy as np
import jax
import jax.numpy as jnp
from jax import lax
from jax.experimental import pallas as pl
from jax.experimental.pallas import tpu as pltpu

F32 = jnp.float32
MXU_DTYPE = jnp.bfloat16

D_MODEL = 1024
HEAD_DIM = 64
N_HEADS = 8
GA_KV_HEADS = 2
GRID_W = 64
NA_WIN_ROWS = 8
NA_WIN_COLS = 16
NA_KEY_ROWS = 10
SSM_WIDTH = 512
SSM_GROUPS = 32
SSM_GROUP_CH = 16
SSM_STATE = 64
SSM_CHUNK = 16
SSM_PAIRS = SSM_GROUPS // 2
D_FF = 2816
FF_TILE = 1408
ROPE_THETA = 10000.0
EPS = 1e-6
IN_WIDTH = 5888
NEG_BIG = -1e30

LANE = 128
TOKEN_TILE = 512
FFN_TOKEN_TILE = 512
VMEM_LIMIT = 56 * 1024 * 1024

_Q0, _K0, _V0, _U0, _NQ0, _NK0, _NV0, _G0 = 0, 512, 640, 768, 1280, 1792, 2304, 2816


def _sigmoid(x):
    return 1.0 / (1.0 + jnp.exp(-x))


def _gelu_tanh(x):
    return 0.5 * x * (1.0 + jnp.tanh(math.sqrt(2.0 / math.pi) * (x + 0.044715 * (x * x * x))))


def _rms(x, g):
    ms = jnp.mean(x * x, axis=-1, keepdims=True)
    return (x * lax.rsqrt(ms + EPS)) * g


def _mm(a, b):
    return jnp.dot(a.astype(MXU_DTYPE), b.astype(MXU_DTYPE), preferred_element_type=F32)


def _mm_nt(a, b):
    return lax.dot_general(a.astype(MXU_DTYPE), b.astype(MXU_DTYPE), (((1,), (1,)), ((), ())),
                           preferred_element_type=F32)


def _params(sem):
    return pltpu.CompilerParams(dimension_semantics=sem, vmem_limit_bytes=VMEM_LIMIT)


def _mod_kernel(c_ref, w_ref, b_ref, o_ref):
    c = c_ref[...]
    o_ref[...] = _mm(c * _sigmoid(c), w_ref[...]) + b_ref[...]


def _modulation(cvec, w_mod, b_mod):
    depth = w_mod.shape[0]
    rows = cvec.shape[0]
    tn = 1536
    return pl.pallas_call(
        _mod_kernel,
        grid=(depth, 6 * D_MODEL // tn),
        in_specs=[
            pl.BlockSpec((rows, D_MODEL), lambda l, j: (0, 0)),
            pl.BlockSpec((None, D_MODEL, tn), lambda l, j: (l, 0, j)),
            pl.BlockSpec((None, 1, tn), lambda l, j: (l, 0, j)),
        ],
        out_specs=pl.BlockSpec((None, rows, tn), lambda l, j: (l, 0, j)),
        out_shape=jax.ShapeDtypeStruct((depth, rows, 6 * D_MODEL), F32),
        compiler_params=_params(("parallel", "parallel")),
        name="adaln_mod",
    )(cvec, w_mod, b_mod.reshape(depth, 1, 6 * D_MODEL))


def _head_rms(z, seg, gain):
    ms = jnp.dot((z * z).astype(MXU_DTYPE), seg, preferred_element_type=F32)
    return (z * lax.rsqrt(ms + EPS)) * gain


def _rope_tile(t, c, s_up, s_dn):
    return t * c + pltpu.roll(t, LANE - 16, 1) * s_up + pltpu.roll(t, 16, 1) * s_dn


def _inproj_kernel(*refs, rope, n_carried, new_stack_layer=None):
    n_in = 10 if rope else 7
    refs = refs[:n_in] + refs[n_in + n_carried:]
    if rope:
        (x_ref, mod_ref, ng_ref, w_ref, qg_ref, kg_ref, seg_ref, cos_ref, sup_ref, sdn_ref,
         q_o, k_o, v_o, u_o, nq_o, nk_o, nv_o, g_o) = refs
    else:
        (x_ref, mod_ref, ng_ref, w_ref, qg_ref, kg_ref, seg_ref,
         q_o, k_o, v_o, u_o, nq_o, nk_o, nv_o, g_o) = refs
    x = x_ref[...]
    h = _rms(x, ng_ref[0:1, :]) * (1.0 + mod_ref[1:2, :]) + mod_ref[0:1, :]
    hb = h.astype(MXU_DTYPE)
    scale = HEAD_DIM ** -0.5

    def proj(lo, width):
        return jnp.dot(hb, w_ref[:, lo:lo + width], preferred_element_type=F32)

    def maybe_rope(z):
        if not rope:
            return z
        c, su, sd = cos_ref[...], sup_ref[...], sdn_ref[...]
        tiles = [_rope_tile(z[:, i * LANE:(i + 1) * LANE], c, su, sd) for i in range(z.shape[1] // LANE)]
        return tiles[0] if len(tiles) == 1 else jnp.concatenate(tiles, axis=1)

    def put(o_ref, val):
        val = val.astype(o_ref.dtype)
        if new_stack_layer is None:
            o_ref[...] = val.reshape(o_ref.shape)
            return
        nseq, depth, seq, width = o_ref.shape
        for d in range(depth):
            o_ref[:, d] = val.reshape(nseq, seq, width) if d == new_stack_layer else jnp.zeros(
                (nseq, seq, width), o_ref.dtype)

    q = maybe_rope(_head_rms(proj(_Q0, 512), seg_ref[...], qg_ref[...]))
    q_o[...] = (q * scale).astype(q_o.dtype)
    put(k_o, maybe_rope(_head_rms(proj(_K0, 128), seg_ref[0:LANE, 0:LANE], kg_ref[...])))
    put(v_o, proj(_V0, 128))
    u_o[...] = proj(_U0, 512).astype(u_o.dtype)
    nq_o[...] = (proj(_NQ0, 512) * scale).astype(nq_o.dtype)
    put(nk_o, proj(_NK0, 512))
    put(nv_o, proj(_NV0, 512))
    for i in range(3):
        g_o[:, i * D_MODEL:(i + 1) * D_MODEL] = _sigmoid(proj(_G0 + i * D_MODEL, D_MODEL)).astype(g_o.dtype)


def _in_projection(x2d, mod_l, mod_row, ng, w_in, qg, kg, seg, rope_tabs, seq_len, kv_dtype, stack=None):
    t = x2d.shape[0]
    tm = TOKEN_TILE
    tiles_per_seq = max(seq_len // tm, 1)
    rope = rope_tabs is not None
    row = lambda i: (i, 0)
    const = lambda i: (0, 0)
    in_specs = [
        pl.BlockSpec((tm, D_MODEL), row),
        pl.BlockSpec((None, 6, D_MODEL), lambda i: (mod_row(i, tm), 0, 0)),
        pl.BlockSpec((4, D_MODEL), const),
        pl.BlockSpec((D_MODEL, IN_WIDTH), const),
        pl.BlockSpec((1, 512), const),
        pl.BlockSpec((1, LANE), const),
        pl.BlockSpec((512, 512), const),
    ]
    args = [x2d, mod_l, ng, w_in, qg, kg, seg]
    if rope:
        in_specs += [pl.BlockSpec((tm, LANE), lambda i: (i % tiles_per_seq, 0))] * 3
        args += list(rope_tabs)
    widths = (512, 128, 128, 512, 512, 512, 512, 3 * D_MODEL)
    dtypes = (MXU_DTYPE, kv_dtype, kv_dtype, F32, MXU_DTYPE, kv_dtype, kv_dtype, MXU_DTYPE)
    out_specs = [pl.BlockSpec((tm, w), row) for w in widths]
    out_shape = [jax.ShapeDtypeStruct((t, w), dt) for w, dt in zip(widths, dtypes)]
    aliases, n_carried, new_stack_layer = {}, 0, None
    if stack is not None:
        layer, depth, carried = stack
        assert tm % seq_len == 0
        nseq = tm // seq_len
        for o in (1, 2, 5, 6):
            if carried is None:
                out_specs[o] = pl.BlockSpec((nseq, depth, seq_len, widths[o]), lambda i: (i, 0, 0, 0))
            else:
                out_specs[o] = pl.BlockSpec((nseq, None, seq_len, widths[o]), lambda i: (i, layer, 0, 0))
            out_shape[o] = jax.ShapeDtypeStruct((t // seq_len, depth, seq_len, widths[o]), dtypes[o])
        if carried is None:
            new_stack_layer = layer
        else:
            n_carried = len(carried)
            aliases = {len(args) + n: o for n, o in enumerate((1, 2, 5, 6))}
            in_specs += [pl.BlockSpec(memory_space=pl.ANY)] * n_carried
            args += list(carried)
    return pl.pallas_call(
        functools.partial(_inproj_kernel, rope=rope, n_carried=n_carried, new_stack_layer=new_stack_layer),
        grid=(t // tm,),
        in_specs=in_specs,
        out_specs=out_specs,
        out_shape=out_shape,
        input_output_aliases=aliases,
        compiler_params=_params(("parallel",)),
        name="in_proj_rope" if rope else "in_proj",
    )(*args)


def _lane_masks(dtype):
    lane = lax.broadcasted_iota(jnp.int32, (1, LANE), 1)
    lo = lane < HEAD_DIM
    return lo, lo.astype(dtype), (~lo).astype(dtype)


_NA_TILES_PER_STEP = 2
_KEY_BLOCK = 256
_Q_SUB = 128


def _softmax_pv(qs, key_blocks, s_ref):
    macc = None
    for bi, (score_fn, _) in enumerate(key_blocks):
        sj = score_fn(qs)
        s_ref[bi] = sj
        macc = sj if macc is None else jnp.maximum(macc, sj)
    mb = jnp.broadcast_to(jnp.max(macc, axis=-1, keepdims=True), macc.shape)
    lacc = jnp.zeros(macc.shape, F32)
    o = jnp.zeros((qs.shape[0], LANE), F32)
    for bi, (_, v_fn) in enumerate(key_blocks):
        p = jnp.exp(s_ref[bi] - mb)
        lacc = lacc + p
        vb = v_fn()
        o = o + _mm(p[:, :vb.shape[0]], vb)
    return o * (1.0 / jnp.sum(lacc, axis=-1, keepdims=True))


def _attn_kernel(*refs, kv_tiles, cached):
    if cached:
        q_ref, k_ref, v_ref, kc_ref, vc_ref, o_ref, s_ref = refs
    else:
        q_ref, k_ref, v_ref, o_ref, s_ref = refs
    tq = q_ref.shape[0]
    kb = _KEY_BLOCK
    lo, m_lo, m_hi = _lane_masks(MXU_DTYPE)
    pairs_per_kv = (N_HEADS // 2) // kv_tiles
    sources = [(k_ref, v_ref)] + ([(kc_ref, vc_ref)] if cached else [])
    for q0 in range(0, tq, _Q_SUB):
        for hp in range(N_HEADS // 2):
            ksl = slice((hp // pairs_per_kv) * LANE, (hp // pairs_per_kv + 1) * LANE)
            q2 = q_ref[q0:q0 + _Q_SUB, hp * LANE:(hp + 1) * LANE]
            qs = jnp.concatenate([q2 * m_lo, q2 * m_hi], axis=0)
            blocks = [(functools.partial(lambda x, kr, off, ksl: _mm_nt(x, kr[off:off + kb, ksl]),
                                         kr=kr, off=off, ksl=ksl),
                       functools.partial(lambda vr, off, ksl: vr[off:off + kb, ksl], vr=vr, off=off, ksl=ksl))
                      for kr, vr in sources for off in range(0, kr.shape[0], kb)]
            o = _softmax_pv(qs, blocks, s_ref)
            o_ref[q0:q0 + _Q_SUB, hp * LANE:(hp + 1) * LANE] = jnp.where(
                lo, o[:_Q_SUB], o[_Q_SUB:]).astype(o_ref.dtype)


def _attention(q, k, v, kc, vc, tq, name, kv_layer=None, cache_layer=None):
    b, lq, _ = q.shape
    lk, kw = k.shape[-2], k.shape[-1]
    cached = kc is not None
    qmap = lambda bi, ti: (bi, ti, 0)

    def key_spec(n, layer):
        if layer is None:
            return pl.BlockSpec((None, n, kw), lambda bi, ti: (bi, 0, 0))
        return pl.BlockSpec((None, None, n, kw), lambda bi, ti: (bi, layer, 0, 0))

    in_specs = [pl.BlockSpec((None, tq, 512), qmap)] + [key_spec(lk, kv_layer)] * 2
    args = [q, k, v]
    lc = kc.shape[-2] if cached else 0
    if cached:
        in_specs += [key_spec(lc, cache_layer)] * 2
        args += [kc, vc]
    assert lk % _KEY_BLOCK == 0 and lc % _KEY_BLOCK == 0 and tq % _Q_SUB == 0
    n_blocks = (lk + lc) // _KEY_BLOCK
    return pl.pallas_call(
        functools.partial(_attn_kernel, kv_tiles=kw // LANE, cached=cached),
        grid=(b, lq // tq),
        in_specs=in_specs,
        out_specs=pl.BlockSpec((None, tq, 512), qmap),
        out_shape=jax.ShapeDtypeStruct((b, lq, 512), MXU_DTYPE),
        scratch_shapes=[pltpu.VMEM((n_blocks, 2 * _Q_SUB, _KEY_BLOCK), F32)],
        compiler_params=_params(("parallel", "parallel")),
        name=name,
    )(*args)


def _na_geometry(seq_len):
    rows = seq_len // GRID_W
    n_tiles = rows // 2
    assert rows >= NA_KEY_ROWS and NA_WIN_ROWS <= rows and NA_KEY_ROWS % 2 == 0
    ws = np.clip(2 * np.arange(n_tiles) - NA_WIN_ROWS // 2, 0, rows - NA_KEY_ROWS)
    r = 2 * np.arange(n_tiles)[:, None, None] + np.arange(2)[None, :, None]
    key_r = ws[:, None, None] + np.arange(NA_KEY_ROWS)[None, None, :]
    r0 = np.clip(r - NA_WIN_ROWS // 2, 0, rows - NA_WIN_ROWS)
    valid = (key_r >= r0) & (key_r < r0 + NA_WIN_ROWS)
    dr = np.where(valid, key_r - r + NA_WIN_ROWS - 1, 2 * NA_WIN_ROWS - 1)
    assert (valid.sum(-1) == NA_WIN_ROWS).all()
    return ws.astype(np.int32), dr.reshape(-1).astype(np.int32)


def _na_bias_blocks(rpb):
    h = rpb.shape[0]
    nrel = 2 * NA_WIN_ROWS - 1
    zeros = jnp.zeros((h, nrel, LANE - (2 * NA_WIN_COLS - 1)), F32)
    v = jnp.concatenate([rpb[..., NA_WIN_COLS - 1:], zeros, rpb[..., :NA_WIN_COLS - 1]], axis=-1).astype(F32)
    t = jnp.tile(v, (1, 1, GRID_W))[..., :GRID_W * (LANE - 1)].reshape(h, nrel, GRID_W, LANE - 1)[..., :GRID_W]
    c = np.arange(GRID_W)
    c0 = np.clip(c - NA_WIN_COLS // 2, 0, GRID_W - NA_WIN_COLS)
    colmask = (c[None, :] >= c0[:, None]) & (c[None, :] < c0[:, None] + NA_WIN_COLS)
    t = jnp.where(jnp.asarray(colmask)[None, None], t, NEG_BIG)
    t = jnp.concatenate([t, jnp.full((h, 1, GRID_W, GRID_W), NEG_BIG, F32)], axis=1)
    pad = jnp.zeros_like(t)
    return jnp.concatenate([t, pad], axis=-1), jnp.concatenate([pad, t], axis=-1)


def _na_kernel(ws_ref, dr_ref, q_ref, k_ref, v_ref, kc_ref, vc_ref, bl_ref, br_ref, o_ref, s_ref):
    for tt in range(_NA_TILES_PER_STEP):
        _na_tile(pl.program_id(1) * _NA_TILES_PER_STEP + tt, tt * 2 * GRID_W,
                 ws_ref, dr_ref, q_ref, k_ref, v_ref, kc_ref, vc_ref, bl_ref, br_ref, o_ref, s_ref)


def _na_tile(i, q0, ws_ref, dr_ref, q_ref, k_ref, v_ref, kc_ref, vc_ref, bl_ref, br_ref, o_ref, s_ref):
    start = pl.multiple_of(ws_ref[i] * GRID_W, GRID_W)
    nk = NA_KEY_ROWS * GRID_W
    kb = _KEY_BLOCK
    tq = 2 * GRID_W
    lc = kc_ref.shape[0]
    lo, m_lo, m_hi = _lane_masks(MXU_DTYPE)

    def bias_block(hp, off, width):
        rows = []
        for h in (2 * hp, 2 * hp + 1):
            for qr in range(2):
                base = (i * 2 + qr) * NA_KEY_ROWS + off // GRID_W
                tiles = [bl_ref[h, dr_ref[base + 2 * kp]] + br_ref[h, dr_ref[base + 2 * kp + 1]]
                         for kp in range(width // LANE)]
                rows.append(tiles[0] if len(tiles) == 1 else jnp.concatenate(tiles, axis=1))
        return jnp.concatenate(rows, axis=0)

    def local_scores(x, hp, sl, off, width):
        s = _mm_nt(x, k_ref[pl.ds(start + off, width), sl]) + bias_block(hp, off, width)
        if width < kb:
            s = jnp.concatenate([s, jnp.full((s.shape[0], kb - width), NEG_BIG, F32)], axis=1)
        return s

    for hp in range(N_HEADS // 2):
        sl = slice(hp * LANE, (hp + 1) * LANE)
        q2 = q_ref[q0:q0 + tq, sl]
        qs = jnp.concatenate([q2 * m_lo, q2 * m_hi], axis=0)
        blocks = []
        for off in range(0, nk, kb):
            width = min(kb, nk - off)
            blocks.append((functools.partial(local_scores, hp=hp, sl=sl, off=off, width=width),
                           functools.partial(lambda sl, off, width: v_ref[pl.ds(start + off, width), sl],
                                             sl=sl, off=off, width=width)))
        for off in range(0, lc, kb):
            blocks.append((functools.partial(lambda x, sl, off: _mm_nt(x, kc_ref[off:off + kb, sl]), sl=sl, off=off),
                           functools.partial(lambda sl, off: vc_ref[off:off + kb, sl], sl=sl, off=off)))
        o = _softmax_pv(qs, blocks, s_ref)
        o_ref[q0:q0 + tq, sl] = jnp.where(lo, o[:tq], o[tq:]).astype(o_ref.dtype)


def _neighbourhood_attention(q, k, v, kc, vc, cache_layer, rpb):
    b, seq_len, _ = q.shape
    lc = kc.shape[-2]
    ws, dr = _na_geometry(seq_len)
    b_left, b_right = _na_bias_blocks(rpb)
    n_tiles = len(ws)
    assert n_tiles % _NA_TILES_PER_STEP == 0
    tq = 2 * GRID_W
    tb = tq * _NA_TILES_PER_STEP
    qmap = lambda bi, ti, ws_r, dr_r: (bi, ti, 0)
    kmap = lambda bi, ti, ws_r, dr_r: (bi, 0, 0)
    bmap = lambda bi, ti, ws_r, dr_r: (0, 0, 0, 0)
    grid_spec = pltpu.PrefetchScalarGridSpec(
        num_scalar_prefetch=2,
        grid=(b, n_tiles // _NA_TILES_PER_STEP),
        in_specs=[
            pl.BlockSpec((None, tb, 512), qmap),
            pl.BlockSpec((None, seq_len, 512), kmap),
            pl.BlockSpec((None, seq_len, 512), kmap),
            pl.BlockSpec((None, None, lc, 512), lambda bi, ti, ws_r, dr_r: (bi, cache_layer, 0, 0)),
            pl.BlockSpec((None, None, lc, 512), lambda bi, ti, ws_r, dr_r: (bi, cache_layer, 0, 0)),
            pl.BlockSpec(b_left.shape, bmap),
            pl.BlockSpec(b_right.shape, bmap),
        ],
        out_specs=pl.BlockSpec((None, tb, 512), qmap),
        scratch_shapes=[pltpu.VMEM((-(-NA_KEY_ROWS * GRID_W // _KEY_BLOCK) + lc // _KEY_BLOCK, 2 * tq, _KEY_BLOCK),
                                   F32)],
    )
    assert lc % _KEY_BLOCK == 0
    return pl.pallas_call(
        _na_kernel,
        grid_spec=grid_spec,
        out_shape=jax.ShapeDtypeStruct((b, seq_len, 512), MXU_DTYPE),
        compiler_params=_params(("parallel", "arbitrary")),
        name="na_attn",
    )(jnp.asarray(ws), jnp.asarray(dr), q, k, v, kc, vc, b_left, b_right)


def _cmul(ar, ai, br, bi):
    return ar * br - ai * bi, ar * bi + ai * br


def _lam_bar(lr, li, ls):
    dt = jnp.exp(ls)
    mag = jnp.exp(lr * dt)
    return mag * jnp.cos(li * dt), mag * jnp.sin(li * dt)


def _zoh_coef(lr, li, zr, zi):
    nr, ni = zr - 1.0, zi
    den = 1.0 / (lr * lr + li * li)
    return (nr * lr + ni * li) * den, (ni * lr - nr * li) * den


def _squarings(zr, zi, n):
    out = [(zr, zi)]
    for _ in range(n - 1):
        zr, zi = _cmul(zr, zi, zr, zi)
        out.append((zr, zi))
    return out


def _cpow(squares, e):
    pr, pi = jnp.ones(e.shape, F32), jnp.zeros(e.shape, F32)
    for k, (zr, zi) in enumerate(squares):
        bit = jnp.bitwise_and(jnp.right_shift(e, k), 1) == 1
        nr, ni = _cmul(pr, pi, zr, zi)
        pr, pi = jnp.where(bit, nr, pr), jnp.where(bit, ni, pi)
    return pr, pi


def _ssm_ops_kernel(lrr_ref, lir_ref, lsr_ref, lrc_ref, lic_ref, lsc_ref, bt_ref, ct_ref, s0_ref, s1_ref,
                    wt_o, wb_o, wc_o, l16_o, tg_ref):
    tc, hg, p = SSM_CHUNK, SSM_GROUP_CH, SSM_STATE
    w = tc * hg
    lane_w = lax.broadcasted_iota(jnp.int32, (1, w), 1)
    lane_p = lax.broadcasted_iota(jnp.int32, (1, 2 * p), 1)
    row_w = lax.broadcasted_iota(jnp.int32, (2 * w, 1), 0)
    row_p = lax.broadcasted_iota(jnp.int32, (2 * p, 1), 0)
    tau_of_lane = jnp.right_shift(lane_w, 4)
    gl_of_lane = jnp.right_shift(lane_p, 6)
    same_group = jnp.bitwise_and(jnp.right_shift(row_w, 4), 1) == gl_of_lane
    first_rows = row_p < p
    tg_ref[...] = jnp.zeros_like(tg_ref)
    l16_rows = []
    for d in range(2):
        lr, li = lrr_ref[d], lir_ref[d]
        zr, zi = _lam_bar(lr, li, lsr_ref[d])
        cfr, cfi = _zoh_coef(lr, li, zr, zi)
        btr, bti = _cmul(cfr, cfi, bt_ref[d, 0], bt_ref[d, 1])
        powers = [(jnp.ones_like(zr), jnp.zeros_like(zi))]
        for _ in range(tc):
            powers.append(_cmul(powers[-1][0], powers[-1][1], zr, zi))
        order = range(tc - 1, -1, -1) if d == 0 else range(tc)
        rows_per_step = 2 * hg
        pr = jnp.concatenate([jnp.broadcast_to(powers[e][0], (rows_per_step, 2 * p)) for e in order], axis=0)
        pi = jnp.concatenate([jnp.broadcast_to(powers[e][1], (rows_per_step, 2 * p)) for e in order], axis=0)
        ir, ii = _cmul(pr, pi, jnp.tile(btr, (2 * tc, 1)), jnp.tile(bti, (2 * tc, 1)))
        wb_o[:, (2 * d) * LANE:(2 * d + 1) * LANE] = jnp.where(same_group, ir, 0.0).astype(wb_o.dtype)
        wb_o[:, (2 * d + 1) * LANE:(2 * d + 2) * LANE] = jnp.where(same_group, ii, 0.0).astype(wb_o.dtype)
        l16_rows += [powers[tc][0], powers[tc][1]]
        lrc, lic = lrc_ref[d], lic_ref[d]
        zcr, zci = _lam_bar(lrc, lic, lsc_ref[d])
        tau = tau_of_lane if d == 0 else (tc - 1) - tau_of_lane
        pr, pi = _cpow(_squarings(zcr, zci, 4), jnp.broadcast_to(tau, (2 * p, w)))
        c0r, c0i = _cmul(ct_ref[d, 0], ct_ref[d, 1], pr, pi)
        c1r, c1i = _cmul(c0r, c0i, zcr, zci)
        for r, val in ((2 * d, c1r), (2 * d + 1, -c1i)):
            vb = val.astype(MXU_DTYPE)
            spread = jnp.where(first_rows, jnp.dot(vb, s0_ref[...], preferred_element_type=F32),
                               jnp.dot(vb, s1_ref[...], preferred_element_type=F32))
            wc_o[r * LANE:(r + 1) * LANE, :] = spread.astype(wc_o.dtype)
        for gl in range(2):
            in_group = gl_of_lane == gl
            kt = (jnp.dot(jnp.where(in_group, btr, 0.0), c0r, preferred_element_type=F32,
                          precision=lax.Precision.HIGHEST)
                  - jnp.dot(jnp.where(in_group, bti, 0.0), c0i, preferred_element_type=F32,
                            precision=lax.Precision.HIGHEST))
            for s in range(tc):
                if d == 0:
                    shift, keep = hg * s, lane_w >= hg * s
                else:
                    shift, keep = (w - hg * (tc - 1 - s)) % w, lane_w < hg * (s + 1)
                rolled = kt if shift == 0 else pltpu.roll(kt, shift, 1)
                tg_ref[gl, s * hg:(s + 1) * hg, :] += jnp.where(keep, rolled, 0.0)
    for gl, s_ref in enumerate((s0_ref, s1_ref)):
        spread = jnp.dot(tg_ref[gl].astype(MXU_DTYPE), s_ref[...], preferred_element_type=F32)
        for s in range(tc):
            wt_o[(2 * s + gl) * hg:(2 * s + gl + 1) * hg, :] = spread[s * hg:(s + 1) * hg].astype(wt_o.dtype)
    l16_o[...] = jnp.concatenate(l16_rows + [jnp.zeros((4, 2 * p), F32)], axis=0)


def _ssm_operators(lam_re, lam_im, log_step, b_re, b_im, c_re, c_im):
    depth = lam_re.shape[0]
    p, hg, tc, q = SSM_STATE, SSM_GROUP_CH, SSM_CHUNK, SSM_PAIRS
    assert (hg, p, tc) == (16, 64, 16), "lane/row index arithmetic in the kernel uses these as shifts"
    w = tc * hg

    def per_pair(a, tail):
        a = a.astype(F32).reshape((depth, 2, q, 2) + tail)
        return jnp.transpose(a, (0, 2, 1, 3) + tuple(range(4, 4 + len(tail))))

    lam_r, lam_i = per_pair(lam_re, (p,)), per_pair(lam_im, (p,))
    ls = jnp.broadcast_to(per_pair(log_step, ())[..., None], lam_r.shape)
    rows = [a.reshape(depth, q, 2, 1, 2 * p) for a in (lam_r, lam_i, ls)]
    cols = [a.reshape(depth, q, 2, 2 * p, 1) for a in (lam_r, lam_i, ls)]
    bt = jnp.stack([per_pair(b_re, (p, hg)), per_pair(b_im, (p, hg))], axis=3)
    bt = jnp.transpose(bt, (0, 1, 2, 3, 6, 4, 5)).reshape(depth, q, 2, 2, hg, 2 * p)
    ct = jnp.stack([per_pair(c_re, (hg, p)), per_pair(c_im, (hg, p))], axis=3)
    ct = jnp.transpose(ct, (0, 1, 2, 3, 4, 6, 5)).reshape(depth, q, 2, 2, 2 * p, hg)
    ct = jnp.tile(ct, (1, 1, 1, 1, 1, tc))
    r, c = np.arange(w)[:, None], np.arange(2 * w)[None, :]
    hit = (r // hg == c // (2 * hg)) & (r % hg == c % hg)
    spread = [jnp.asarray(hit & ((c // hg) % 2 == gl), MXU_DTYPE) for gl in range(2)]

    blk = lambda a: pl.BlockSpec((None, None) + a.shape[2:], lambda l, i: (l, i) + (0,) * (a.ndim - 2))
    const = pl.BlockSpec((w, 2 * w), lambda l, i: (0, 0))
    mat = pl.BlockSpec((None, None, 2 * w, 2 * w), lambda l, i: (l, i, 0, 0))
    args = rows + cols + [bt, ct]
    return pl.pallas_call(
        _ssm_ops_kernel,
        grid=(depth, q),
        in_specs=[blk(a) for a in args] + [const, const],
        out_specs=[mat, mat, mat, pl.BlockSpec((None, None, 8, 2 * p), lambda l, i: (l, i, 0, 0))],
        out_shape=[jax.ShapeDtypeStruct((depth, q, 2 * w, 2 * w), MXU_DTYPE)] * 3
        + [jax.ShapeDtypeStruct((depth, q, 8, 2 * p), F32)],
        scratch_shapes=[pltpu.VMEM((2, w, w), F32)],
        compiler_params=_params(("parallel", "parallel")),
        name="ssm_ops",
    )(*args, *spread)


_PAIRS_PER_TILE = 4
_SSM_ROW_BLOCK = 64
_SCAN_ROW_PAD = 8


def _ssm_kernel(zu_ref, wt_ref, wb_ref, wc_ref, l16_ref, h0_ref, y_o, fin_o, u_ref, yp_ref, *state_refs,
                batch, seq_len):
    tc, npair = SSM_CHUNK, _PAIRS_PER_TILE
    n_chunks = seq_len // tc
    nrows = batch * n_chunks
    rb = _SSM_ROW_BLOCK
    rs = n_chunks + _SCAN_ROW_PAD
    slot_w = LANE // npair
    slot = jnp.right_shift(lax.broadcasted_iota(jnp.int32, (1, LANE), 1), 5)
    dx_refs, xs_refs = state_refs[:4], state_refs[4:]

    def place(pieces, src_slot):
        offset = src_slot
        out = None
        for j, piece in enumerate(pieces):
            shift = (slot_w * (j - offset[j])) % LANE
            r = piece if shift == 0 else pltpu.roll(piece, shift, 1)
            out = r if out is None else jnp.where(slot == j, r, out)
        return out

    def gather_block(i, carry):
        r0 = pl.multiple_of(i * rb, rb)
        steps = [zu_ref[pl.ds(r0 * tc + s, rb, stride=tc), :] for s in range(tc)]
        for p in range(npair):
            tiles = [place(steps[4 * k:4 * k + 4], [p] * 4) for k in range(tc // 4)]
            u_ref[p, pl.ds(r0, rb), :] = jnp.concatenate(tiles, axis=1).astype(u_ref.dtype)
        return carry

    lax.fori_loop(0, nrows // rb, gather_block, 0)

    for p in range(npair):
        u = u_ref[p]
        y_intra = jnp.dot(u, wt_ref[p], preferred_element_type=F32)
        dx = jnp.dot(u, wb_ref[p], preferred_element_type=F32)
        for r in range(4):
            for b in range(batch):
                dx_refs[r][b * rs:b * rs + n_chunks, :] = dx[b * n_chunks:(b + 1) * n_chunks, r * LANE:(r + 1) * LANE]
        lfr, lfi, lbr, lbi = (l16_ref[p, r:r + 1, :] for r in range(4))

        def body(c, carry):
            fr, fi, br, bi = carry
            fwd = pl.ds(c, batch, stride=rs)
            bwd = pl.ds(n_chunks - 1 - c, batch, stride=rs)
            xs_refs[0][fwd, :] = fr
            xs_refs[1][fwd, :] = fi
            xs_refs[2][bwd, :] = br
            xs_refs[3][bwd, :] = bi
            nfr = lfr * fr - lfi * fi + dx_refs[0][fwd, :]
            nfi = lfr * fi + lfi * fr + dx_refs[1][fwd, :]
            nbr = lbr * br - lbi * bi + dx_refs[2][bwd, :]
            nbi = lbr * bi + lbi * br + dx_refs[3][bwd, :]
            return nfr, nfi, nbr, nbi

        fin = lax.fori_loop(0, n_chunks, body, tuple(h0_ref[p, :, r * LANE:(r + 1) * LANE] for r in range(4)),
                            unroll=4)
        for r in range(4):
            fin_o[p, :, r * LANE:(r + 1) * LANE] = fin[r]
        xs = jnp.concatenate(
            [jnp.concatenate([x[b * rs:b * rs + n_chunks, :] for b in range(batch)], axis=0) for x in xs_refs],
            axis=1).astype(MXU_DTYPE)
        yp_ref[p] = y_intra + jnp.dot(xs, wc_ref[p], preferred_element_type=F32)

    def scatter_block(i, carry):
        r0 = pl.multiple_of(i * rb, rb)
        for k in range(tc // 4):
            pieces = [yp_ref[p, pl.ds(r0, rb), k * LANE:(k + 1) * LANE] for p in range(npair)]
            for j in range(4):
                y_o[pl.ds(r0 * tc + 4 * k + j, rb, stride=tc), :] = place(pieces, [j] * npair)
        return carry

    lax.fori_loop(0, nrows // rb, scatter_block, 0)


def _ssm_scan(zu2d, w_t, w_b, w_c, l16, h0, batch, seq_len):
    t = zu2d.shape[0]
    nrows = t // SSM_CHUNK
    npair = _PAIRS_PER_TILE
    assert nrows % _SSM_ROW_BLOCK == 0 and seq_len % SSM_CHUNK == 0
    once = pl.Buffered(1)
    wspec = pl.BlockSpec((npair, 512, 512), lambda i: (i, 0, 0))
    return pl.pallas_call(
        functools.partial(_ssm_kernel, batch=batch, seq_len=seq_len),
        grid=(SSM_PAIRS // npair,),
        in_specs=[
            pl.BlockSpec((t, LANE), lambda i: (0, i), pipeline_mode=once),
            wspec, wspec, wspec,
            pl.BlockSpec((npair, 8, LANE), lambda i: (i, 0, 0)),
            pl.BlockSpec((npair, batch, 512), lambda i: (i, 0, 0)),
        ],
        out_specs=[
            pl.BlockSpec((t, LANE), lambda i: (0, i), pipeline_mode=once),
            pl.BlockSpec((npair, batch, 512), lambda i: (i, 0, 0)),
        ],
        out_shape=[jax.ShapeDtypeStruct((t, SSM_WIDTH), F32),
                   jax.ShapeDtypeStruct((SSM_PAIRS, batch, 512), F32)],
        scratch_shapes=[pltpu.VMEM((npair, nrows, 512), MXU_DTYPE), pltpu.VMEM((npair, nrows, 512), F32)]
        + [pltpu.VMEM((batch * (seq_len // SSM_CHUNK + _SCAN_ROW_PAD), LANE), F32)] * 8,
        compiler_params=_params(("arbitrary",)),
        name="ssm_scan",
    )(zu2d, w_t, w_b, w_c, l16, h0)


def _pack_state(s_re, s_im):
    b = s_re.shape[0]
    a = jnp.stack([s_re, s_im], axis=2).reshape(b, 2, 2, SSM_PAIRS, 2 * SSM_STATE)
    return jnp.transpose(a, (3, 0, 1, 2, 4)).reshape(SSM_PAIRS, b, 4 * 2 * SSM_STATE).astype(F32)


def _unpack_state(fin):
    b = fin.shape[1]
    a = jnp.transpose(fin.reshape(SSM_PAIRS, b, 2, 2, 2 * SSM_STATE), (1, 2, 3, 0, 4))
    a = a.reshape(b, 2, 2, SSM_GROUPS, SSM_STATE)
    return a[:, :, 0], a[:, :, 1]


def _merge_kernel(ya_ref, ys_ref, u_ref, yc_ref, g_ref, x_ref, mod_ref, ng_ref, d_ref,
                  wglu_ref, wa_ref, wb_ref, wc_ref, wo_ref, x_o, h_o):
    y = ys_ref[...].astype(F32) + d_ref[...] * u_ref[...]
    gl = _gelu_tanh(y)
    yb = gl * _sigmoid(_mm(gl, wglu_ref[...]))
    merged = (g_ref[:, 0:D_MODEL].astype(F32) * _mm(ya_ref[...], wa_ref[...])
              + g_ref[:, D_MODEL:2 * D_MODEL].astype(F32) * _mm(yb, wb_ref[...])
              + g_ref[:, 2 * D_MODEL:3 * D_MODEL].astype(F32) * _mm(yc_ref[...], wc_ref[...]))
    x1 = x_ref[...] + mod_ref[2:3, :] * _rms(_mm(merged, wo_ref[...]), ng_ref[1:2, :])
    x_o[...] = x1
    h_o[...] = (_rms(x1, ng_ref[2:3, :]) * (1.0 + mod_ref[4:5, :]) + mod_ref[3:4, :]).astype(h_o.dtype)


def _merge(ya, ys, zu, yc, gates, x2d, mod_l, mod_row, ng, ssm_d, w_glu, w_a, w_b, w_c, w_o):
    t = x2d.shape[0]
    tm = TOKEN_TILE
    row = lambda i: (i, 0)
    const = lambda i: (0, 0)
    r512 = pl.BlockSpec((tm, 512), row)
    wbr = pl.BlockSpec((512, D_MODEL), const)
    return pl.pallas_call(
        _merge_kernel,
        grid=(t // tm,),
        in_specs=[
            r512, r512, r512, r512,
            pl.BlockSpec((tm, 3 * D_MODEL), row),
            pl.BlockSpec((tm, D_MODEL), row),
            pl.BlockSpec((None, 6, D_MODEL), lambda i: (mod_row(i, tm), 0, 0)),
            pl.BlockSpec((4, D_MODEL), const),
            pl.BlockSpec((1, 512), const),
            pl.BlockSpec((512, 512), const),
            wbr, wbr, wbr,
            pl.BlockSpec((D_MODEL, D_MODEL), const),
        ],
        out_specs=[pl.BlockSpec((tm, D_MODEL), row), pl.BlockSpec((tm, D_MODEL), row)],
        out_shape=[jax.ShapeDtypeStruct((t, D_MODEL), F32), jax.ShapeDtypeStruct((t, D_MODEL), MXU_DTYPE)],
        compiler_params=_params(("parallel",)),
        name="merge",
    )(ya, ys, zu, yc, gates, x2d, mod_l, ng, ssm_d, w_glu, w_a, w_b, w_c, w_o)


def _ffn_kernel(h_ref, hp_ref, hn_ref, x_ref, mod_ref, ng_ref, wu_ref, cw_ref, cb_ref, wd_ref, out_ref,
                ua_ref, ug_ref, act_ref, *, seq_len):
    i = pl.program_id(0)
    tm = h_ref.shape[0]
    ft = FF_TILE
    n = tm + 16
    start = jnp.bitwise_and(i * tm, seq_len - 1)
    keep_prev = (start != 0).astype(F32)
    keep_next = (jnp.bitwise_and(start + tm, seq_len - 1) != 0).astype(F32)
    hh = jnp.concatenate([(hp_ref[...] * keep_prev).astype(h_ref.dtype), h_ref[...],
                          (hn_ref[...] * keep_next).astype(h_ref.dtype)], axis=0)
    interior = tm > seq_len
    if interior:
        pos = jnp.bitwise_and(lax.broadcasted_iota(jnp.int32, (tm, 1), 0), seq_len - 1)
        has_prev = (pos != 0).astype(F32)
        has_next = (pos != seq_len - 1).astype(F32)

    def conv(u_ref, lo, col):
        uc = u_ref[:, lo:lo + LANE]
        up = pltpu.roll(uc, 1, 0)[8:8 + tm]
        un = pltpu.roll(uc, n - 1, 0)[8:8 + tm]
        if interior:
            up, un = up * has_prev, un * has_next
        return (cw_ref[0:1, col:col + LANE] * up + cw_ref[1:2, col:col + LANE] * uc[8:8 + tm]
                + cw_ref[2:3, col:col + LANE] * un + cb_ref[0:1, col:col + LANE])

    total = None
    for j in range(D_FF // ft):
        a0, g0 = j * ft, D_FF + j * ft
        ua_ref[...] = jnp.dot(hh, wu_ref[:, a0:a0 + ft], preferred_element_type=F32)
        ug_ref[...] = jnp.dot(hh, wu_ref[:, g0:g0 + ft], preferred_element_type=F32)
        for kc in range(ft // LANE):
            a = conv(ua_ref, kc * LANE, a0 + kc * LANE)
            g = conv(ug_ref, kc * LANE, g0 + kc * LANE)
            act_ref[:, kc * LANE:(kc + 1) * LANE] = (g * _sigmoid(g) * a).astype(act_ref.dtype)
        part = jnp.dot(act_ref[...], wd_ref[a0:a0 + ft, :], preferred_element_type=F32)
        total = part if total is None else total + part
    out_ref[...] = x_ref[...] + mod_ref[5:6, :] * _rms(total, ng_ref[3:4, :])


def _conv_ffn(h2, x1, mod_l, mod_row, ng, w_up, conv_w, conv_b, w_down, seq_len):
    t = x1.shape[0]
    tm = FFN_TOKEN_TILE
    ft = FF_TILE
    nblk8 = t // 8
    assert seq_len & (seq_len - 1) == 0 and (seq_len % tm == 0 or tm % seq_len == 0) and t % tm == 0
    assert D_FF % ft == 0 and ft % LANE == 0
    row = pl.BlockSpec((tm, D_MODEL), lambda i: (i, 0))
    once = pl.Buffered(1)
    whole = lambda a: pl.BlockSpec(a.shape, lambda i: (0, 0), pipeline_mode=once)
    return pl.pallas_call(
        functools.partial(_ffn_kernel, seq_len=seq_len),
        grid=(t // tm,),
        in_specs=[
            row,
            pl.BlockSpec((8, D_MODEL), lambda i: (jnp.maximum(i * (tm // 8) - 1, 0), 0)),
            pl.BlockSpec((8, D_MODEL), lambda i: (jnp.minimum((i + 1) * (tm // 8), nblk8 - 1), 0)),
            row,
            pl.BlockSpec((None, 6, D_MODEL), lambda i: (mod_row(i, tm), 0, 0)),
            pl.BlockSpec((4, D_MODEL), lambda i: (0, 0)),
            whole(w_up), whole(conv_w), whole(conv_b), whole(w_down),
        ],
        out_specs=row,
        out_shape=jax.ShapeDtypeStruct((t, D_MODEL), F32),
        scratch_shapes=[pltpu.VMEM((tm + 16, ft), F32), pltpu.VMEM((tm + 16, ft), F32),
                        pltpu.VMEM((tm, ft), MXU_DTYPE)],
        compiler_params=_params(("parallel",)),
        name="conv_ffn",
    )(h2, h2, h2, x1, mod_l, ng, w_up, conv_w, conv_b, w_down)


_Q_HEAD_ORDER = (0, 4, 1, 5, 2, 6, 3, 7)


def _rope_tables(seq_len):
    nf = HEAD_DIM // 4
    t = np.arange(seq_len)
    pos = np.stack([t // GRID_W, t % GRID_W]).astype(np.float32)
    inv = jnp.asarray(ROPE_THETA, F32) ** (-jnp.arange(nf, dtype=F32) / nf)
    ang = jnp.asarray(pos)[:, :, None] * inv
    d = np.arange(HEAD_DIM)
    ang = ang[d // (2 * nf), :, d % nf].T
    second = jnp.asarray(((d % (2 * nf)) // nf) == 1)[None, :]
    cos, sin = jnp.cos(ang), jnp.sin(ang)
    tabs = (cos, jnp.where(second, 0.0, -sin), jnp.where(second, sin, 0.0))
    return tuple(jnp.tile(x, (1, LANE // HEAD_DIM)).astype(F32) for x in tabs)


def _layer_weights(w_in, qk_g, w_br_a):
    hd = HEAD_DIM
    w_in_p = jnp.concatenate([w_in[:, h * hd:(h + 1) * hd] for h in _Q_HEAD_ORDER] + [w_in[:, 512:]],
                             axis=1).astype(MXU_DTYPE)
    w_a_p = jnp.concatenate([w_br_a[h * hd:(h + 1) * hd] for h in _Q_HEAD_ORDER], axis=0).astype(MXU_DTYPE)
    qg = jnp.tile(qk_g[0], N_HEADS).reshape(1, 512).astype(F32)
    kg = jnp.tile(qk_g[1], GA_KV_HEADS).reshape(1, LANE).astype(F32)
    return w_in_p, w_a_p, qg, kg


def kernel(x_prompt, x_sample, c, cache_ga_k, cache_ga_v, cache_na_k, cache_na_v, state_ssm_re, state_ssm_im,
           c_ctx, w_mod, b_mod, norm_g, w_in, qk_norm_g, na_rpb, ssm_lam_re, ssm_lam_im, ssm_log_step,
           ssm_b_re, ssm_b_im, ssm_c_re, ssm_c_im, ssm_d, w_glu, w_br_a, w_br_b, w_br_c, w_out,
           w_up, conv_w, conv_b, w_down):
    depth = w_in.shape[0]
    bp, lp, _ = x_prompt.shape
    bs, ls, _ = x_sample.shape
    lc = cache_ga_k.shape[2]
    assert lp % 256 == 0 and ls % FFN_TOKEN_TILE == 0 and (bp * lp) % FFN_TOKEN_TILE == 0
    assert FFN_TOKEN_TILE % TOKEN_TILE == 0
    assert bs % 8 == 0 and bp % 8 == 0, "the scan keeps one batch row per sublane"

    rows = 1 + bs
    rows_p = -(-rows // 8) * 8
    cvec = jnp.concatenate([c_ctx[None], c, jnp.zeros((rows_p - rows, D_MODEL), F32)], axis=0)
    mod = _modulation(cvec, w_mod, b_mod).reshape(depth, rows_p, 6, D_MODEL)

    w_t, w_b, w_c, l16 = _ssm_operators(ssm_lam_re, ssm_lam_im, ssm_log_step, ssm_b_re, ssm_b_im,
                                        ssm_c_re, ssm_c_im)
    seg = jnp.asarray(np.kron(np.eye(N_HEADS), np.full((HEAD_DIM, HEAD_DIM), 1.0 / HEAD_DIM)), MXU_DTYPE)
    rope_tabs = _rope_tables(ls)
    ctx_row = lambda i, tm: 0
    lat_row = lambda i, tm: 1 + (i * tm) // ls

    y_p = x_prompt.reshape(bp * lp, D_MODEL)
    y_s = x_sample.reshape(bs * ls, D_MODEL)
    zero_state = jnp.zeros((SSM_PAIRS, bp, 512), F32)
    cache_ga = [a.reshape(bs, depth, lc, LANE) for a in (cache_ga_k, cache_ga_v)]
    cache_na = [a.reshape(bs, depth, lc, 512) for a in (cache_na_k, cache_na_v)]
    states = ([], [])
    kv_stacks = None
    for l in range(depth):
        w_in_p, w_a_p, qg, kg = _layer_weights(w_in[l], qk_norm_g[l], w_br_a[l])
        w_glu_l, w_b_l, w_c_l, w_o_l = (a[l].astype(MXU_DTYPE) for a in (w_glu, w_br_b, w_br_c, w_out))
        ffn_w = (w_up[l].astype(MXU_DTYPE), conv_w[l].astype(F32), conv_b[l].reshape(1, 2 * D_FF).astype(F32),
                 w_down[l].astype(MXU_DTYPE))
        d_l = ssm_d[l].reshape(1, SSM_WIDTH).astype(F32)
        ng = norm_g[l].astype(F32)
        ssm_ops = (w_t[l], w_b[l], w_c[l], l16[l])

        q, k, v, zu, nq, nk, nv, gates = _in_projection(
            y_p, mod[l], ctx_row, ng, w_in_p, qg, kg, seg, None, lp, F32, stack=(l, depth, kv_stacks))
        kv_stacks = (k, v, nk, nv)
        r3 = lambda a: a.reshape(bp, lp, a.shape[-1])
        ya = _attention(r3(q), k, v, None, None, lp, "ga_ctx", kv_layer=l)
        yc = _attention(r3(nq), nk, nv, None, None, lp, "na_ctx", kv_layer=l)
        ys, fin = _ssm_scan(zu, *ssm_ops, zero_state, bp, lp)
        x1, h2 = _merge(ya.reshape(-1, 512), ys, zu, yc.reshape(-1, 512), gates, y_p,
                        mod[l], ctx_row, ng, d_l, w_glu_l, w_a_p, w_b_l, w_c_l, w_o_l)
        y_p = _conv_ffn(h2, x1, mod[l], ctx_row, ng, *ffn_w, lp)
        f_re, f_im = _unpack_state(fin)
        states[0].append(f_re)
        states[1].append(f_im)

        q, k, v, zu, nq, nk, nv, gates = _in_projection(
            y_s, mod[l], lat_row, ng, w_in_p, qg, kg, seg, rope_tabs, ls, MXU_DTYPE)
        r3 = lambda a: a.reshape(bs, ls, a.shape[-1])
        ya = _attention(r3(q), r3(k), r3(v), cache_ga[0], cache_ga[1], 4 * GRID_W, "ga_lat", cache_layer=l)
        yc = _neighbourhood_attention(r3(nq), r3(nk), r3(nv), cache_na[0], cache_na[1], l, na_rpb[l])
        h0 = _pack_state(state_ssm_re[:, l], state_ssm_im[:, l])
        ys, _ = _ssm_scan(zu, *ssm_ops, h0, bs, ls)
        x1, h2 = _merge(ya.reshape(-1, 512), ys, zu, yc.reshape(-1, 512), gates, y_s,
                        mod[l], lat_row, ng, d_l, w_glu_l, w_a_p, w_b_l, w_c_l, w_o_l)
        y_s = _conv_ffn(h2, x1, mod[l], lat_row, ng, *ffn_w, ls)

    k, v, nk, nv = kv_stacks
    return (y_p.reshape(bp, lp, D_MODEL), y_s.reshape(bs, ls, D_MODEL),
            k.reshape(bp, depth, lp, GA_KV_HEADS, HEAD_DIM), v.reshape(bp, depth, lp, GA_KV_HEADS, HEAD_DIM),
            nk.reshape(bp, depth, lp, N_HEADS, HEAD_DIM), nv.reshape(bp, depth, lp, N_HEADS, HEAD_DIM),
            jnp.stack(states[0], axis=1), jnp.stack(states[1], axis=1))
```
